```python
import jax, jax.numpy as jnp
from jax import lax
import numpy as np

D_MODEL = 2048
BATCH = 1
SEQ = 8192
DEPTH = 2

EPS = 1e-5
N_MEM = 256
D_FF = 5632
GM_CHUNK = 128
GM_GROUPS = 4
GM_WIDTH = D_MODEL
GM_GDIM = GM_WIDTH // GM_GROUPS
SSD_WIDTH = D_MODEL
SSD_HEAD_DIM = 64
SSD_HEADS = SSD_WIDTH // SSD_HEAD_DIM
SSD_GROUPS = 4
SSD_HPG = SSD_HEADS // SSD_GROUPS
SSD_STATE = 128
SSD_CONV = 4
SSD_CHUNK = 128
SSD_BC = SSD_GROUPS * SSD_STATE
SSD_CONV_DIM = SSD_WIDTH + 2 * SSD_BC
EVEN_IN = 2 * GM_WIDTH + SSD_WIDTH + SSD_CONV_DIM + SSD_HEADS
EVEN_MIX = GM_WIDTH + SSD_WIDTH
ATT_HEADS = 32
ATT_KV_HEADS = 4
ATT_HEAD_DIM = D_MODEL // ATT_HEADS
ATT_REP = ATT_HEADS // ATT_KV_HEADS
WINDOW = 128
ATT_SCALE = ATT_HEAD_DIM ** -0.5
ROT_DIM = ATT_HEAD_DIM // 4
ROPE_THETA = 500000.0
ODD_IN = (ATT_HEADS + 2 * ATT_KV_HEADS) * ATT_HEAD_DIM
X_HEADS = 4
X_HEAD_DIM = 128
X_WIDTH = X_HEADS * X_HEAD_DIM
X_SCALE = X_HEAD_DIM ** -0.5
N_EVEN = (DEPTH + 1) // 2
N_ODD = DEPTH // 2

kernel_name = 'hybrid_gmlp_ssd_swa_macaron'


def rmsnorm(x, g):
    xf = x.astype(jnp.float32)
    y = xf * lax.rsqrt(jnp.mean(xf * xf, -1, keepdims=True) + EPS)
    return (y * g.astype(jnp.float32)).astype(x.dtype)


def swiglu(h, w_gu, w_down):
    g, u = jnp.split(h @ w_gu, 2, axis=-1)
    return (jax.nn.silu(g) * u) @ w_down


def chunked_gmlp(uv, ln_g, ln_b, w_s, b_s):
    bsz, L, _ = uv.shape
    nc = L // GM_CHUNK
    u, v = jnp.split(jax.nn.gelu(uv, approximate=False), 2, axis=-1)
    vf = v.reshape(bsz, nc, GM_CHUNK, GM_GROUPS, GM_GDIM).astype(jnp.float32)
    mu = jnp.mean(vf, -1, keepdims=True)
    var = jnp.mean(jnp.square(vf - mu), -1, keepdims=True)
    vn = ((vf - mu) * lax.rsqrt(var + EPS)).astype(v.dtype)
    vn = vn * ln_g.reshape(GM_GROUPS, GM_GDIM) + ln_b.reshape(GM_GROUPS, GM_GDIM)
    causal = jnp.tril(jnp.ones((GM_CHUNK, GM_CHUNK), dtype=bool))
    ws = jnp.where(causal[None], w_s, 0.0)
    s = jnp.einsum('gij,bcjgd->bcigd', ws, vn) + b_s.T[None, None, :, :, None]
    return u * s.reshape(bsz, L, GM_WIDTH)


def causal_dwconv(x, w, b):
    K = w.shape[0]
    L = x.shape[1]
    xp = jnp.pad(x, ((0, 0), (K - 1, 0), (0, 0)))
    y = xp[:, 0:L] * w[0]
    for k in range(1, K):
        y = y + xp[:, k:k + L] * w[k]
    return y + b


def ssd_scan(xs, dt, A, Bm, Cm):
    bsz, L = xs.shape[:2]
    nc = L // SSD_CHUNK
    Q, G, R, P, N = SSD_CHUNK, SSD_GROUPS, SSD_HPG, SSD_HEAD_DIM, SSD_STATE
    x = xs.astype(jnp.float32).reshape(bsz, nc, Q, G, R, P)
    dtc = dt.reshape(bsz, nc, Q, G, R)
    Bc = Bm.astype(jnp.float32).reshape(bsz, nc, Q, G, N)
    Cc = Cm.astype(jnp.float32).reshape(bsz, nc, Q, G, N)
    a = jnp.moveaxis(dtc * A.reshape(G, R), 2, -1)
    a_cs = jnp.cumsum(a, axis=-1)
    xdt = x * dtc[..., None]
    causal = jnp.tril(jnp.ones((Q, Q), dtype=bool))
    seg = a_cs[..., :, None] - a_cs[..., None, :]
    Lmat = jnp.where(causal, jnp.exp(jnp.where(causal, seg, 0.0)), 0.0)
    cb = jnp.einsum('bcign,bcjgn->bcgij', Cc, Bc)
    y_diag = jnp.einsum('bcgij,bcgrij,bcjgrp->bcigrp', cb, Lmat, xdt)
    decay_states = jnp.exp(a_cs[..., -1:] - a_cs)
    states = jnp.einsum('bcjgn,bcgrj,bcjgrp->bcgrpn', Bc, decay_states, xdt)
    chunk_decay = jnp.exp(a_cs[..., -1])

    def step(h, inp):
        s_c, d_c = inp
        return h * d_c[..., None, None] + s_c, h

    h0 = jnp.zeros((bsz, G, R, P, N), jnp.float32)
    _, prev = lax.scan(step, h0, (jnp.moveaxis(states, 1, 0), jnp.moveaxis(chunk_decay, 1, 0)))
    prev = jnp.moveaxis(prev, 0, 1)
    y_off = jnp.einsum('bcign,bcgrpn,bcgri->bcigrp', Cc, prev, jnp.exp(a_cs))
    return (y_diag + y_off).reshape(bsz, L, G * R * P)


def even_mixer(h, w_in, gm_ln_g, gm_ln_b, gm_ws, gm_bs, conv_w, conv_b, dt_bias, a_log, d_skip, ssd_norm, w_out):
    bsz, L, _ = h.shape
    proj = h @ w_in
    c0 = 2 * GM_WIDTH
    c1 = c0 + SSD_WIDTH
    c2 = c1 + SSD_CONV_DIM
    uv, z, xbc, dt_raw = jnp.split(proj, [c0, c1, c2], axis=-1)
    a_out = chunked_gmlp(uv, gm_ln_g, gm_ln_b, gm_ws, gm_bs)
    xbc = jax.nn.silu(causal_dwconv(xbc, conv_w, conv_b))
    xs, Bm, Cm = jnp.split(xbc, [SSD_WIDTH, SSD_WIDTH + SSD_BC], axis=-1)
    dt = jax.nn.softplus(dt_raw.astype(jnp.float32) + dt_bias.astype(jnp.float32))
    A = -jnp.exp(a_log.astype(jnp.float32))
    xh = xs.reshape(bsz, L, SSD_HEADS, SSD_HEAD_DIM)
    y = ssd_scan(xh, dt, A, Bm.reshape(bsz, L, SSD_GROUPS, SSD_STATE), Cm.reshape(bsz, L, SSD_GROUPS, SSD_STATE))
    y = y + (xh.astype(jnp.float32) * d_skip.astype(jnp.float32)[:, None]).reshape(bsz, L, SSD_WIDTH)
    yg = (y * jax.nn.silu(z.astype(jnp.float32))).reshape(bsz, L, SSD_GROUPS, SSD_WIDTH // SSD_GROUPS)
    yg = yg * lax.rsqrt(jnp.mean(yg * yg, -1, keepdims=True) + EPS)
    b_out = (yg.reshape(bsz, L, SSD_WIDTH) * ssd_norm.astype(jnp.float32)).astype(h.dtype)
    return jnp.concatenate([a_out, b_out], axis=-1) @ w_out


def rope_partial(x, cos, sin):
    half = ROT_DIM // 2
    x1 = x[..., :half]
    x2 = x[..., half:ROT_DIM]
    return jnp.concatenate([x1 * cos - x2 * sin, x2 * cos + x1 * sin, x[..., ROT_DIM:]], axis=-1)


def swa_sinks(h, w_qkv, b_qkv, sinks, w_o, cos, sin):
    bsz, L, _ = h.shape
    nb = L // WINDOW
    W = WINDOW
    qkv = h @ w_qkv + b_qkv
    q, k, v = jnp.split(qkv, [ATT_HEADS * ATT_HEAD_DIM, (ATT_HEADS + ATT_KV_HEADS) * ATT_HEAD_DIM], axis=-1)
    q = rope_partial(q.reshape(bsz, L, ATT_HEADS, ATT_HEAD_DIM), cos, sin)
    k = rope_partial(k.reshape(bsz, L, ATT_KV_HEADS, ATT_HEAD_DIM), cos, sin)
    q = q.reshape(bsz, nb, W, ATT_KV_HEADS, ATT_REP, ATT_HEAD_DIM)
    kb = k.reshape(bsz, nb, W, ATT_KV_HEADS, ATT_HEAD_DIM)
    vb = v.reshape(bsz, nb, W, ATT_KV_HEADS, ATT_HEAD_DIM)
    pad = ((0, 0), (1, 0), (0, 0), (0, 0), (0, 0))
    kcat = jnp.concatenate([jnp.pad(kb, pad)[:, :-1], kb], axis=2)
    vcat = jnp.concatenate([jnp.pad(vb, pad)[:, :-1], vb], axis=2)
    s = jnp.einsum('bnqkrd,bnskd->bnkrqs', q, kcat).astype(jnp.float32) * ATT_SCALE
    iq = jnp.arange(W)[:, None]
    js = jnp.arange(2 * W)[None, :]
    rel = iq + W - js
    band = (rel >= 0) & (rel < WINDOW)
    blk = jnp.arange(nb)[:, None, None]
    mask = band[None] & ((blk > 0) | (js >= W)[None])
    s = jnp.where(mask[None, :, None, None], s, -jnp.inf)
    sink = sinks.astype(jnp.float32).reshape(ATT_KV_HEADS, ATT_REP)[None, None, :, :, None, None]
    m = jnp.maximum(jnp.max(s, -1, keepdims=True), sink)
    p = jnp.exp(s - m)
    pr = (p / (jnp.sum(p, -1, keepdims=True) + jnp.exp(sink - m))).astype(vcat.dtype)
    o = jnp.einsum('bnkrqs,bnskd->bnqkrd', pr, vcat).reshape(bsz, L, ATT_HEADS * ATT_HEAD_DIM)
    return o @ w_o


def mem_cross_attn(h, mem_n, w_q, w_kv, w_o):
    bsz, L, _ = h.shape
    q = (h @ w_q).reshape(bsz, L, X_HEADS, X_HEAD_DIM)
    k, v = jnp.split(mem_n @ w_kv, 2, axis=-1)
    k = k.reshape(bsz, -1, X_HEADS, X_HEAD_DIM)
    v = v.reshape(bsz, -1, X_HEADS, X_HEAD_DIM)
    s = jnp.einsum('blhd,bmhd->bhlm', q, k).astype(jnp.float32) * X_SCALE
    p = jax.nn.softmax(s, axis=-1).astype(v.dtype)
    o = jnp.einsum('bhlm,bmhd->blhd', p, v).reshape(bsz, L, X_WIDTH)
    return o @ w_o


def setup_inputs(seed: int = 0) -> dict:
    key = jax.random.key(seed)
    ks = list(jax.random.split(key, 40))
    f32 = jnp.float32

    def nrm(i, shape, scale):
        return jax.random.normal(ks[i], shape, f32) * scale

    def gain(i, shape):
        return 1.0 + 0.02 * jax.random.normal(ks[i], shape, f32)

    x = nrm(0, (BATCH, SEQ, D_MODEL), 1.0)
    mem = nrm(1, (BATCH, N_MEM, D_MODEL), 1.0)
    start = jax.random.randint(ks[2], (BATCH, 1), 0, 4096, dtype=jnp.int32)
    positions = start + jnp.arange(SEQ, dtype=jnp.int32)[None, :]
    dt0 = jnp.exp(jax.random.uniform(ks[20], (N_EVEN, SSD_HEADS), f32, np.log(1e-3), np.log(1e-1)))
    return {
        'x': x,
        'mem': mem,
        'positions': positions,
        'norm_ffn1': gain(3, (DEPTH, D_MODEL)),
        'w_ffn1_gu': nrm(4, (DEPTH, D_MODEL, 2 * D_FF), D_MODEL ** -0.5),
        'w_ffn1_down': nrm(5, (DEPTH, D_FF, D_MODEL), D_FF ** -0.5),
        'norm_mix': gain(6, (DEPTH, D_MODEL)),
        'w_in_even': nrm(7, (N_EVEN, D_MODEL, EVEN_IN), D_MODEL ** -0.5),
        'gm_ln_g': gain(8, (N_EVEN, GM_WIDTH)),
        'gm_ln_b': nrm(9, (N_EVEN, GM_WIDTH), 0.02),
        'gm_ws': nrm(10, (N_EVEN, GM_GROUPS, GM_CHUNK, GM_CHUNK), GM_CHUNK ** -0.5),
        'gm_bs': 1.0 + nrm(11, (N_EVEN, GM_GROUPS, GM_CHUNK), 0.1),
        'conv_w': nrm(12, (N_EVEN, SSD_CONV, SSD_CONV_DIM), SSD_CONV ** -0.5),
        'conv_b': nrm(13, (N_EVEN, SSD_CONV_DIM), 0.02),
        'dt_bias': dt0 + jnp.log(-jnp.expm1(-dt0)),
        'a_log': jnp.log(jax.random.uniform(ks[14], (N_EVEN, SSD_HEADS), f32, 1.0, 16.0)),
        'd_skip': 1.0 + nrm(15, (N_EVEN, SSD_HEADS), 0.1),
        'ssd_norm': gain(16, (N_EVEN, SSD_WIDTH)),
        'w_out_even': nrm(17, (N_EVEN, EVEN_MIX, D_MODEL), EVEN_MIX ** -0.5),
        'w_qkv': nrm(18, (N_ODD, D_MODEL, ODD_IN), D_MODEL ** -0.5),
        'b_qkv': nrm(19, (N_ODD, ODD_IN), 0.02),
        'sinks': nrm(21, (N_ODD, ATT_HEADS), 0.5),
        'w_o_odd': nrm(22, (N_ODD, ATT_HEADS * ATT_HEAD_DIM, D_MODEL), (ATT_HEADS * ATT_HEAD_DIM) ** -0.5),
        'norm_xq': gain(23, (DEPTH, D_MODEL)),
        'norm_mem': gain(24, (DEPTH, D_MODEL)),
        'w_xq': nrm(25, (DEPTH, D_MODEL, X_WIDTH), D_MODEL ** -0.5),
        'w_xkv': nrm(26, (DEPTH, D_MODEL, 2 * X_WIDTH), D_MODEL ** -0.5),
        'w_xo': nrm(27, (DEPTH, X_WIDTH, D_MODEL), X_WIDTH ** -0.5),
        'norm_ffn2': gain(28, (DEPTH, D_MODEL)),
        'w_ffn2_gu': nrm(29, (DEPTH, D_MODEL, 2 * D_FF), D_MODEL ** -0.5),
        'w_ffn2_down': nrm(30, (DEPTH, D_FF, D_MODEL), D_FF ** -0.5),
        'final_norm': gain(31, (D_MODEL,)),
    }


def reference(x, mem, positions, norm_ffn1, w_ffn1_gu, w_ffn1_down, norm_mix, w_in_even, gm_ln_g, gm_ln_b,
              gm_ws, gm_bs, conv_w, conv_b, dt_bias, a_log, d_skip, ssd_norm, w_out_even, w_qkv, b_qkv, sinks,
              w_o_odd, norm_xq, norm_mem, w_xq, w_xkv, w_xo, norm_ffn2, w_ffn2_gu, w_ffn2_down, final_norm):
    inv_freq = ROPE_THETA ** (-jnp.arange(0, ROT_DIM, 2, dtype=jnp.float32) / ROT_DIM)
    ang = positions.astype(jnp.float32)[..., None] * inv_freq
    cos = jnp.cos(ang)[:, :, None, :].astype(x.dtype)
    sin = jnp.sin(ang)[:, :, None, :].astype(x.dtype)
    for i in range(DEPTH):
        j = i // 2
        x = x + 0.5 * swiglu(rmsnorm(x, norm_ffn1[i]), w_ffn1_gu[i], w_ffn1_down[i])
        h = rmsnorm(x, norm_mix[i])
        if i % 2 == 0:
            x = x + even_mixer(h, w_in_even[j], gm_ln_g[j], gm_ln_b[j], gm_ws[j], gm_bs[j], conv_w[j], conv_b[j],
                               dt_bias[j], a_log[j], d_skip[j], ssd_norm[j], w_out_even[j])
        else:
            x = x + swa_sinks(h, w_qkv[j], b_qkv[j], sinks[j], w_o_odd[j], cos, sin)
        x = x + mem_cross_attn(rmsnorm(x, norm_xq[i]), rmsnorm(mem, norm_mem[i]), w_xq[i], w_xkv[i], w_xo[i])
        x = x + 0.5 * swiglu(rmsnorm(x, norm_ffn2[i]), w_ffn2_gu[i], w_ffn2_down[i])
    return rmsnorm(x, final_norm)
```

```python
import functools

import jax
import jax.numpy as jnp
from jax import lax
from jax.experimental import pallas as pl
from jax.experimental.pallas import tpu as pltpu

F32 = jnp.float32
BF16 = jnp.bfloat16

D_MODEL = 2048
SEQ = 8192
DEPTH = 2
EPS = 1e-5
N_MEM = 256
D_FF = 5632
CHUNK = 128
GROUPS = 4
GDIM = D_MODEL // GROUPS
HEAD_DIM = 64
SSD_HEADS = 32
SSD_STATE = 128
SSD_CONV = 4
CONV_DIM = D_MODEL + 2 * GROUPS * SSD_STATE
EVEN_MAIN = 2 * D_MODEL + D_MODEL + CONV_DIM
ATT_HEADS = 32
ATT_REP = ATT_HEADS // GROUPS
ATT_SCALE = HEAD_DIM ** -0.5
ROT_DIM = HEAD_DIM // 4
ROPE_THETA = 500000.0
QK_COLS = (ATT_HEADS + GROUPS) * HEAD_DIM
ODD_IN = (ATT_HEADS + 2 * GROUPS) * HEAD_DIM
X_HEADS = 4
X_HEAD_DIM = 128
X_WIDTH = X_HEADS * X_HEAD_DIM
X_SCALE = X_HEAD_DIM ** -0.5

LANES = 128
CONV_TAIL = 16
BM = 1024
BN = 512
MIB = 1024 * 1024


def _params(semantics, vmem_mib):
    return pltpu.CompilerParams(dimension_semantics=semantics, vmem_limit_bytes=vmem_mib * MIB)


def _rms(x, g):
    ms = jnp.mean(x * x, axis=-1, keepdims=True)
    return x * lax.rsqrt(ms + EPS) * g


def _silu(x):
    return x * jax.nn.sigmoid(x)


def _gelu(x):
    return 0.5 * x * (1.0 + lax.erf(x * (2.0 ** -0.5)))


def _dot(a, b):
    return jnp.dot(a, b, preferred_element_type=F32)


def _dot_nt(a, b):
    return lax.dot_general(a, b, (((1,), (1,)), ((), ())), preferred_element_type=F32)


def _dot_tn(a, b):
    return lax.dot_general(a, b, (((0,), (0,)), ((), ())), preferred_element_type=F32)


def _ffn_up_body(x_ref, g_ref, wg_ref, wu_ref, o_ref, h_ref):
    @pl.when(pl.program_id(1) == 0)
    def _():
        h_ref[...] = _rms(x_ref[...], g_ref[...]).astype(BF16)

    h = h_ref[...]
    g = _dot(h, wg_ref[...])
    u = _dot(h, wu_ref[...])
    o_ref[...] = (_silu(g) * u).astype(BF16)


def _ffn_up(x, gain, w_gu):
    m = x.shape[0]
    nj = D_FF // BN
    return pl.pallas_call(
        _ffn_up_body,
        grid=(m // BM, nj),
        in_specs=[
            pl.BlockSpec((BM, D_MODEL), lambda i, j: (i, 0)),
            pl.BlockSpec((1, D_MODEL), lambda i, j: (0, 0)),
            pl.BlockSpec((D_MODEL, BN), lambda i, j: (0, j)),
            pl.BlockSpec((D_MODEL, BN), lambda i, j: (0, j + nj)),
        ],
        out_specs=pl.BlockSpec((BM, BN), lambda i, j: (i, j)),
        out_shape=jax.ShapeDtypeStruct((m, D_FF), BF16),
        scratch_shapes=[pltpu.VMEM((BM, D_MODEL), BF16)],
        compiler_params=_params(("parallel", "arbitrary"), 48),
        name="ffn_up",
    )(x, gain, w_gu, w_gu)


def _mm_res_body(a_ref, w_ref, r_ref, o_ref, *, scale):
    o_ref[...] = r_ref[...] + scale * _dot(a_ref[...], w_ref[...])


def _mm_res(a, w, res, scale, name):
    m, k = a.shape
    n = w.shape[1]
    return pl.pallas_call(
        functools.partial(_mm_res_body, scale=scale),
        grid=(m // BM, n // BN),
        in_specs=[
            pl.BlockSpec((BM, k), lambda i, j: (i, 0)),
            pl.BlockSpec((k, BN), lambda i, j: (0, j)),
            pl.BlockSpec((BM, BN), lambda i, j: (i, j)),
        ],
        out_specs=pl.BlockSpec((BM, BN), lambda i, j: (i, j)),
        out_shape=jax.ShapeDtypeStruct((m, n), F32),
        compiler_params=_params(("parallel", "arbitrary"), 56),
        name=name,
    )(a, w, res)


def _norm_mm_body(x_ref, g_ref, w_ref, o_ref, h_ref):
    @pl.when(pl.program_id(1) == 0)
    def _():
        h_ref[...] = _rms(x_ref[...], g_ref[...]).astype(BF16)

    o_ref[...] = _dot(h_ref[...], w_ref[...]).astype(o_ref.dtype)


def _norm_mm(x, gain, w, bm, out_dtype, name):
    m, k = x.shape
    n = w.shape[1]
    return pl.pallas_call(
        _norm_mm_body,
        grid=(m // bm, n // BN),
        in_specs=[
            pl.BlockSpec((bm, k), lambda i, j: (i, 0)),
            pl.BlockSpec((1, k), lambda i, j: (0, 0)),
            pl.BlockSpec((k, BN), lambda i, j: (0, j)),
        ],
        out_specs=pl.BlockSpec((bm, BN), lambda i, j: (i, j)),
        out_shape=jax.ShapeDtypeStruct((m, n), out_dtype),
        scratch_shapes=[pltpu.VMEM((bm, k), BF16)],
        compiler_params=_params(("parallel", "arbitrary"), 40),
        name=name,
    )(x, gain, w)


def _even_in_body(x_ref, g_ref, w_ref, wdt_ref, o_ref, dt_ref, h_ref, *, n_gelu):
    j = pl.program_id(1)

    @pl.when(j == 0)
    def _():
        h = _rms(x_ref[...], g_ref[...]).astype(BF16)
        h_ref[...] = h
        dt_ref[...] = _dot(h, wdt_ref[...])

    acc = _dot(h_ref[...], w_ref[...])

    @pl.when(j < n_gelu)
    def _():
        o_ref[...] = _gelu(acc).astype(BF16)

    @pl.when(j >= n_gelu)
    def _():
        o_ref[...] = acc.astype(BF16)


def _even_in(x, gain, w_main, w_dt):
    m = x.shape[0]
    return pl.pallas_call(
        functools.partial(_even_in_body, n_gelu=2 * D_MODEL // BN),
        grid=(m // BM, EVEN_MAIN // BN),
        in_specs=[
            pl.BlockSpec((BM, D_MODEL), lambda i, j: (i, 0)),
            pl.BlockSpec((1, D_MODEL), lambda i, j: (0, 0)),
            pl.BlockSpec((D_MODEL, BN), lambda i, j: (0, j)),
            pl.BlockSpec((D_MODEL, LANES), lambda i, j: (0, 0)),
        ],
        out_specs=[
            pl.BlockSpec((BM, BN), lambda i, j: (i, j)),
            pl.BlockSpec((BM, LANES), lambda i, j: (i, 0)),
        ],
        out_shape=[
            jax.ShapeDtypeStruct((m, EVEN_MAIN), BF16),
            jax.ShapeDtypeStruct((m, LANES), F32),
        ],
        scratch_shapes=[pltpu.VMEM((BM, D_MODEL), BF16)],
        compiler_params=_params(("parallel", "arbitrary"), 40),
        name="even_in",
    )(x, gain, w_main, w_dt)


def _qkv_body(x_ref, g_ref, w_ref, b_ref, pos_ref, invf_ref, o_ref, h_ref, cos_ref, sin_ref):
    j = pl.program_id(1)
    lane = lax.broadcasted_iota(jnp.int32, (1, LANES), 1) % HEAD_DIM
    first = lane < ROT_DIM // 2
    second = (lane >= ROT_DIM // 2) & (lane < ROT_DIM)

    @pl.when(j == 0)
    def _():
        h_ref[...] = _rms(x_ref[...], g_ref[...]).astype(BF16)
        ang = pos_ref[...].astype(F32) * invf_ref[...]
        sin = jnp.sin(ang)
        cos_ref[...] = jnp.where(first | second, jnp.cos(ang), 1.0)
        sin_ref[...] = jnp.where(first, -sin, jnp.where(second, sin, 0.0))

    acc = _dot(h_ref[...], w_ref[...]) + b_ref[...]

    def rope(a):
        reps = a.shape[1] // LANES
        c = jnp.concatenate([cos_ref[...]] * reps, axis=1)
        s = jnp.concatenate([sin_ref[...]] * reps, axis=1)
        half = ROT_DIM // 2
        take_next = lax.broadcasted_iota(jnp.int32, (1, a.shape[1]), 1) % HEAD_DIM < half
        partner = jnp.where(take_next, pltpu.roll(a, a.shape[1] - half, 1), pltpu.roll(a, half, 1))
        return a * c + partner * s

    n_q = ATT_HEADS * HEAD_DIM // BN

    @pl.when(j < n_q)
    def _():
        o_ref[...] = (rope(acc) * ATT_SCALE).astype(BF16)

    @pl.when(j >= n_q)
    def _():
        kw = GROUPS * HEAD_DIM
        o_ref[...] = jnp.concatenate([rope(acc[:, :kw]), acc[:, kw:]], axis=1).astype(BF16)


def _qkv(x, gain, w, b, pos, invf):
    m = x.shape[0]
    return pl.pallas_call(
        _qkv_body,
        grid=(m // BM, ODD_IN // BN),
        in_specs=[
            pl.BlockSpec((BM, D_MODEL), lambda i, j: (i, 0)),
            pl.BlockSpec((1, D_MODEL), lambda i, j: (0, 0)),
            pl.BlockSpec((D_MODEL, BN), lambda i, j: (0, j)),
            pl.BlockSpec((1, BN), lambda i, j: (0, j)),
            pl.BlockSpec((BM, 1), lambda i, j: (i, 0)),
            pl.BlockSpec((1, LANES), lambda i, j: (0, 0)),
        ],
        out_specs=pl.BlockSpec((BM, BN), lambda i, j: (i, j)),
        out_shape=jax.ShapeDtypeStruct((m, ODD_IN), BF16),
        scratch_shapes=[
            pltpu.VMEM((BM, D_MODEL), BF16),
            pltpu.VMEM((BM, LANES), F32),
            pltpu.VMEM((BM, LANES), F32),
        ],
        compiler_params=_params(("parallel", "arbitrary"), 40),
        name="qkv_rope",
    )(x, gain, w, b, pos, invf)


def _split3(x):
    hi = x.astype(BF16)
    r1 = x - hi.astype(F32)
    mid = r1.astype(BF16)
    lo = (r1 - mid.astype(F32)).astype(BF16)
    return hi, mid, lo


def _even_mix_body(u_ref, v_ref, z_ref, xbc_ref, dt_ref, lng_ref, lnb_ref, ws_ref, bs_ref, cw_ref, cb_ref,
                   dtb_ref, alog_ref, dskip_ref, snorm_ref, e3_ref, o_ref, state_ref, xpad_ref):
    c = pl.program_id(0)
    q = CHUNK

    @pl.when(c == 0)
    def _():
        state_ref[...] = jnp.zeros_like(state_ref)
        xpad_ref[pl.ds(0, CONV_TAIL), :] = jnp.zeros((CONV_TAIL, CONV_DIM), F32)

    row = lax.broadcasted_iota(jnp.int32, (q, q), 0)
    col = lax.broadcasted_iota(jnp.int32, (q, q), 1)
    causal = col <= row

    for g in range(GROUPS):
        seg = slice(g * GDIM, (g + 1) * GDIM)
        vg = v_ref[:, seg].astype(F32)
        mu = jnp.mean(vg, axis=-1, keepdims=True)
        d = vg - mu
        var = jnp.mean(d * d, axis=-1, keepdims=True)
        vn = d * lax.rsqrt(var + EPS) * lng_ref[:, seg] + lnb_ref[:, seg]
        w = jnp.where(causal, ws_ref[g], 0.0).astype(BF16)
        s = _dot(w, vn.astype(BF16)) + bs_ref[:, g:g + 1]
        o_ref[:, seg] = (u_ref[:, seg].astype(F32) * s).astype(BF16)

    xpad_ref[pl.ds(CONV_TAIL, q), :] = xbc_ref[...].astype(F32)
    conv = cb_ref[...] + cw_ref[SSD_CONV - 1:SSD_CONV, :] * xpad_ref[pl.ds(CONV_TAIL, q), :]
    for k in range(SSD_CONV - 1):
        conv = conv + cw_ref[k:k + 1, :] * xpad_ref[pl.ds(CONV_TAIL - (SSD_CONV - 1) + k, q), :]
    xpad_ref[pl.ds(0, CONV_TAIL), :] = xpad_ref[pl.ds(q, CONV_TAIL), :]
    xbc = _silu(conv)
    xs = xbc[:, :D_MODEL]

    dt = jax.nn.softplus(dt_ref[...] + dtb_ref[...])
    a = dt * (-jnp.exp(alog_ref[...]))
    tri = jnp.where(causal, 1.0, 0.0).astype(BF16)
    a_hi, a_mid, a_lo = _split3(a)
    acs = _dot(tri, a_hi) + _dot(tri, a_mid) + _dot(tri, a_lo)
    acs_t = acs.T
    both = jnp.concatenate([dt, acs], axis=0)
    b_hi, b_mid, b_lo = _split3(both)
    both_e = _dot(jnp.concatenate([b_hi, b_mid, b_lo], axis=1), e3_ref[...])
    dt_e = both_e[:q]
    acs_e = both_e[q:]
    last_e = acs_e[q - 1:q, :]
    xdt = xs * dt_e
    xdec = (xdt * jnp.exp(last_e - acs_e)).astype(BF16)
    xdt_b = xdt.astype(BF16)
    grow_e = jnp.exp(acs_e)
    chunk_decay = jnp.exp(last_e)

    lane = lax.broadcasted_iota(jnp.int32, (q, LANES), 1)
    lo_half = lane < HEAD_DIM
    zf = z_ref[...].astype(F32)
    gate = _silu(zf)

    for g in range(GROUPS):
        seg = slice(g * GDIM, (g + 1) * GDIM)
        b_g = xbc[:, D_MODEL + g * SSD_STATE:D_MODEL + (g + 1) * SSD_STATE].astype(BF16)
        c_g = xbc[:, D_MODEL + GROUPS * SSD_STATE + g * SSD_STATE:
                  D_MODEL + GROUPS * SSD_STATE + (g + 1) * SSD_STATE].astype(BF16)
        cb = _dot_nt(c_g, b_g)
        y_off = _dot(c_g, state_ref[g].astype(BF16)) * grow_e[:, seg]
        pieces = []
        for p in range(GDIM // LANES):
            mats = []
            for hh in range(2):
                h = g * (GDIM // HEAD_DIM) + 2 * p + hh
                seg_ij = acs[:, h:h + 1] - acs_t[h:h + 1, :]
                lmat = jnp.where(causal, jnp.exp(jnp.where(causal, seg_ij, 0.0)), 0.0)
                mats.append((cb * lmat).astype(BF16))
            x2 = xdt_b[:, g * GDIM + p * LANES:g * GDIM + (p + 1) * LANES]
            zero = jnp.zeros_like(x2)
            rhs = jnp.concatenate([jnp.where(lo_half, x2, zero), jnp.where(lo_half, zero, x2)], axis=0)
            pieces.append(_dot(jnp.concatenate(mats, axis=1), rhs))
        y_diag = jnp.concatenate(pieces, axis=1)
        new_states = _dot_tn(b_g, xdec[:, seg])
        state_ref[g] = state_ref[g] * chunk_decay[:, seg] + new_states
        y = y_diag + y_off + xs[:, seg] * dskip_ref[:, seg]
        y = y * gate[:, seg]
        y = y * lax.rsqrt(jnp.mean(y * y, axis=-1, keepdims=True) + EPS)
        o_ref[:, D_MODEL + g * GDIM:D_MODEL + (g + 1) * GDIM] = (y * snorm_ref[:, seg]).astype(BF16)


def _even_mix(proj, dt_raw, ln_g, ln_b, ws, bs_t, conv_w, conv_b, dt_bias, a_log, d_skip_e, ssd_norm, e3):
    m = proj.shape[0]
    full = lambda shape: pl.BlockSpec(shape, lambda c: (0,) * len(shape))
    return pl.pallas_call(
        _even_mix_body,
        grid=(m // CHUNK,),
        in_specs=[
            pl.BlockSpec((CHUNK, D_MODEL), lambda c: (c, 0)),
            pl.BlockSpec((CHUNK, D_MODEL), lambda c: (c, 1)),
            pl.BlockSpec((CHUNK, D_MODEL), lambda c: (c, 2)),
            pl.BlockSpec((CHUNK, CONV_DIM), lambda c: (c, 2)),
            pl.BlockSpec((CHUNK, LANES), lambda c: (c, 0)),
            full((1, D_MODEL)), full((1, D_MODEL)),
            full((GROUPS, CHUNK, CHUNK)), full((CHUNK, GROUPS)),
            full((SSD_CONV, CONV_DIM)), full((1, CONV_DIM)),
            full((1, LANES)), full((1, LANES)),
            full((1, D_MODEL)), full((1, D_MODEL)),
            full((3 * LANES, D_MODEL)),
        ],
        out_specs=pl.BlockSpec((CHUNK, 2 * D_MODEL), lambda c: (c, 0)),
        out_shape=jax.ShapeDtypeStruct((m, 2 * D_MODEL), BF16),
        scratch_shapes=[
            pltpu.VMEM((GROUPS, SSD_STATE, GDIM), F32),
            pltpu.VMEM((CONV_TAIL + CHUNK, CONV_DIM), F32),
        ],
        compiler_params=_params(("arbitrary",), 48),
        name="even_mix",
    )(proj, proj, proj, proj, dt_raw, ln_g, ln_b, ws, bs_t, conv_w, conv_b, dt_bias, a_log, d_skip_e, ssd_norm, e3)


def _swa_body(sink_ref, q_ref, kv_ref, kvp_ref, o_ref):
    n = pl.program_id(0)
    w = CHUNK
    row = lax.broadcasted_iota(jnp.int32, (w, w), 0)
    col = lax.broadcasted_iota(jnp.int32, (w, w), 1)
    mask_cur = col <= row
    mask_prev = (col > row) & (n > 0)
    lane = lax.broadcasted_iota(jnp.int32, (w, LANES), 1)
    lo_half = lane < HEAD_DIM
    kv_width = GROUPS * HEAD_DIM

    def head_pair_tile(ref, base, k):
        t = ref[:, base + (k // 2) * LANES:base + (k // 2 + 1) * LANES].astype(F32)
        r = pltpu.roll(t, HEAD_DIM, 1)
        lo = lax.broadcasted_iota(jnp.int32, t.shape, 1) < HEAD_DIM
        dup = jnp.where(lo, t, r) if k % 2 == 0 else jnp.where(lo, r, t)
        return dup.astype(BF16)

    for k in range(GROUPS):
        kk_cur = head_pair_tile(kv_ref, 0, k)
        kk_prev = head_pair_tile(kvp_ref, 0, k)
        vv_cur = head_pair_tile(kv_ref, kv_width, k)
        vv_prev = head_pair_tile(kvp_ref, kv_width, k)
        lhs = []
        for p in range(ATT_REP // 2):
            q2 = q_ref[:, k * GDIM + p * LANES:k * GDIM + (p + 1) * LANES]
            zero = jnp.zeros_like(q2)
            lhs.append(jnp.where(lo_half, q2, zero))
            lhs.append(jnp.where(lo_half, zero, q2))
        lhs = jnp.concatenate(lhs, axis=0)
        s_prev = _dot_nt(lhs, kk_prev)
        s_cur = _dot_nt(lhs, kk_cur)
        p_prev, p_cur, inv = [], [], []
        for r in range(ATT_REP):
            sink = sink_ref[k * ATT_REP + r]
            sp = jnp.where(mask_prev, s_prev[r * w:(r + 1) * w], -jnp.inf)
            sc = jnp.where(mask_cur, s_cur[r * w:(r + 1) * w], -jnp.inf)
            mx = jnp.maximum(jnp.maximum(jnp.max(sp, axis=-1, keepdims=True),
                                         jnp.max(sc, axis=-1, keepdims=True)), sink)
            ep = jnp.exp(sp - mx)
            ec = jnp.exp(sc - mx)
            den = jnp.sum(ep, axis=-1, keepdims=True) + jnp.sum(ec, axis=-1, keepdims=True) + jnp.exp(sink - mx)
            p_prev.append(ep.astype(BF16))
            p_cur.append(ec.astype(BF16))
            inv.append(1.0 / den)
        o = _dot(jnp.concatenate(p_prev, axis=0), vv_prev) + _dot(jnp.concatenate(p_cur, axis=0), vv_cur)
        for p in range(ATT_REP // 2):
            o_lo = o[(2 * p) * w:(2 * p + 1) * w] * inv[2 * p]
            o_hi = o[(2 * p + 1) * w:(2 * p + 2) * w] * inv[2 * p + 1]
            o_ref[:, k * GDIM + p * LANES:k * GDIM + (p + 1) * LANES] = jnp.where(lo_half, o_lo, o_hi).astype(BF16)


def _swa(qkv, sinks):
    m = qkv.shape[0]
    kv_block = ATT_HEADS * HEAD_DIM // (2 * GROUPS * HEAD_DIM)
    return pl.pallas_call(
        _swa_body,
        grid=(m // CHUNK,),
        in_specs=[
            pl.BlockSpec(memory_space=pltpu.SMEM),
            pl.BlockSpec((CHUNK, D_MODEL), lambda n: (n, 0)),
            pl.BlockSpec((CHUNK, 2 * GROUPS * HEAD_DIM), lambda n: (n, kv_block)),
            pl.BlockSpec((CHUNK, 2 * GROUPS * HEAD_DIM), lambda n: (jnp.maximum(n - 1, 0), kv_block)),
        ],
        out_specs=pl.BlockSpec((CHUNK, D_MODEL), lambda n: (n, 0)),
        out_shape=jax.ShapeDtypeStruct((m, D_MODEL), BF16),
        compiler_params=_params(("parallel",), 32),
        name="swa",
    )(sinks, qkv, qkv, qkv)


def _xattn_body(x_ref, g_ref, wq_ref, kv_ref, wo_ref, o_ref):
    x = x_ref[...]
    h = _rms(x, g_ref[...]).astype(BF16)
    q = _dot(h, wq_ref[...]).astype(BF16)
    outs = []
    for hd in range(X_HEADS):
        seg = slice(hd * X_HEAD_DIM, (hd + 1) * X_HEAD_DIM)
        k = kv_ref[:, seg]
        v = kv_ref[:, X_WIDTH + hd * X_HEAD_DIM:X_WIDTH + (hd + 1) * X_HEAD_DIM]
        s = _dot_nt(q[:, seg], k) * X_SCALE
        e = jnp.exp(s - jnp.max(s, axis=-1, keepdims=True))
        o = _dot(e.astype(BF16), v) * (1.0 / jnp.sum(e, axis=-1, keepdims=True))
        outs.append(o.astype(BF16))
    o_ref[...] = x + _dot(jnp.concatenate(outs, axis=1), wo_ref[...])


def _xattn(x, gain, w_q, kv, w_o):
    m = x.shape[0]
    bm = 512
    return pl.pallas_call(
        _xattn_body,
        grid=(m // bm,),
        in_specs=[
            pl.BlockSpec((bm, D_MODEL), lambda i: (i, 0)),
            pl.BlockSpec((1, D_MODEL), lambda i: (0, 0)),
            pl.BlockSpec((D_MODEL, X_WIDTH), lambda i: (0, 0)),
            pl.BlockSpec((N_MEM, 2 * X_WIDTH), lambda i: (0, 0)),
            pl.BlockSpec((X_WIDTH, D_MODEL), lambda i: (0, 0)),
        ],
        out_specs=pl.BlockSpec((bm, D_MODEL), lambda i: (i, 0)),
        out_shape=jax.ShapeDtypeStruct((m, D_MODEL), F32),
        compiler_params=_params(("parallel",), 48),
        name="xattn",
    )(x, gain, w_q, kv, w_o)


def _final_norm_body(x_ref, g_ref, o_ref):
    o_ref[...] = _rms(x_ref[...], g_ref[...])


def _final_norm(x, gain):
    m = x.shape[0]
    bm = 512
    return pl.pallas_call(
        _final_norm_body,
        grid=(m // bm,),
        in_specs=[pl.BlockSpec((bm, D_MODEL), lambda i: (i, 0)), pl.BlockSpec((1, D_MODEL), lambda i: (0, 0))],
        out_specs=pl.BlockSpec((bm, D_MODEL), lambda i: (i, 0)),
        out_shape=jax.ShapeDtypeStruct((m, D_MODEL), F32),
        compiler_params=_params(("parallel",), 32),
        name="final_norm",
    )(x, gain)


def _ffn(x, gain, w_gu, w_down, name):
    act = _ffn_up(x, gain.reshape(1, -1), w_gu.astype(BF16))
    return _mm_res(act, w_down.astype(BF16), x, 0.5, name)


def _pad_lanes(v):
    return jnp.pad(v.reshape(1, -1), ((0, 0), (0, LANES - v.shape[-1])))


def kernel(x, mem, positions, norm_ffn1, w_ffn1_gu, w_ffn1_down, norm_mix, w_in_even, gm_ln_g, gm_ln_b, gm_ws, gm_bs, conv_w, conv_b, dt_bias, a_log, d_skip, ssd_norm, w_out_even, w_qkv, b_qkv, sinks, w_o_odd, norm_xq, norm_mem, w_xq, w_xkv, w_xo, norm_ffn2, w_ffn2_gu, w_ffn2_down, final_norm):
    bsz, seq, d = x.shape
    assert (bsz, seq, d) == (1, SEQ, D_MODEL)
    xr = x.reshape(seq, d)
    memr = mem.reshape(N_MEM, d)
    pos = positions.reshape(seq, 1)
    inv_freq = ROPE_THETA ** (-jnp.arange(0, ROT_DIM, 2, dtype=F32) / ROT_DIM)
    invf = jnp.tile(inv_freq, LANES // (ROT_DIM // 2)).reshape(1, LANES)
    head_of_lane = jnp.arange(D_MODEL, dtype=jnp.int32) // HEAD_DIM
    e1 = (jnp.arange(LANES, dtype=jnp.int32)[:, None] == head_of_lane[None, :]).astype(BF16)
    e3 = jnp.concatenate([e1, e1, e1], axis=0)

    for i in range(DEPTH):
        j = i // 2
        xr = _ffn(xr, norm_ffn1[i], w_ffn1_gu[i], w_ffn1_down[i], "ffn1_down")
        if i % 2 == 0:
            w_in = w_in_even[j]
            proj, dt_raw = _even_in(
                xr, norm_mix[i].reshape(1, -1), w_in[:, :EVEN_MAIN].astype(BF16),
                jnp.pad(w_in[:, EVEN_MAIN:], ((0, 0), (0, LANES - SSD_HEADS))).astype(BF16))
            mix = _even_mix(
                proj, dt_raw, gm_ln_g[j].reshape(1, -1), gm_ln_b[j].reshape(1, -1), gm_ws[j], gm_bs[j].T,
                conv_w[j], conv_b[j].reshape(1, -1), _pad_lanes(dt_bias[j]), _pad_lanes(a_log[j]),
                jnp.repeat(d_skip[j], HEAD_DIM).reshape(1, -1), ssd_norm[j].reshape(1, -1), e3)
            xr = _mm_res(mix, w_out_even[j].astype(BF16), xr, 1.0, "even_out")
        else:
            qkv = _qkv(xr, norm_mix[i].reshape(1, -1), w_qkv[j].astype(BF16), b_qkv[j].reshape(1, -1), pos, invf)
            att = _swa(qkv, sinks[j])
            xr = _mm_res(att, w_o_odd[j].astype(BF16), xr, 1.0, "odd_out")
        kv = _norm_mm(memr, norm_mem[i].reshape(1, -1), w_xkv[i].astype(BF16), N_MEM, BF16, "mem_kv")
        xr = _xattn(xr, norm_xq[i].reshape(1, -1), w_xq[i].astype(BF16), kv, w_xo[i].astype(BF16))
        xr = _ffn(xr, norm_ffn2[i], w_ffn2_gu[i], w_ffn2_down[i], "ffn2_down")
    return _final_norm(xr, final_norm.reshape(1, -1)).reshape(bsz, seq, d)
```

```python
import functools

import jax
import jax.numpy as jnp
from jax import lax
from jax.experimental import pallas as pl
from jax.experimental.pallas import tpu as pltpu

F32 = jnp.float32
BF16 = jnp.bfloat16

D_MODEL = 2048
SEQ = 8192
DEPTH = 2
EPS = 1e-5
N_MEM = 256
D_FF = 5632
CHUNK = 128
GROUPS = 4
GDIM = D_MODEL // GROUPS
HEAD_DIM = 64
SSD_HEADS = 32
SSD_STATE = 128
SSD_CONV = 4
CONV_DIM = D_MODEL + 2 * GROUPS * SSD_STATE
EVEN_MAIN = 2 * D_MODEL + D_MODEL + CONV_DIM
ATT_HEADS = 32
ATT_REP = ATT_HEADS // GROUPS
ATT_SCALE = HEAD_DIM ** -0.5
ROT_DIM = HEAD_DIM // 4
ROT_HALF = ROT_DIM // 2
ROPE_THETA = 500000.0
KV_WIDTH = GROUPS * HEAD_DIM
ODD_IN = (ATT_HEADS + 2 * GROUPS) * HEAD_DIM
X_HEADS = 4
X_HEAD_DIM = 128
X_WIDTH = X_HEADS * X_HEAD_DIM
X_SCALE = X_HEAD_DIM ** -0.5

LANES = 128
SUBLANES = 8
CONV_TAIL = 16
BM = 1024
BN = 512
MIB = 1024 * 1024


def _params(semantics, vmem_mib):
    return pltpu.CompilerParams(dimension_semantics=semantics, vmem_limit_bytes=vmem_mib * MIB)


def _rms(x, g):
    ms = jnp.mean(x * x, axis=-1, keepdims=True)
    return x * lax.rsqrt(ms + EPS) * g


def _silu(x):
    return x * jax.nn.sigmoid(x)


def _gelu(x):
    return 0.5 * x * (1.0 + lax.erf(x * (2.0 ** -0.5)))


def _dot(a, b):
    return jnp.dot(a, b, preferred_element_type=F32)


def _dot_nt(a, b):
    return lax.dot_general(a, b, (((1,), (1,)), ((), ())), preferred_element_type=F32)


def _dot_tn(a, b):
    return lax.dot_general(a, b, (((0,), (0,)), ((), ())), preferred_element_type=F32)


def _ffn_up_body(x_ref, g_ref, wg_ref, wu_ref, o_ref, h_ref, w_ref):
    @pl.when(pl.program_id(1) == 0)
    def _():
        h_ref[...] = _rms(x_ref[...], g_ref[...]).astype(BF16)

    w_ref[:, :BN] = wg_ref[...].astype(BF16)
    w_ref[:, BN:] = wu_ref[...].astype(BF16)
    gu = _dot(h_ref[...], w_ref[...])
    o_ref[...] = (_silu(gu[:, :BN]) * gu[:, BN:]).astype(BF16)


def _ffn_up(x, gain, w_gu, layer):
    m = x.shape[0]
    nj = D_FF // BN
    return pl.pallas_call(
        _ffn_up_body,
        grid=(m // BM, nj),
        in_specs=[
            pl.BlockSpec((BM, D_MODEL), lambda i, j: (i, 0)),
            pl.BlockSpec((1, D_MODEL), lambda i, j: (0, 0)),
            pl.BlockSpec((None, D_MODEL, BN), lambda i, j: (layer, 0, j)),
            pl.BlockSpec((None, D_MODEL, BN), lambda i, j: (layer, 0, j + nj)),
        ],
        out_specs=pl.BlockSpec((BM, BN), lambda i, j: (i, j)),
        out_shape=jax.ShapeDtypeStruct((m, D_FF), BF16),
        scratch_shapes=[pltpu.VMEM((BM, D_MODEL), BF16), pltpu.VMEM((D_MODEL, 2 * BN), BF16)],
        compiler_params=_params(("parallel", "arbitrary"), 56),
        name="ffn_up",
    )(x, gain, w_gu, w_gu)


def _mm_res_body(a_ref, w_ref, r_ref, o_ref, *, scale):
    o_ref[...] = r_ref[...] + scale * _dot(a_ref[...], w_ref[...].astype(BF16))


def _mm_res(a, w, layer, res, scale, name):
    m, k = a.shape
    n = w.shape[2]
    bn = BN if k <= 2 * D_MODEL else BN // 2
    return pl.pallas_call(
        functools.partial(_mm_res_body, scale=scale),
        grid=(m // BM, n // bn),
        in_specs=[
            pl.BlockSpec((BM, k), lambda i, j: (i, 0)),
            pl.BlockSpec((None, k, bn), lambda i, j: (layer, 0, j)),
            pl.BlockSpec((BM, bn), lambda i, j: (i, j)),
        ],
        out_specs=pl.BlockSpec((BM, bn), lambda i, j: (i, j)),
        out_shape=jax.ShapeDtypeStruct((m, n), F32),
        compiler_params=_params(("parallel", "arbitrary"), 56),
        name=name,
    )(a, w, res)


def _norm_mm_body(x_ref, g_ref, w_ref, o_ref, h_ref):
    @pl.when(pl.program_id(1) == 0)
    def _():
        h_ref[...] = _rms(x_ref[...], g_ref[...]).astype(BF16)

    o_ref[...] = _dot(h_ref[...], w_ref[...].astype(BF16)).astype(o_ref.dtype)


def _norm_mm(x, gain, w, layer, bm, out_dtype, name):
    m, k = x.shape
    n = w.shape[2]
    return pl.pallas_call(
        _norm_mm_body,
        grid=(m // bm, n // BN),
        in_specs=[
            pl.BlockSpec((bm, k), lambda i, j: (i, 0)),
            pl.BlockSpec((1, k), lambda i, j: (0, 0)),
            pl.BlockSpec((None, k, BN), lambda i, j: (layer, 0, j)),
        ],
        out_specs=pl.BlockSpec((bm, BN), lambda i, j: (i, j)),
        out_shape=jax.ShapeDtypeStruct((m, n), out_dtype),
        scratch_shapes=[pltpu.VMEM((bm, k), BF16)],
        compiler_params=_params(("parallel", "arbitrary"), 40),
        name=name,
    )(x, gain, w)


def _even_in_body(x_ref, g_ref, w_ref, wdt_ref, o_ref, dt_ref, h_ref, *, n_gelu):
    j = pl.program_id(1)

    @pl.when(j == 0)
    def _():
        h = _rms(x_ref[...], g_ref[...]).astype(BF16)
        h_ref[...] = h
        lane = lax.broadcasted_iota(jnp.int32, (1, LANES), 1)
        dt_ref[...] = _dot(h, jnp.where(lane < SSD_HEADS, wdt_ref[...], 0.0).astype(BF16))

    acc = _dot(h_ref[...], w_ref[...].astype(BF16))

    @pl.when(j < n_gelu)
    def _():
        o_ref[...] = _gelu(acc).astype(BF16)

    @pl.when(j >= n_gelu)
    def _():
        o_ref[...] = acc.astype(BF16)


def _even_in(x, gain, w_in, layer):
    m = x.shape[0]
    return pl.pallas_call(
        functools.partial(_even_in_body, n_gelu=2 * D_MODEL // BN),
        grid=(m // BM, EVEN_MAIN // BN),
        in_specs=[
            pl.BlockSpec((BM, D_MODEL), lambda i, j: (i, 0)),
            pl.BlockSpec((1, D_MODEL), lambda i, j: (0, 0)),
            pl.BlockSpec((None, D_MODEL, BN), lambda i, j: (layer, 0, j)),
            pl.BlockSpec((None, D_MODEL, LANES), lambda i, j: (layer, 0, EVEN_MAIN // LANES)),
        ],
        out_specs=[
            pl.BlockSpec((BM, BN), lambda i, j: (i, j)),
            pl.BlockSpec((BM, LANES), lambda i, j: (i, 0)),
        ],
        out_shape=[
            jax.ShapeDtypeStruct((m, EVEN_MAIN), BF16),
            jax.ShapeDtypeStruct((m, LANES), F32),
        ],
        scratch_shapes=[pltpu.VMEM((BM, D_MODEL), BF16)],
        compiler_params=_params(("parallel", "arbitrary"), 48),
        name="even_in",
    )(x, gain, w_in, w_in)


def _rope_table_body(pos_ref, invf_ref, cos_ref, sin_ref):
    ang = pos_ref[...].astype(F32) * invf_ref[...]
    cos_ref[...] = jnp.cos(ang)
    sin_ref[...] = jnp.sin(ang)


def _rope_table(pos_row, invf_col):
    shape = jax.ShapeDtypeStruct((ROT_HALF, pos_row.shape[1]), F32)
    return pl.pallas_call(_rope_table_body, out_shape=[shape, shape], name="rope_table")(pos_row, invf_col)


def _qkv_body(x_ref, g_ref, w_ref, b_ref, cost_ref, sint_ref, o_ref, h_ref, cos_ref, sn_ref, sp_ref):
    j = pl.program_id(1)

    @pl.when(j == 0)
    def _():
        h_ref[...] = _rms(x_ref[...], g_ref[...]).astype(BF16)
        reps = LANES // ROT_HALF
        cos = jnp.concatenate([cost_ref[...]] * reps, axis=0).T
        sin = jnp.concatenate([sint_ref[...]] * reps, axis=0).T
        lane = lax.broadcasted_iota(jnp.int32, (1, LANES), 1) % HEAD_DIM
        first = lane < ROT_HALF
        second = (lane >= ROT_HALF) & (lane < ROT_DIM)
        cos_ref[...] = jnp.where(first | second, cos, 1.0)
        sn_ref[...] = jnp.where(first, -sin, 0.0)
        sp_ref[...] = jnp.where(second, sin, 0.0)

    acc = _dot(h_ref[...], w_ref[...].astype(BF16)) + b_ref[...]

    def rope(a):
        return (a * cos_ref[...] + pltpu.roll(a, LANES - ROT_HALF, 1) * sn_ref[...]
                + pltpu.roll(a, ROT_HALF, 1) * sp_ref[...])

    is_q = j < ATT_HEADS * HEAD_DIM // BN
    scale = jnp.where(is_q, ATT_SCALE, 1.0)
    for t in range(BN // LANES):
        a = acc[:, t * LANES:(t + 1) * LANES]
        val = rope(a)
        if t >= KV_WIDTH // LANES:
            val = jnp.where(is_q, val, a)
        o_ref[:, t * LANES:(t + 1) * LANES] = (val * scale).astype(BF16)


def _qkv(x, gain, w, layer, b, cos_t, sin_t):
    m = x.shape[0]
    return pl.pallas_call(
        _qkv_body,
        grid=(m // BM, ODD_IN // BN),
        in_specs=[
            pl.BlockSpec((BM, D_MODEL), lambda i, j: (i, 0)),
            pl.BlockSpec((1, D_MODEL), lambda i, j: (0, 0)),
            pl.BlockSpec((None, D_MODEL, BN), lambda i, j: (layer, 0, j)),
            pl.BlockSpec((1, BN), lambda i, j: (0, j)),
            pl.BlockSpec((ROT_HALF, BM), lambda i, j: (0, i)),
            pl.BlockSpec((ROT_HALF, BM), lambda i, j: (0, i)),
        ],
        out_specs=pl.BlockSpec((BM, BN), lambda i, j: (i, j)),
        out_shape=jax.ShapeDtypeStruct((m, ODD_IN), BF16),
        scratch_shapes=[
            pltpu.VMEM((BM, D_MODEL), BF16),
            pltpu.VMEM((BM, LANES), F32),
            pltpu.VMEM((BM, LANES), F32),
            pltpu.VMEM((BM, LANES), F32),
        ],
        compiler_params=_params(("parallel", "arbitrary"), 48),
        name="qkv_rope",
    )(x, gain, w, b, cos_t, sin_t)


def _split3(x):
    hi = x.astype(BF16)
    r1 = x - hi.astype(F32)
    mid = r1.astype(BF16)
    lo = (r1 - mid.astype(F32)).astype(BF16)
    return hi, mid, lo


def _even_mix_body(u_ref, v_ref, z_ref, xbc_ref, dt_ref, lng_ref, lnb_ref, ws_ref, bs_ref, cw_ref, cb_ref,
                   dtb_ref, alog_ref, dskip_ref, snorm_ref, e3_ref, o_ref, state_ref, xpad_ref):
    c = pl.program_id(0)
    q = CHUNK

    @pl.when(c == 0)
    def _():
        state_ref[...] = jnp.zeros_like(state_ref)
        xpad_ref[pl.ds(0, CONV_TAIL), :] = jnp.zeros((CONV_TAIL, CONV_DIM), F32)

    row = lax.broadcasted_iota(jnp.int32, (q, q), 0)
    col = lax.broadcasted_iota(jnp.int32, (q, q), 1)
    causal = col <= row

    for g in range(GROUPS):
        seg = slice(g * GDIM, (g + 1) * GDIM)
        vg = v_ref[:, seg].astype(F32)
        mu = jnp.mean(vg, axis=-1, keepdims=True)
        d = vg - mu
        var = jnp.mean(d * d, axis=-1, keepdims=True)
        vn = d * lax.rsqrt(var + EPS) * lng_ref[:, seg] + lnb_ref[:, seg]
        w = jnp.where(causal, ws_ref[g], 0.0).astype(BF16)
        s = _dot(w, vn.astype(BF16)) + bs_ref[:, g:g + 1]
        o_ref[:, seg] = (u_ref[:, seg].astype(F32) * s).astype(BF16)

    xpad_ref[pl.ds(CONV_TAIL, q), :] = xbc_ref[...].astype(F32)
    conv = cb_ref[...] + cw_ref[SSD_CONV - 1:SSD_CONV, :] * xpad_ref[pl.ds(CONV_TAIL, q), :]
    for k in range(SSD_CONV - 1):
        conv = conv + cw_ref[k:k + 1, :] * xpad_ref[pl.ds(CONV_TAIL - (SSD_CONV - 1) + k, q), :]
    xpad_ref[pl.ds(0, CONV_TAIL), :] = xpad_ref[pl.ds(q, CONV_TAIL), :]
    xbc = _silu(conv)
    xs = xbc[:, :D_MODEL]

    dt = jax.nn.softplus(dt_ref[...] + dtb_ref[...])
    a = dt * (-jnp.exp(alog_ref[...]))
    tri = jnp.where(causal, 1.0, 0.0).astype(BF16)
    a_hi, a_mid, a_lo = _split3(a)
    acs = _dot(tri, a_hi) + _dot(tri, a_mid) + _dot(tri, a_lo)
    acs_t = acs.T
    both = jnp.concatenate([dt, acs], axis=0)
    b_hi, b_mid, b_lo = _split3(both)
    both_e = _dot(jnp.concatenate([b_hi, b_mid, b_lo], axis=1), e3_ref[...])
    dt_e = both_e[:q]
    acs_e = both_e[q:]
    last_e = acs_e[q - 1:q, :]
    xdt = xs * dt_e
    xdec = (xdt * jnp.exp(last_e - acs_e)).astype(BF16)
    xdt_b = xdt.astype(BF16)
    grow_e = jnp.exp(acs_e)
    chunk_decay = jnp.exp(last_e)

    lane = lax.broadcasted_iota(jnp.int32, (q, LANES), 1)
    lo_half = lane < HEAD_DIM
    zf = z_ref[...].astype(F32)
    gate = _silu(zf)

    for g in range(GROUPS):
        seg = slice(g * GDIM, (g + 1) * GDIM)
        b_g = xbc[:, D_MODEL + g * SSD_STATE:D_MODEL + (g + 1) * SSD_STATE].astype(BF16)
        c_g = xbc[:, D_MODEL + GROUPS * SSD_STATE + g * SSD_STATE:
                  D_MODEL + GROUPS * SSD_STATE + (g + 1) * SSD_STATE].astype(BF16)
        cb_causal = jnp.where(causal, _dot_nt(c_g, b_g), 0.0)
        y_off = _dot(c_g, state_ref[g].astype(BF16)) * grow_e[:, seg]
        pieces = []
        for p in range(GDIM // LANES):
            mats = []
            for hh in range(2):
                h = g * (GDIM // HEAD_DIM) + 2 * p + hh
                seg_ij = jnp.minimum(acs[:, h:h + 1] - acs_t[h:h + 1, :], 0.0)
                mats.append((cb_causal * jnp.exp(seg_ij)).astype(BF16))
            x2 = xdt_b[:, g * GDIM + p * LANES:g * GDIM + (p + 1) * LANES]
            zero = jnp.zeros_like(x2)
            rhs = jnp.concatenate([jnp.where(lo_half, x2, zero), jnp.where(lo_half, zero, x2)], axis=0)
            pieces.append(_dot(jnp.concatenate(mats, axis=1), rhs))
        y_diag = jnp.concatenate(pieces, axis=1)
        new_states = _dot_tn(b_g, xdec[:, seg])
        state_ref[g] = state_ref[g] * chunk_decay[:, seg] + new_states
        y = y_diag + y_off + xs[:, seg] * dskip_ref[:, seg]
        y = y * gate[:, seg]
        y = y * lax.rsqrt(jnp.mean(y * y, axis=-1, keepdims=True) + EPS)
        o_ref[:, D_MODEL + g * GDIM:D_MODEL + (g + 1) * GDIM] = (y * snorm_ref[:, seg]).astype(BF16)


def _even_mix(proj, dt_raw, ln_g, ln_b, ws, bs_t, conv_w, conv_b, dt_bias, a_log, d_skip_e, ssd_norm, e3):
    m = proj.shape[0]
    full = lambda shape: pl.BlockSpec(shape, lambda c: (0,) * len(shape))
    return pl.pallas_call(
        _even_mix_body,
        grid=(m // CHUNK,),
        in_specs=[
            pl.BlockSpec((CHUNK, D_MODEL), lambda c: (c, 0)),
            pl.BlockSpec((CHUNK, D_MODEL), lambda c: (c, 1)),
            pl.BlockSpec((CHUNK, D_MODEL), lambda c: (c, 2)),
            pl.BlockSpec((CHUNK, CONV_DIM), lambda c: (c, 2)),
            pl.BlockSpec((CHUNK, LANES), lambda c: (c, 0)),
            full((1, D_MODEL)), full((1, D_MODEL)),
            full((GROUPS, CHUNK, CHUNK)), full((CHUNK, GROUPS)),
            full((SSD_CONV, CONV_DIM)), full((1, CONV_DIM)),
            full((1, LANES)), full((1, LANES)),
            full((1, D_MODEL)), full((1, D_MODEL)),
            full((3 * LANES, D_MODEL)),
        ],
        out_specs=pl.BlockSpec((CHUNK, 2 * D_MODEL), lambda c: (c, 0)),
        out_shape=jax.ShapeDtypeStruct((m, 2 * D_MODEL), BF16),
        scratch_shapes=[
            pltpu.VMEM((GROUPS, SSD_STATE, GDIM), F32),
            pltpu.VMEM((CONV_TAIL + CHUNK, CONV_DIM), F32),
        ],
        compiler_params=_params(("arbitrary",), 48),
        name="even_mix",
    )(proj, proj, proj, proj, dt_raw, ln_g, ln_b, ws, bs_t, conv_w, conv_b, dt_bias, a_log, d_skip_e, ssd_norm, e3)


def _swa_body(sink_ref, q_ref, kv_ref, kvp_ref, o_ref):
    n = pl.program_id(0)
    w = CHUNK
    row = lax.broadcasted_iota(jnp.int32, (w, w), 0)
    col = lax.broadcasted_iota(jnp.int32, (w, w), 1)
    own = col <= row
    lo_half = lax.broadcasted_iota(jnp.int32, (w, LANES), 1) < HEAD_DIM
    prev_bias = jnp.where(n > 0, 0.0, -jnp.inf)

    def head_tiles(ref, base, k):
        t = ref[:, base + (k // 2) * LANES:base + (k // 2 + 1) * LANES].astype(F32)
        r = pltpu.roll(t, HEAD_DIM, 1)
        return (t, r) if k % 2 == 0 else (r, t)

    for k in range(GROUPS):
        k_lo, k_hi = head_tiles(kv_ref, 0, k)
        kp_lo, kp_hi = head_tiles(kvp_ref, 0, k)
        v_lo, v_hi = head_tiles(kv_ref, KV_WIDTH, k)
        vp_lo, vp_hi = head_tiles(kvp_ref, KV_WIDTH, k)
        kk = jnp.where(lo_half, k_lo, k_hi).astype(BF16)
        kkp = jnp.where(lo_half, kp_lo, kp_hi).astype(BF16)
        out = []
        for parity in range(2):
            if parity == 0:
                vv = jnp.concatenate([jnp.where(lo_half, v_lo, 1.0), jnp.where(lo_half, vp_lo, 1.0)], axis=0)
            else:
                vv = jnp.concatenate([jnp.where(lo_half, 1.0, v_hi), jnp.where(lo_half, 1.0, vp_hi)], axis=0)
            lhs = []
            for p in range(ATT_REP // 2):
                q2 = q_ref[:, k * GDIM + p * LANES:k * GDIM + (p + 1) * LANES]
                zero = jnp.zeros_like(q2)
                lhs.append(jnp.where(lo_half, q2, zero) if parity == 0 else jnp.where(lo_half, zero, q2))
            lhs = jnp.concatenate(lhs, axis=0)
            s_own = _dot_nt(lhs, kk)
            s_prev = _dot_nt(lhs, kkp)
            probs, esink = [], []
            for p in range(ATT_REP // 2):
                sink = sink_ref[k * ATT_REP + 2 * p + parity]
                s = jnp.where(own, s_own[p * w:(p + 1) * w], s_prev[p * w:(p + 1) * w] + prev_bias)
                mx = jnp.maximum(jnp.max(s, axis=-1, keepdims=True), sink)
                e = jnp.exp(s - mx)
                probs.append(jnp.concatenate([jnp.where(own, e, 0.0).astype(BF16),
                                              jnp.where(own, 0.0, e).astype(BF16)], axis=1))
                esink.append(jnp.exp(sink - mx))
            o = _dot(jnp.concatenate(probs, axis=0), vv.astype(BF16))
            out.append((o, esink))
        for p in range(ATT_REP // 2):
            o_even = out[0][0][p * w:(p + 1) * w]
            o_odd = out[1][0][p * w:(p + 1) * w]
            num = jnp.where(lo_half, o_even, o_odd)
            den = pltpu.roll(jnp.where(lo_half, o_odd, o_even), HEAD_DIM, 1)
            den = den + jnp.where(lo_half, out[0][1][p], out[1][1][p])
            o_ref[:, k * GDIM + p * LANES:k * GDIM + (p + 1) * LANES] = (num / den).astype(BF16)


def _swa(qkv, sinks):
    m = qkv.shape[0]
    kv_block = ATT_HEADS * HEAD_DIM // (2 * KV_WIDTH)
    return pl.pallas_call(
        _swa_body,
        grid=(m // CHUNK,),
        in_specs=[
            pl.BlockSpec(memory_space=pltpu.SMEM),
            pl.BlockSpec((CHUNK, D_MODEL), lambda n: (n, 0)),
            pl.BlockSpec((CHUNK, 2 * KV_WIDTH), lambda n: (n, kv_block)),
            pl.BlockSpec((CHUNK, 2 * KV_WIDTH), lambda n: (jnp.maximum(n - 1, 0), kv_block)),
        ],
        out_specs=pl.BlockSpec((CHUNK, D_MODEL), lambda n: (n, 0)),
        out_shape=jax.ShapeDtypeStruct((m, D_MODEL), BF16),
        compiler_params=_params(("parallel",), 32),
        name="swa",
    )(sinks, qkv, qkv, qkv)


def _xattn_body(x_ref, g_ref, wq_ref, kv_ref, wo_ref, o_ref, wq_b, wo_b):
    @pl.when(pl.program_id(0) == 0)
    def _():
        wq_b[...] = wq_ref[...].astype(BF16)
        wo_b[...] = wo_ref[...].astype(BF16)

    x = x_ref[...]
    h = _rms(x, g_ref[...]).astype(BF16)
    q = _dot(h, wq_b[...]).astype(BF16)
    outs = []
    for hd in range(X_HEADS):
        seg = slice(hd * X_HEAD_DIM, (hd + 1) * X_HEAD_DIM)
        k = kv_ref[:, seg]
        v = kv_ref[:, X_WIDTH + hd * X_HEAD_DIM:X_WIDTH + (hd + 1) * X_HEAD_DIM]
        s = _dot_nt(q[:, seg], k) * X_SCALE
        e = jnp.exp(s - jnp.max(s, axis=-1, keepdims=True))
        o = _dot(e.astype(BF16), v) * (1.0 / jnp.sum(e, axis=-1, keepdims=True))
        outs.append(o.astype(BF16))
    o_ref[...] = x + _dot(jnp.concatenate(outs, axis=1), wo_b[...])


def _xattn(x, gain, w_q, kv, w_o, layer):
    m = x.shape[0]
    bm = 512
    return pl.pallas_call(
        _xattn_body,
        grid=(m // bm,),
        in_specs=[
            pl.BlockSpec((bm, D_MODEL), lambda i: (i, 0)),
            pl.BlockSpec((1, D_MODEL), lambda i: (0, 0)),
            pl.BlockSpec((None, D_MODEL, X_WIDTH), lambda i: (layer, 0, 0)),
            pl.BlockSpec((N_MEM, 2 * X_WIDTH), lambda i: (0, 0)),
            pl.BlockSpec((None, X_WIDTH, D_MODEL), lambda i: (layer, 0, 0)),
        ],
        out_specs=pl.BlockSpec((bm, D_MODEL), lambda i: (i, 0)),
        out_shape=jax.ShapeDtypeStruct((m, D_MODEL), F32),
        scratch_shapes=[pltpu.VMEM((D_MODEL, X_WIDTH), BF16), pltpu.VMEM((X_WIDTH, D_MODEL), BF16)],
        compiler_params=_params(("arbitrary",), 48),
        name="xattn",
    )(x, gain, w_q, kv, w_o)


def _final_norm_body(x_ref, g_ref, o_ref):
    o_ref[...] = _rms(x_ref[...], g_ref[...])


def _final_norm(x, gain):
    m = x.shape[0]
    bm = 512
    return pl.pallas_call(
        _final_norm_body,
        grid=(m // bm,),
        in_specs=[pl.BlockSpec((bm, D_MODEL), lambda i: (i, 0)), pl.BlockSpec((1, D_MODEL), lambda i: (0, 0))],
        out_specs=pl.BlockSpec((bm, D_MODEL), lambda i: (i, 0)),
        out_shape=jax.ShapeDtypeStruct((m, D_MODEL), F32),
        compiler_params=_params(("parallel",), 32),
        name="final_norm",
    )(x, gain)


def _ffn(x, gain, w_gu, w_down, layer, name):
    act = _ffn_up(x, gain.reshape(1, -1), w_gu, layer)
    return _mm_res(act, w_down, layer, x, 0.5, name)


def _pad_lanes(v):
    return jnp.pad(v.reshape(1, -1), ((0, 0), (0, LANES - v.shape[-1])))


def kernel(x, mem, positions, norm_ffn1, w_ffn1_gu, w_ffn1_down, norm_mix, w_in_even, gm_ln_g, gm_ln_b, gm_ws, gm_bs, conv_w, conv_b, dt_bias, a_log, d_skip, ssd_norm, w_out_even, w_qkv, b_qkv, sinks, w_o_odd, norm_xq, norm_mem, w_xq, w_xkv, w_xo, norm_ffn2, w_ffn2_gu, w_ffn2_down, final_norm):
    bsz, seq, d = x.shape
    assert (bsz, seq, d) == (1, SEQ, D_MODEL)
    xr = x.reshape(seq, d)
    memr = mem.reshape(N_MEM, d)
    inv_freq = ROPE_THETA ** (-jnp.arange(0, ROT_DIM, 2, dtype=F32) / ROT_DIM)
    cos_t, sin_t = _rope_table(positions.reshape(1, seq), inv_freq.reshape(ROT_HALF, 1))
    head_of_lane = jnp.arange(D_MODEL, dtype=jnp.int32) // HEAD_DIM
    e1 = (jnp.arange(LANES, dtype=jnp.int32)[:, None] == head_of_lane[None, :]).astype(BF16)
    e3 = jnp.concatenate([e1, e1, e1], axis=0)

    for i in range(DEPTH):
        j = i // 2
        xr = _ffn(xr, norm_ffn1[i], w_ffn1_gu, w_ffn1_down, i, "ffn1_down")
        if i % 2 == 0:
            proj, dt_raw = _even_in(xr, norm_mix[i].reshape(1, -1), w_in_even, j)
            mix = _even_mix(
                proj, dt_raw, gm_ln_g[j].reshape(1, -1), gm_ln_b[j].reshape(1, -1), gm_ws[j], gm_bs[j].T,
                conv_w[j], conv_b[j].reshape(1, -1), _pad_lanes(dt_bias[j]), _pad_lanes(a_log[j]),
                jnp.repeat(d_skip[j], HEAD_DIM).reshape(1, -1), ssd_norm[j].reshape(1, -1), e3)
            xr = _mm_res(mix, w_out_even, j, xr, 1.0, "even_out")
        else:
            qkv = _qkv(xr, norm_mix[i].reshape(1, -1), w_qkv, j, b_qkv[j].reshape(1, -1), cos_t, sin_t)
            att = _swa(qkv, sinks[j])
            xr = _mm_res(att, w_o_odd, j, xr, 1.0, "odd_out")
        kv = _norm_mm(memr, norm_mem[i].reshape(1, -1), w_xkv, i, N_MEM, BF16, "mem_kv")
        xr = _xattn(xr, norm_xq[i].reshape(1, -1), w_xq, kv, w_xo, i)
        xr = _ffn(xr, norm_ffn2[i], w_ffn2_gu, w_ffn2_down, i, "ffn2_down")
    return _final_norm(xr, final_norm.reshape(1, -1)).reshape(bsz, seq, d)
```

```python
import functools

import jax
import jax.numpy as jnp
from jax import lax
from jax.experimental import pallas as pl
from jax.experimental.pallas import tpu as pltpu

F32 = jnp.float32
BF16 = jnp.bfloat16

D_MODEL = 2048
SEQ = 8192
DEPTH = 2
EPS = 1e-5
N_MEM = 256
D_FF = 5632
CHUNK = 128
GROUPS = 4
GDIM = D_MODEL // GROUPS
HEAD_DIM = 64
SSD_HEADS = 32
SSD_STATE = 128
SSD_CONV = 4
CONV_DIM = D_MODEL + 2 * GROUPS * SSD_STATE
EVEN_MAIN = 2 * D_MODEL + D_MODEL + CONV_DIM
ATT_HEADS = 32
ATT_REP = ATT_HEADS // GROUPS
ATT_SCALE = HEAD_DIM ** -0.5
ROT_DIM = HEAD_DIM // 4
ROT_HALF = ROT_DIM // 2
ROPE_THETA = 500000.0
KV_WIDTH = GROUPS * HEAD_DIM
ODD_IN = (ATT_HEADS + 2 * GROUPS) * HEAD_DIM
X_HEADS = 4
X_HEAD_DIM = 128
X_WIDTH = X_HEADS * X_HEAD_DIM
X_SCALE = X_HEAD_DIM ** -0.5

LANES = 128
SUBLANES = 8
CONV_TAIL = 16
BM = 1024
BN = 512
MIB = 1024 * 1024


def _params(semantics, vmem_mib):
    return pltpu.CompilerParams(dimension_semantics=semantics, vmem_limit_bytes=vmem_mib * MIB)


def _rms(x, g):
    ms = jnp.mean(x * x, axis=-1, keepdims=True)
    return x * lax.rsqrt(ms + EPS) * g


def _silu(x):
    return x * jax.nn.sigmoid(x)


def _gelu(x):
    return 0.5 * x * (1.0 + lax.erf(x * (2.0 ** -0.5)))


def _dot(a, b):
    return jnp.dot(a, b, preferred_element_type=F32)


def _dot_nt(a, b):
    return lax.dot_general(a, b, (((1,), (1,)), ((), ())), preferred_element_type=F32)


def _dot_tn(a, b):
    return lax.dot_general(a, b, (((0,), (0,)), ((), ())), preferred_element_type=F32)


def _cast_slab_spec(rows, steps_per_row_block, n_slabs, layer):
    slab = lambda i, j: jnp.minimum(i * steps_per_row_block + j, n_slabs - 1)
    return (pl.BlockSpec((None, rows, D_MODEL), lambda i, j: (layer, slab(i, j), 0)),
            pl.BlockSpec((rows, D_MODEL), lambda i, j: (slab(i, j), 0)))


def _ffn_up_body(x_ref, g_ref, wg_ref, wu_ref, wd_ref, o_ref, wd_out_ref, h_ref, w_ref):
    @pl.when(pl.program_id(1) == 0)
    def _():
        h_ref[...] = _rms(x_ref[...], g_ref[...]).astype(BF16)

    wd_out_ref[...] = wd_ref[...].astype(BF16)
    w_ref[:, :BN] = wg_ref[...].astype(BF16)
    w_ref[:, BN:] = wu_ref[...].astype(BF16)
    gu = _dot(h_ref[...], w_ref[...])
    o_ref[...] = (_silu(gu[:, :BN]) * gu[:, BN:]).astype(BF16)


def _ffn_up(x, gain, w_gu, w_down, layer):
    m = x.shape[0]
    nj = D_FF // BN
    steps = (m // BM) * nj
    slab_rows = D_FF // steps
    assert slab_rows * steps == D_FF
    wd_in_spec, wd_out_spec = _cast_slab_spec(slab_rows, nj, steps, layer)
    return pl.pallas_call(
        _ffn_up_body,
        grid=(m // BM, nj),
        in_specs=[
            pl.BlockSpec((BM, D_MODEL), lambda i, j: (i, 0)),
            pl.BlockSpec((1, D_MODEL), lambda i, j: (0, 0)),
            pl.BlockSpec((None, D_MODEL, BN), lambda i, j: (layer, 0, j)),
            pl.BlockSpec((None, D_MODEL, BN), lambda i, j: (layer, 0, j + nj)),
            wd_in_spec,
        ],
        out_specs=[pl.BlockSpec((BM, BN), lambda i, j: (i, j)), wd_out_spec],
        out_shape=[jax.ShapeDtypeStruct((m, D_FF), BF16), jax.ShapeDtypeStruct((D_FF, D_MODEL), BF16)],
        scratch_shapes=[pltpu.VMEM((BM, D_MODEL), BF16), pltpu.VMEM((D_MODEL, 2 * BN), BF16)],
        compiler_params=_params(("arbitrary", "arbitrary"), 56),
        name="ffn_up",
    )(x, gain, w_gu, w_gu, w_down)


def _mm_res_body(a_ref, w_ref, r_ref, o_ref, *, scale):
    o_ref[...] = r_ref[...] + scale * _dot(a_ref[...], w_ref[...])


def _mm_res(a, w, res, scale, name):
    m, k = a.shape
    n = w.shape[1]
    return pl.pallas_call(
        functools.partial(_mm_res_body, scale=scale),
        grid=(m // BM, n // BN),
        in_specs=[
            pl.BlockSpec((BM, k), lambda i, j: (i, 0)),
            pl.BlockSpec((k, BN), lambda i, j: (0, j)),
            pl.BlockSpec((BM, BN), lambda i, j: (i, j)),
        ],
        out_specs=pl.BlockSpec((BM, BN), lambda i, j: (i, j)),
        out_shape=jax.ShapeDtypeStruct((m, n), F32),
        compiler_params=_params(("parallel", "arbitrary"), 56),
        name=name,
    )(a, w, res)


def _norm_mm_body(x_ref, g_ref, w_ref, o_ref, h_ref):
    @pl.when(pl.program_id(1) == 0)
    def _():
        h_ref[...] = _rms(x_ref[...], g_ref[...]).astype(BF16)

    o_ref[...] = _dot(h_ref[...], w_ref[...].astype(BF16)).astype(o_ref.dtype)


def _norm_mm(x, gain, w, layer, bm, out_dtype, name):
    m, k = x.shape
    n = w.shape[2]
    return pl.pallas_call(
        _norm_mm_body,
        grid=(m // bm, n // BN),
        in_specs=[
            pl.BlockSpec((bm, k), lambda i, j: (i, 0)),
            pl.BlockSpec((1, k), lambda i, j: (0, 0)),
            pl.BlockSpec((None, k, BN), lambda i, j: (layer, 0, j)),
        ],
        out_specs=pl.BlockSpec((bm, BN), lambda i, j: (i, j)),
        out_shape=jax.ShapeDtypeStruct((m, n), out_dtype),
        scratch_shapes=[pltpu.VMEM((bm, k), BF16)],
        compiler_params=_params(("parallel", "arbitrary"), 40),
        name=name,
    )(x, gain, w)


def _even_in_body(x_ref, g_ref, w_ref, wdt_ref, wo_ref, o_ref, dt_ref, wo_out_ref, h_ref, *, n_gelu):
    j = pl.program_id(1)

    @pl.when(j == 0)
    def _():
        h = _rms(x_ref[...], g_ref[...]).astype(BF16)
        h_ref[...] = h
        row = lax.broadcasted_iota(jnp.int32, (LANES, 1), 0)
        dt_ref[...] = _dot_nt(h, jnp.where(row < SSD_HEADS, wdt_ref[...], 0.0).astype(BF16))

    wo_out_ref[...] = wo_ref[...].astype(BF16)
    acc = _dot_nt(h_ref[...], w_ref[...].astype(BF16))

    @pl.when(j < n_gelu)
    def _():
        o_ref[...] = _gelu(acc).astype(BF16)

    @pl.when(j >= n_gelu)
    def _():
        o_ref[...] = acc.astype(BF16)


def _even_in(x, gain, w_in_t, w_out, layer):
    m = x.shape[0]
    nj = EVEN_MAIN // BN
    k_out = w_out.shape[1]
    slab_rows = 32
    wo_in_spec, wo_out_spec = _cast_slab_spec(slab_rows, nj, k_out // slab_rows, layer)
    return pl.pallas_call(
        functools.partial(_even_in_body, n_gelu=2 * D_MODEL // BN),
        grid=(m // BM, nj),
        in_specs=[
            pl.BlockSpec((BM, D_MODEL), lambda i, j: (i, 0)),
            pl.BlockSpec((1, D_MODEL), lambda i, j: (0, 0)),
            pl.BlockSpec((None, BN, D_MODEL), lambda i, j: (layer, j, 0)),
            pl.BlockSpec((None, LANES, D_MODEL), lambda i, j: (layer, EVEN_MAIN // LANES, 0)),
            wo_in_spec,
        ],
        out_specs=[
            pl.BlockSpec((BM, BN), lambda i, j: (i, j)),
            pl.BlockSpec((BM, LANES), lambda i, j: (i, 0)),
            wo_out_spec,
        ],
        out_shape=[
            jax.ShapeDtypeStruct((m, EVEN_MAIN), BF16),
            jax.ShapeDtypeStruct((m, LANES), F32),
            jax.ShapeDtypeStruct((k_out, D_MODEL), BF16),
        ],
        scratch_shapes=[pltpu.VMEM((BM, D_MODEL), BF16)],
        compiler_params=_params(("arbitrary", "arbitrary"), 48),
        name="even_in",
    )(x, gain, w_in_t, w_in_t, w_out)


def _rope_table_body(pos_ref, invf_ref, cos_ref, sin_ref):
    ang = pos_ref[...].astype(F32) * invf_ref[...]
    cos_ref[...] = jnp.cos(ang)
    sin_ref[...] = jnp.sin(ang)


def _rope_table(pos_row, invf_col):
    shape = jax.ShapeDtypeStruct((ROT_HALF, pos_row.shape[1]), F32)
    return pl.pallas_call(_rope_table_body, out_shape=[shape, shape], name="rope_table")(pos_row, invf_col)


def _qkv_body(x_ref, g_ref, w_ref, b_ref, cost_ref, sint_ref, wo_ref, o_ref, wo_out_ref,
              h_ref, cos_ref, sn_ref, sp_ref):
    j = pl.program_id(1)

    @pl.when(j == 0)
    def _():
        h_ref[...] = _rms(x_ref[...], g_ref[...]).astype(BF16)
        reps = LANES // ROT_HALF
        cos = jnp.concatenate([cost_ref[...]] * reps, axis=0).T
        sin = jnp.concatenate([sint_ref[...]] * reps, axis=0).T
        lane = lax.broadcasted_iota(jnp.int32, (1, LANES), 1) % HEAD_DIM
        first = lane < ROT_HALF
        second = (lane >= ROT_HALF) & (lane < ROT_DIM)
        cos_ref[...] = jnp.where(first | second, cos, 1.0)
        sn_ref[...] = jnp.where(first, -sin, 0.0)
        sp_ref[...] = jnp.where(second, sin, 0.0)

    wo_out_ref[...] = wo_ref[...].astype(BF16)
    acc = _dot(h_ref[...], w_ref[...].astype(BF16)) + b_ref[...]

    def rope(a):
        return (a * cos_ref[...] + pltpu.roll(a, LANES - ROT_HALF, 1) * sn_ref[...]
                + pltpu.roll(a, ROT_HALF, 1) * sp_ref[...])

    is_q = j < ATT_HEADS * HEAD_DIM // BN
    scale = jnp.where(is_q, ATT_SCALE, 1.0)
    for t in range(BN // LANES):
        a = acc[:, t * LANES:(t + 1) * LANES]
        val = rope(a)
        if t >= KV_WIDTH // LANES:
            val = jnp.where(is_q, val, a)
        o_ref[:, t * LANES:(t + 1) * LANES] = (val * scale).astype(BF16)


def _qkv(x, gain, w, w_o, layer, b, cos_t, sin_t):
    m = x.shape[0]
    nj = ODD_IN // BN
    k_out = w_o.shape[1]
    slab_rows = 64
    wo_in_spec, wo_out_spec = _cast_slab_spec(slab_rows, nj, k_out // slab_rows, layer)
    return pl.pallas_call(
        _qkv_body,
        grid=(m // BM, nj),
        in_specs=[
            pl.BlockSpec((BM, D_MODEL), lambda i, j: (i, 0)),
            pl.BlockSpec((1, D_MODEL), lambda i, j: (0, 0)),
            pl.BlockSpec((None, D_MODEL, BN), lambda i, j: (layer, 0, j)),
            pl.BlockSpec((1, BN), lambda i, j: (0, j)),
            pl.BlockSpec((ROT_HALF, BM), lambda i, j: (0, i)),
            pl.BlockSpec((ROT_HALF, BM), lambda i, j: (0, i)),
            wo_in_spec,
        ],
        out_specs=[pl.BlockSpec((BM, BN), lambda i, j: (i, j)), wo_out_spec],
        out_shape=[jax.ShapeDtypeStruct((m, ODD_IN), BF16), jax.ShapeDtypeStruct((k_out, D_MODEL), BF16)],
        scratch_shapes=[
            pltpu.VMEM((BM, D_MODEL), BF16),
            pltpu.VMEM((BM, LANES), F32),
            pltpu.VMEM((BM, LANES), F32),
            pltpu.VMEM((BM, LANES), F32),
        ],
        compiler_params=_params(("arbitrary", "arbitrary"), 48),
        name="qkv_rope",
    )(x, gain, w, b, cos_t, sin_t, w_o)


def _split3(x):
    hi = x.astype(BF16)
    r1 = x - hi.astype(F32)
    mid = r1.astype(BF16)
    lo = (r1 - mid.astype(F32)).astype(BF16)
    return hi, mid, lo


def _even_mix_body(u_ref, v_ref, z_ref, xbc_ref, dt_ref, lng_ref, lnb_ref, ws_ref, bs_ref, cw_ref, cb_ref,
                   dtb_ref, alog_ref, dskip_ref, snorm_ref, e3_ref, shift_ref, o_ref, state_ref, xx_ref):
    c = pl.program_id(0)
    q = CHUNK

    @pl.when(c == 0)
    def _():
        state_ref[...] = jnp.zeros_like(state_ref)
        xx_ref[pl.ds(0, q), :] = jnp.zeros((q, CONV_DIM), BF16)

    row = lax.broadcasted_iota(jnp.int32, (q, q), 0)
    col = lax.broadcasted_iota(jnp.int32, (q, q), 1)
    causal = col <= row

    for g in range(GROUPS):
        seg = slice(g * GDIM, (g + 1) * GDIM)
        vg = v_ref[:, seg].astype(F32)
        mu = jnp.mean(vg, axis=-1, keepdims=True)
        d = vg - mu
        var = jnp.mean(d * d, axis=-1, keepdims=True)
        vn = d * lax.rsqrt(var + EPS) * lng_ref[:, seg] + lnb_ref[:, seg]
        w = jnp.where(causal, ws_ref[g], 0.0).astype(BF16)
        s = _dot(w, vn.astype(BF16)) + bs_ref[:, g:g + 1]
        o_ref[:, seg] = (u_ref[:, seg].astype(F32) * s).astype(BF16)

    x_cur = xbc_ref[...]
    xx_ref[pl.ds(q, q), :] = x_cur
    shifted = _dot(shift_ref[...], xx_ref[...])
    conv = cb_ref[...] + cw_ref[SSD_CONV - 1:SSD_CONV, :] * x_cur.astype(F32)
    for k in range(SSD_CONV - 1):
        conv = conv + cw_ref[k:k + 1, :] * shifted[k * q:(k + 1) * q]
    xx_ref[pl.ds(q - CONV_TAIL, CONV_TAIL), :] = x_cur[q - CONV_TAIL:, :]
    xbc = _silu(conv)
    xs = xbc[:, :D_MODEL]

    dt = jax.nn.softplus(dt_ref[...] + dtb_ref[...])
    a = dt * (-jnp.exp(alog_ref[...]))
    tri = jnp.where(causal, 1.0, 0.0).astype(BF16)
    a_hi, a_mid, a_lo = _split3(a)
    acs = _dot(tri, a_hi) + _dot(tri, a_mid) + _dot(tri, a_lo)
    acs_t = acs.T
    both = jnp.concatenate([dt, acs], axis=0)
    b_hi, b_mid, b_lo = _split3(both)
    both_e = _dot(jnp.concatenate([b_hi, b_mid, b_lo], axis=1), e3_ref[...])
    dt_e = both_e[:q]
    acs_e = both_e[q:]
    last_e = acs_e[q - 1:q, :]
    xdt = xs * dt_e
    xdec = (xdt * jnp.exp(last_e - acs_e)).astype(BF16)
    xdt_b = xdt.astype(BF16)
    grow_e = jnp.exp(acs_e)
    chunk_decay = jnp.exp(last_e)

    lane = lax.broadcasted_iota(jnp.int32, (q, LANES), 1)
    lo_half = lane < HEAD_DIM
    zf = z_ref[...].astype(F32)
    gate = _silu(zf)

    for g in range(GROUPS):
        seg = slice(g * GDIM, (g + 1) * GDIM)
        b_g = xbc[:, D_MODEL + g * SSD_STATE:D_MODEL + (g + 1) * SSD_STATE].astype(BF16)
        c_g = xbc[:, D_MODEL + GROUPS * SSD_STATE + g * SSD_STATE:
                  D_MODEL + GROUPS * SSD_STATE + (g + 1) * SSD_STATE].astype(BF16)
        cb_causal = jnp.where(causal, _dot_nt(c_g, b_g), 0.0)
        y_off = _dot(c_g, state_ref[g].astype(BF16)) * grow_e[:, seg]
        pieces = []
        for p in range(GDIM // LANES):
            mats = []
            for hh in range(2):
                h = g * (GDIM // HEAD_DIM) + 2 * p + hh
                seg_ij = jnp.minimum(acs[:, h:h + 1] - acs_t[h:h + 1, :], 0.0)
                mats.append((cb_causal * jnp.exp(seg_ij)).astype(BF16))
            x2 = xdt_b[:, g * GDIM + p * LANES:g * GDIM + (p + 1) * LANES]
            zero = jnp.zeros_like(x2)
            rhs = jnp.concatenate([jnp.where(lo_half, x2, zero), jnp.where(lo_half, zero, x2)], axis=0)
            pieces.append(_dot(jnp.concatenate(mats, axis=1), rhs))
        y_diag = jnp.concatenate(pieces, axis=1)
        new_states = _dot_tn(b_g, xdec[:, seg])
        state_ref[g] = state_ref[g] * chunk_decay[:, seg] + new_states
        y = y_diag + y_off + xs[:, seg] * dskip_ref[:, seg]
        y = y * gate[:, seg]
        y = y * lax.rsqrt(jnp.mean(y * y, axis=-1, keepdims=True) + EPS)
        o_ref[:, D_MODEL + g * GDIM:D_MODEL + (g + 1) * GDIM] = (y * snorm_ref[:, seg]).astype(BF16)


def _even_mix(proj, dt_raw, ln_g, ln_b, ws, bs_t, conv_w, conv_b, dt_bias, a_log, d_skip_e, ssd_norm, e3, shift):
    m = proj.shape[0]
    full = lambda shape: pl.BlockSpec(shape, lambda c: (0,) * len(shape))
    return pl.pallas_call(
        _even_mix_body,
        grid=(m // CHUNK,),
        in_specs=[
            pl.BlockSpec((CHUNK, D_MODEL), lambda c: (c, 0)),
            pl.BlockSpec((CHUNK, D_MODEL), lambda c: (c, 1)),
            pl.BlockSpec((CHUNK, D_MODEL), lambda c: (c, 2)),
            pl.BlockSpec((CHUNK, CONV_DIM), lambda c: (c, 2)),
            pl.BlockSpec((CHUNK, LANES), lambda c: (c, 0)),
            full((1, D_MODEL)), full((1, D_MODEL)),
            full((GROUPS, CHUNK, CHUNK)), full((CHUNK, GROUPS)),
            full((SSD_CONV, CONV_DIM)), full((1, CONV_DIM)),
            full((1, LANES)), full((1, LANES)),
            full((1, D_MODEL)), full((1, D_MODEL)),
            full((3 * LANES, D_MODEL)),
            full(((SSD_CONV - 1) * CHUNK, 2 * CHUNK)),
        ],
        out_specs=pl.BlockSpec((CHUNK, 2 * D_MODEL), lambda c: (c, 0)),
        out_shape=jax.ShapeDtypeStruct((m, 2 * D_MODEL), BF16),
        scratch_shapes=[
            pltpu.VMEM((GROUPS, SSD_STATE, GDIM), F32),
            pltpu.VMEM((2 * CHUNK, CONV_DIM), BF16),
        ],
        compiler_params=_params(("arbitrary",), 48),
        name="even_mix",
    )(proj, proj, proj, proj, dt_raw, ln_g, ln_b, ws, bs_t, conv_w, conv_b, dt_bias, a_log, d_skip_e, ssd_norm,
      e3, shift)


def _swa_body(sink_ref, q_ref, kv_ref, kvp_ref, o_ref):
    n = pl.program_id(0)
    w = CHUNK
    row = lax.broadcasted_iota(jnp.int32, (w, w), 0)
    col = lax.broadcasted_iota(jnp.int32, (w, w), 1)
    own = col <= row
    lo_half = lax.broadcasted_iota(jnp.int32, (w, LANES), 1) < HEAD_DIM
    prev_bias = jnp.where(n > 0, 0.0, -jnp.inf)

    def head_tiles(ref, base, k):
        t = ref[:, base + (k // 2) * LANES:base + (k // 2 + 1) * LANES].astype(F32)
        r = pltpu.roll(t, HEAD_DIM, 1)
        return (t, r) if k % 2 == 0 else (r, t)

    for k in range(GROUPS):
        k_lo, k_hi = head_tiles(kv_ref, 0, k)
        kp_lo, kp_hi = head_tiles(kvp_ref, 0, k)
        v_lo, v_hi = head_tiles(kv_ref, KV_WIDTH, k)
        vp_lo, vp_hi = head_tiles(kvp_ref, KV_WIDTH, k)
        kk = jnp.where(lo_half, k_lo, k_hi).astype(BF16)
        kkp = jnp.where(lo_half, kp_lo, kp_hi).astype(BF16)
        out = []
        for parity in range(2):
            if parity == 0:
                vv = jnp.concatenate([jnp.where(lo_half, v_lo, 1.0), jnp.where(lo_half, vp_lo, 1.0)], axis=0)
            else:
                vv = jnp.concatenate([jnp.where(lo_half, 1.0, v_hi), jnp.where(lo_half, 1.0, vp_hi)], axis=0)
            lhs = []
            for p in range(ATT_REP // 2):
                q2 = q_ref[:, k * GDIM + p * LANES:k * GDIM + (p + 1) * LANES]
                zero = jnp.zeros_like(q2)
                lhs.append(jnp.where(lo_half, q2, zero) if parity == 0 else jnp.where(lo_half, zero, q2))
            lhs = jnp.concatenate(lhs, axis=0)
            s_own = _dot_nt(lhs, kk)
            s_prev = _dot_nt(lhs, kkp)
            probs, esink = [], []
            for p in range(ATT_REP // 2):
                sink = sink_ref[k * ATT_REP + 2 * p + parity]
                s = jnp.where(own, s_own[p * w:(p + 1) * w], s_prev[p * w:(p + 1) * w] + prev_bias)
                mx = jnp.maximum(jnp.max(s, axis=-1, keepdims=True), sink)
                e = jnp.exp(s - mx)
                probs.append(jnp.concatenate([jnp.where(own, e, 0.0).astype(BF16),
                                              jnp.where(own, 0.0, e).astype(BF16)], axis=1))
                esink.append(jnp.exp(sink - mx))
            o = _dot(jnp.concatenate(probs, axis=0), vv.astype(BF16))
            out.append((o, esink))
        for p in range(ATT_REP // 2):
            o_even = out[0][0][p * w:(p + 1) * w]
            o_odd = out[1][0][p * w:(p + 1) * w]
            num = jnp.where(lo_half, o_even, o_odd)
            den = pltpu.roll(jnp.where(lo_half, o_odd, o_even), HEAD_DIM, 1)
            den = den + jnp.where(lo_half, out[0][1][p], out[1][1][p])
            o_ref[:, k * GDIM + p * LANES:k * GDIM + (p + 1) * LANES] = (num / den).astype(BF16)


def _swa(qkv, sinks):
    m = qkv.shape[0]
    kv_block = ATT_HEADS * HEAD_DIM // (2 * KV_WIDTH)
    return pl.pallas_call(
        _swa_body,
        grid=(m // CHUNK,),
        in_specs=[
            pl.BlockSpec(memory_space=pltpu.SMEM),
            pl.BlockSpec((CHUNK, D_MODEL), lambda n: (n, 0)),
            pl.BlockSpec((CHUNK, 2 * KV_WIDTH), lambda n: (n, kv_block)),
            pl.BlockSpec((CHUNK, 2 * KV_WIDTH), lambda n: (jnp.maximum(n - 1, 0), kv_block)),
        ],
        out_specs=pl.BlockSpec((CHUNK, D_MODEL), lambda n: (n, 0)),
        out_shape=jax.ShapeDtypeStruct((m, D_MODEL), BF16),
        compiler_params=_params(("parallel",), 32),
        name="swa",
    )(sinks, qkv, qkv, qkv)


def _xattn_body(x_ref, g_ref, wq_ref, kv_ref, wo_ref, o_ref, wq_b, wo_b):
    @pl.when(pl.program_id(0) == 0)
    def _():
        wq_b[...] = wq_ref[...].astype(BF16)
        wo_b[...] = wo_ref[...].astype(BF16)

    x = x_ref[...]
    h = _rms(x, g_ref[...]).astype(BF16)
    q = _dot(h, wq_b[...]).astype(BF16)
    outs = []
    for hd in range(X_HEADS):
        seg = slice(hd * X_HEAD_DIM, (hd + 1) * X_HEAD_DIM)
        k = kv_ref[:, seg]
        v = kv_ref[:, X_WIDTH + hd * X_HEAD_DIM:X_WIDTH + (hd + 1) * X_HEAD_DIM]
        s = _dot_nt(q[:, seg], k) * X_SCALE
        e = jnp.exp(s - jnp.max(s, axis=-1, keepdims=True))
        o = _dot(e.astype(BF16), v) * (1.0 / jnp.sum(e, axis=-1, keepdims=True))
        outs.append(o.astype(BF16))
    o_ref[...] = x + _dot(jnp.concatenate(outs, axis=1), wo_b[...])


def _xattn(x, gain, w_q, kv, w_o, layer):
    m = x.shape[0]
    bm = 512
    return pl.pallas_call(
        _xattn_body,
        grid=(m // bm,),
        in_specs=[
            pl.BlockSpec((bm, D_MODEL), lambda i: (i, 0)),
            pl.BlockSpec((1, D_MODEL), lambda i: (0, 0)),
            pl.BlockSpec((None, D_MODEL, X_WIDTH), lambda i: (layer, 0, 0)),
            pl.BlockSpec((N_MEM, 2 * X_WIDTH), lambda i: (0, 0)),
            pl.BlockSpec((None, X_WIDTH, D_MODEL), lambda i: (layer, 0, 0)),
        ],
        out_specs=pl.BlockSpec((bm, D_MODEL), lambda i: (i, 0)),
        out_shape=jax.ShapeDtypeStruct((m, D_MODEL), F32),
        scratch_shapes=[pltpu.VMEM((D_MODEL, X_WIDTH), BF16), pltpu.VMEM((X_WIDTH, D_MODEL), BF16)],
        compiler_params=_params(("arbitrary",), 48),
        name="xattn",
    )(x, gain, w_q, kv, w_o)


def _final_norm_body(x_ref, g_ref, o_ref):
    o_ref[...] = _rms(x_ref[...], g_ref[...])


def _final_norm(x, gain):
    m = x.shape[0]
    bm = 512
    return pl.pallas_call(
        _final_norm_body,
        grid=(m // bm,),
        in_specs=[pl.BlockSpec((bm, D_MODEL), lambda i: (i, 0)), pl.BlockSpec((1, D_MODEL), lambda i: (0, 0))],
        out_specs=pl.BlockSpec((bm, D_MODEL), lambda i: (i, 0)),
        out_shape=jax.ShapeDtypeStruct((m, D_MODEL), F32),
        compiler_params=_params(("parallel",), 32),
        name="final_norm",
    )(x, gain)


def _ffn(x, gain, w_gu, w_down, layer, name):
    act, w_down_b = _ffn_up(x, gain.reshape(1, -1), w_gu, w_down, layer)
    return _mm_res(act, w_down_b, x, 0.5, name)


def _pad_lanes(v):
    return jnp.pad(v.reshape(1, -1), ((0, 0), (0, LANES - v.shape[-1])))


def kernel(x, mem, positions, norm_ffn1, w_ffn1_gu, w_ffn1_down, norm_mix, w_in_even, gm_ln_g, gm_ln_b, gm_ws, gm_bs, conv_w, conv_b, dt_bias, a_log, d_skip, ssd_norm, w_out_even, w_qkv, b_qkv, sinks, w_o_odd, norm_xq, norm_mem, w_xq, w_xkv, w_xo, norm_ffn2, w_ffn2_gu, w_ffn2_down, final_norm):
    bsz, seq, d = x.shape
    assert (bsz, seq, d) == (1, SEQ, D_MODEL)
    xr = x.reshape(seq, d)
    memr = mem.reshape(N_MEM, d)
    inv_freq = ROPE_THETA ** (-jnp.arange(0, ROT_DIM, 2, dtype=F32) / ROT_DIM)
    cos_t, sin_t = _rope_table(positions.reshape(1, seq), inv_freq.reshape(ROT_HALF, 1))
    head_of_lane = jnp.arange(D_MODEL, dtype=jnp.int32) // HEAD_DIM
    e1 = (jnp.arange(LANES, dtype=jnp.int32)[:, None] == head_of_lane[None, :]).astype(BF16)
    e3 = jnp.concatenate([e1, e1, e1], axis=0)
    sel_row = jnp.arange((SSD_CONV - 1) * CHUNK, dtype=jnp.int32)[:, None]
    sel_col = jnp.arange(2 * CHUNK, dtype=jnp.int32)[None, :]
    shift = (sel_col == CHUNK + sel_row % CHUNK - (SSD_CONV - 1) + sel_row // CHUNK).astype(BF16)

    w_in_t = jnp.swapaxes(w_in_even, 1, 2)

    for i in range(DEPTH):
        j = i // 2
        xr = _ffn(xr, norm_ffn1[i], w_ffn1_gu, w_ffn1_down, i, "ffn1_down")
        if i % 2 == 0:
            proj, dt_raw, w_out_b = _even_in(xr, norm_mix[i].reshape(1, -1), w_in_t, w_out_even, j)
            mix = _even_mix(
                proj, dt_raw, gm_ln_g[j].reshape(1, -1), gm_ln_b[j].reshape(1, -1), gm_ws[j], gm_bs[j].T,
                conv_w[j], conv_b[j].reshape(1, -1), _pad_lanes(dt_bias[j]), _pad_lanes(a_log[j]),
                jnp.repeat(d_skip[j], HEAD_DIM).reshape(1, -1), ssd_norm[j].reshape(1, -1), e3, shift)
            xr = _mm_res(mix, w_out_b, xr, 1.0, "even_out")
        else:
            qkv, w_o_b = _qkv(xr, norm_mix[i].reshape(1, -1), w_qkv, w_o_odd, j, b_qkv[j].reshape(1, -1),
                              cos_t, sin_t)
            att = _swa(qkv, sinks[j])
            xr = _mm_res(att, w_o_b, xr, 1.0, "odd_out")
        kv = _norm_mm(memr, norm_mem[i].reshape(1, -1), w_xkv, i, N_MEM, BF16, "mem_kv")
        xr = _xattn(xr, norm_xq[i].reshape(1, -1), w_xq, kv, w_xo, i)
        xr = _ffn(xr, norm_ffn2[i], w_ffn2_gu, w_ffn2_down, i, "ffn2_down")
    return _final_norm(xr, final_norm.reshape(1, -1)).reshape(bsz, seq, d)
```

```python
import functools

import jax
import jax.numpy as jnp
from jax import lax
from jax.experimental import pallas as pl
from jax.experimental.pallas import tpu as pltpu

F32 = jnp.float32
BF16 = jnp.bfloat16

D_MODEL = 2048
SEQ = 8192
DEPTH = 2
EPS = 1e-5
N_MEM = 256
D_FF = 5632
CHUNK = 128
GROUPS = 4
GDIM = D_MODEL // GROUPS
HEAD_DIM = 64
SSD_HEADS = 32
SSD_STATE = 128
SSD_CONV = 4
CONV_DIM = D_MODEL + 2 * GROUPS * SSD_STATE
EVEN_MAIN = 2 * D_MODEL + D_MODEL + CONV_DIM
ATT_HEADS = 32
ATT_REP = ATT_HEADS // GROUPS
ATT_SCALE = HEAD_DIM ** -0.5
ROT_DIM = HEAD_DIM // 4
ROT_HALF = ROT_DIM // 2
ROPE_THETA = 500000.0
KV_WIDTH = GROUPS * HEAD_DIM
ODD_IN = (ATT_HEADS + 2 * GROUPS) * HEAD_DIM
X_HEADS = 4
X_HEAD_DIM = 128
X_WIDTH = X_HEADS * X_HEAD_DIM
X_SCALE = X_HEAD_DIM ** -0.5

LANES = 128
SUBLANES = 8
CONV_TAIL = 16
BM = 1024
BN = 512
MIB = 1024 * 1024


def _params(semantics, vmem_mib):
    return pltpu.CompilerParams(dimension_semantics=semantics, vmem_limit_bytes=vmem_mib * MIB)


def _rms(x, g):
    ms = jnp.mean(x * x, axis=-1, keepdims=True)
    return x * lax.rsqrt(ms + EPS) * g


def _silu(x):
    return x * jax.nn.sigmoid(x)


def _gelu(x):
    return 0.5 * x * (1.0 + lax.erf(x * (2.0 ** -0.5)))


def _dot(a, b):
    return jnp.dot(a, b, preferred_element_type=F32)


def _dot_nt(a, b):
    return lax.dot_general(a, b, (((1,), (1,)), ((), ())), preferred_element_type=F32)


def _dot_tn(a, b):
    return lax.dot_general(a, b, (((0,), (0,)), ((), ())), preferred_element_type=F32)


def _snake(i, j, nj):
    return jnp.where(i % 2 == 0, j, nj - 1 - j)


def _cast_specs(w, layer, rows, steps_per_row_block):
    _, r, c = w.shape
    n_slabs = pl.cdiv(r, rows)
    slab = lambda i, j: jnp.minimum(i * steps_per_row_block + j, n_slabs - 1)
    return (pl.BlockSpec((None, rows, c), lambda i, j: (layer, slab(i, j), 0)),
            pl.BlockSpec((rows, c), lambda i, j: (slab(i, j), 0)),
            jax.ShapeDtypeStruct((r, c), BF16))


def _ffn_up_body(x_ref, g_ref, wg_ref, wu_ref, *refs, n_cast):
    cast_in, o_ref, cast_out = refs[:n_cast], refs[n_cast], refs[n_cast + 1:2 * n_cast + 1]
    h_ref, w_ref = refs[2 * n_cast + 1:]

    @pl.when(pl.program_id(1) == 0)
    def _():
        h_ref[...] = _rms(x_ref[...], g_ref[...]).astype(BF16)

    for src, dst in zip(cast_in, cast_out):
        dst[...] = src[...].astype(BF16)
    w_ref[:, :BN] = wg_ref[...].astype(BF16)
    w_ref[:, BN:] = wu_ref[...].astype(BF16)
    gu = _dot(h_ref[...], w_ref[...])
    o_ref[...] = (_silu(gu[:, :BN]) * gu[:, BN:]).astype(BF16)


def _ffn_up(x, gain, w_gu, layer, casts):
    m = x.shape[0]
    nj = D_FF // BN
    if w_gu.ndim == 3:
        w_spec = lambda off: pl.BlockSpec((None, D_MODEL, BN), lambda i, j: (layer, 0, _snake(i, j, nj) + off))
    else:
        w_spec = lambda off: pl.BlockSpec((D_MODEL, BN), lambda i, j: (0, _snake(i, j, nj) + off))
    cast_specs = [_cast_specs(w, l, rows, nj) for w, l, rows in casts]
    return pl.pallas_call(
        functools.partial(_ffn_up_body, n_cast=len(casts)),
        grid=(m // BM, nj),
        in_specs=[
            pl.BlockSpec((BM, D_MODEL), lambda i, j: (i, 0)),
            pl.BlockSpec((1, D_MODEL), lambda i, j: (0, 0)),
            w_spec(0), w_spec(nj),
        ] + [s[0] for s in cast_specs],
        out_specs=[pl.BlockSpec((BM, BN), lambda i, j: (i, _snake(i, j, nj)))] + [s[1] for s in cast_specs],
        out_shape=[jax.ShapeDtypeStruct((m, D_FF), BF16)] + [s[2] for s in cast_specs],
        scratch_shapes=[pltpu.VMEM((BM, D_MODEL), BF16), pltpu.VMEM((D_MODEL, 2 * BN), BF16)],
        compiler_params=_params(("arbitrary", "arbitrary"), 56),
        name="ffn_up",
    )(x, gain, w_gu, w_gu, *[w for w, _, _ in casts])


def _mm_res_body(a_ref, w_ref, r_ref, o_ref, *, scale):
    o_ref[...] = r_ref[...] + scale * _dot(a_ref[...], w_ref[...])


def _mm_res(a, w, res, scale, name):
    m, k = a.shape
    n = w.shape[1]
    nj = n // BN
    return pl.pallas_call(
        functools.partial(_mm_res_body, scale=scale),
        grid=(m // BM, nj),
        in_specs=[
            pl.BlockSpec((BM, k), lambda i, j: (i, 0)),
            pl.BlockSpec((k, BN), lambda i, j: (0, _snake(i, j, nj))),
            pl.BlockSpec((BM, BN), lambda i, j: (i, _snake(i, j, nj))),
        ],
        out_specs=pl.BlockSpec((BM, BN), lambda i, j: (i, _snake(i, j, nj))),
        out_shape=jax.ShapeDtypeStruct((m, n), F32),
        compiler_params=_params(("arbitrary", "arbitrary"), 56),
        name=name,
    )(a, w, res)


def _norm_mm_body(x_ref, g_ref, w_ref, o_ref, h_ref):
    @pl.when(pl.program_id(1) == 0)
    def _():
        h_ref[...] = _rms(x_ref[...], g_ref[...]).astype(BF16)

    o_ref[...] = _dot(h_ref[...], w_ref[...].astype(BF16)).astype(o_ref.dtype)


def _norm_mm(x, gain, w, layer, bm, out_dtype, name):
    m, k = x.shape
    n = w.shape[2]
    return pl.pallas_call(
        _norm_mm_body,
        grid=(m // bm, n // BN),
        in_specs=[
            pl.BlockSpec((bm, k), lambda i, j: (i, 0)),
            pl.BlockSpec((1, k), lambda i, j: (0, 0)),
            pl.BlockSpec((None, k, BN), lambda i, j: (layer, 0, j)),
        ],
        out_specs=pl.BlockSpec((bm, BN), lambda i, j: (i, j)),
        out_shape=jax.ShapeDtypeStruct((m, n), out_dtype),
        scratch_shapes=[pltpu.VMEM((bm, k), BF16)],
        compiler_params=_params(("parallel", "arbitrary"), 40),
        name=name,
    )(x, gain, w)


def _even_in_body(x_ref, g_ref, w_ref, wdt_ref, wo_ref, o_ref, dt_ref, wo_out_ref, h_ref, *, n_gelu, nj):
    j = pl.program_id(1)

    @pl.when(j == 0)
    def _():
        h = _rms(x_ref[...], g_ref[...]).astype(BF16)
        h_ref[...] = h
        row = lax.broadcasted_iota(jnp.int32, (LANES, 1), 0)
        dt_ref[...] = _dot_nt(h, jnp.where(row < SSD_HEADS, wdt_ref[...], 0.0).astype(BF16))

    wo_out_ref[...] = wo_ref[...].astype(BF16)
    acc = _dot_nt(h_ref[...], w_ref[...].astype(BF16))
    col_block = _snake(pl.program_id(0), j, nj)

    @pl.when(col_block < n_gelu)
    def _():
        o_ref[...] = _gelu(acc).astype(BF16)

    @pl.when(col_block >= n_gelu)
    def _():
        o_ref[...] = acc.astype(BF16)


def _even_in(x, gain, w_in_t, w_out, layer):
    m = x.shape[0]
    nj = EVEN_MAIN // BN
    wo_in_spec, wo_out_spec, wo_shape = _cast_specs(w_out, layer, 32, nj)
    return pl.pallas_call(
        functools.partial(_even_in_body, n_gelu=2 * D_MODEL // BN, nj=nj),
        grid=(m // BM, nj),
        in_specs=[
            pl.BlockSpec((BM, D_MODEL), lambda i, j: (i, 0)),
            pl.BlockSpec((1, D_MODEL), lambda i, j: (0, 0)),
            pl.BlockSpec((None, BN, D_MODEL), lambda i, j: (layer, _snake(i, j, nj), 0)),
            pl.BlockSpec((None, LANES, D_MODEL), lambda i, j: (layer, EVEN_MAIN // LANES, 0)),
            wo_in_spec,
        ],
        out_specs=[
            pl.BlockSpec((BM, BN), lambda i, j: (i, _snake(i, j, nj))),
            pl.BlockSpec((BM, LANES), lambda i, j: (i, 0)),
            wo_out_spec,
        ],
        out_shape=[
            jax.ShapeDtypeStruct((m, EVEN_MAIN), BF16),
            jax.ShapeDtypeStruct((m, LANES), F32),
            wo_shape,
        ],
        scratch_shapes=[pltpu.VMEM((BM, D_MODEL), BF16)],
        compiler_params=_params(("arbitrary", "arbitrary"), 48),
        name="even_in",
    )(x, gain, w_in_t, w_in_t, w_out)


def _rope_table_body(pos_ref, invf_ref, cos_ref, sin_ref):
    ang = pos_ref[...].astype(F32) * invf_ref[...]
    cos_ref[...] = jnp.cos(ang)
    sin_ref[...] = jnp.sin(ang)


def _rope_table(pos_row, invf_col):
    shape = jax.ShapeDtypeStruct((ROT_HALF, pos_row.shape[1]), F32)
    return pl.pallas_call(_rope_table_body, out_shape=[shape, shape], name="rope_table")(pos_row, invf_col)


def _qkv_body(x_ref, g_ref, w_ref, b_ref, cost_ref, sint_ref, wo_ref, o_ref, wo_out_ref,
              h_ref, cos_ref, sn_ref, sp_ref):
    j = pl.program_id(1)

    @pl.when(j == 0)
    def _():
        h_ref[...] = _rms(x_ref[...], g_ref[...]).astype(BF16)
        reps = LANES // ROT_HALF
        cos = jnp.concatenate([cost_ref[...]] * reps, axis=0).T
        sin = jnp.concatenate([sint_ref[...]] * reps, axis=0).T
        lane = lax.broadcasted_iota(jnp.int32, (1, LANES), 1) % HEAD_DIM
        first = lane < ROT_HALF
        second = (lane >= ROT_HALF) & (lane < ROT_DIM)
        cos_ref[...] = jnp.where(first | second, cos, 1.0)
        sn_ref[...] = jnp.where(first, -sin, 0.0)
        sp_ref[...] = jnp.where(second, sin, 0.0)

    wo_out_ref[...] = wo_ref[...].astype(BF16)
    acc = _dot(h_ref[...], w_ref[...]) + b_ref[...]

    def rope(a):
        return (a * cos_ref[...] + pltpu.roll(a, LANES - ROT_HALF, 1) * sn_ref[...]
                + pltpu.roll(a, ROT_HALF, 1) * sp_ref[...])

    is_q = _snake(pl.program_id(0), j, ODD_IN // BN) < ATT_HEADS * HEAD_DIM // BN
    scale = jnp.where(is_q, ATT_SCALE, 1.0)
    for t in range(BN // LANES):
        a = acc[:, t * LANES:(t + 1) * LANES]
        val = rope(a)
        if t >= KV_WIDTH // LANES:
            val = jnp.where(is_q, val, a)
        o_ref[:, t * LANES:(t + 1) * LANES] = (val * scale).astype(BF16)


def _qkv(x, gain, w_b, w_o, layer, b, cos_t, sin_t):
    m = x.shape[0]
    nj = ODD_IN // BN
    wo_in_spec, wo_out_spec, wo_shape = _cast_specs(w_o, layer, 64, nj)
    return pl.pallas_call(
        _qkv_body,
        grid=(m // BM, nj),
        in_specs=[
            pl.BlockSpec((BM, D_MODEL), lambda i, j: (i, 0)),
            pl.BlockSpec((1, D_MODEL), lambda i, j: (0, 0)),
            pl.BlockSpec((D_MODEL, BN), lambda i, j: (0, _snake(i, j, nj))),
            pl.BlockSpec((1, BN), lambda i, j: (0, _snake(i, j, nj))),
            pl.BlockSpec((ROT_HALF, BM), lambda i, j: (0, i)),
            pl.BlockSpec((ROT_HALF, BM), lambda i, j: (0, i)),
            wo_in_spec,
        ],
        out_specs=[pl.BlockSpec((BM, BN), lambda i, j: (i, _snake(i, j, nj))), wo_out_spec],
        out_shape=[jax.ShapeDtypeStruct((m, ODD_IN), BF16), wo_shape],
        scratch_shapes=[
            pltpu.VMEM((BM, D_MODEL), BF16),
            pltpu.VMEM((BM, LANES), F32),
            pltpu.VMEM((BM, LANES), F32),
            pltpu.VMEM((BM, LANES), F32),
        ],
        compiler_params=_params(("arbitrary", "arbitrary"), 48),
        name="qkv_rope",
    )(x, gain, w_b, b, cos_t, sin_t, w_o)


def _split3(x):
    hi = x.astype(BF16)
    r1 = x - hi.astype(F32)
    mid = r1.astype(BF16)
    lo = (r1 - mid.astype(F32)).astype(BF16)
    return hi, mid, lo


def _even_mix_body(u_ref, v_ref, z_ref, xbc_ref, dt_ref, lng_ref, lnb_ref, ws_ref, bs_ref, cw_ref, cb_ref,
                   dtb_ref, alog_ref, dskip_ref, snorm_ref, e3_ref, shift_ref, o_ref, state_ref, xx_ref):
    c = pl.program_id(0)
    q = CHUNK

    @pl.when(c == 0)
    def _():
        state_ref[...] = jnp.zeros_like(state_ref)
        xx_ref[pl.ds(0, q), :] = jnp.zeros((q, CONV_DIM), BF16)

    row = lax.broadcasted_iota(jnp.int32, (q, q), 0)
    col = lax.broadcasted_iota(jnp.int32, (q, q), 1)
    causal = col <= row

    for g in range(GROUPS):
        seg = slice(g * GDIM, (g + 1) * GDIM)
        vg = v_ref[:, seg].astype(F32)
        mu = jnp.mean(vg, axis=-1, keepdims=True)
        d = vg - mu
        var = jnp.mean(d * d, axis=-1, keepdims=True)
        vn = d * lax.rsqrt(var + EPS) * lng_ref[:, seg] + lnb_ref[:, seg]
        w = jnp.where(causal, ws_ref[g], 0.0).astype(BF16)
        s = _dot(w, vn.astype(BF16)) + bs_ref[:, g:g + 1]
        o_ref[:, seg] = (u_ref[:, seg].astype(F32) * s).astype(BF16)

    x_cur = xbc_ref[...]
    xx_ref[pl.ds(q, q), :] = x_cur
    shifted = _dot(shift_ref[...], xx_ref[...])
    conv = cb_ref[...] + cw_ref[SSD_CONV - 1:SSD_CONV, :] * x_cur.astype(F32)
    for k in range(SSD_CONV - 1):
        conv = conv + cw_ref[k:k + 1, :] * shifted[k * q:(k + 1) * q]
    xx_ref[pl.ds(q - CONV_TAIL, CONV_TAIL), :] = x_cur[q - CONV_TAIL:, :]
    xbc = _silu(conv)
    xs = xbc[:, :D_MODEL]

    dt = jax.nn.softplus(dt_ref[...] + dtb_ref[...])
    a = dt * (-jnp.exp(alog_ref[...]))
    tri = jnp.where(causal, 1.0, 0.0).astype(BF16)
    a_hi, a_mid, a_lo = _split3(a)
    acs = _dot(tri, a_hi) + _dot(tri, a_mid) + _dot(tri, a_lo)
    acs_t = acs.T
    both = jnp.concatenate([dt, acs], axis=0)
    b_hi, b_mid, b_lo = _split3(both)
    both_e = _dot(jnp.concatenate([b_hi, b_mid, b_lo], axis=1), e3_ref[...])
    dt_e = both_e[:q]
    acs_e = both_e[q:]
    last_e = acs_e[q - 1:q, :]
    xdt = xs * dt_e
    xdec = (xdt * jnp.exp(last_e - acs_e)).astype(BF16)
    xdt_b = xdt.astype(BF16)
    grow_e = jnp.exp(acs_e)
    chunk_decay = jnp.exp(last_e)

    lane = lax.broadcasted_iota(jnp.int32, (q, LANES), 1)
    lo_half = lane < HEAD_DIM
    zf = z_ref[...].astype(F32)
    gate = _silu(zf)

    for g in range(GROUPS):
        seg = slice(g * GDIM, (g + 1) * GDIM)
        b_g = xbc[:, D_MODEL + g * SSD_STATE:D_MODEL + (g + 1) * SSD_STATE].astype(BF16)
        c_g = xbc[:, D_MODEL + GROUPS * SSD_STATE + g * SSD_STATE:
                  D_MODEL + GROUPS * SSD_STATE + (g + 1) * SSD_STATE].astype(BF16)
        cb_causal = jnp.where(causal, _dot_nt(c_g, b_g), 0.0)
        y_off = _dot(c_g, state_ref[g].astype(BF16)) * grow_e[:, seg]
        pieces = []
        for p in range(GDIM // LANES):
            mats = []
            for hh in range(2):
                h = g * (GDIM // HEAD_DIM) + 2 * p + hh
                seg_ij = jnp.minimum(acs[:, h:h + 1] - acs_t[h:h + 1, :], 0.0)
                mats.append((cb_causal * jnp.exp(seg_ij)).astype(BF16))
            x2 = xdt_b[:, g * GDIM + p * LANES:g * GDIM + (p + 1) * LANES]
            zero = jnp.zeros_like(x2)
            rhs = jnp.concatenate([jnp.where(lo_half, x2, zero), jnp.where(lo_half, zero, x2)], axis=0)
            pieces.append(_dot(jnp.concatenate(mats, axis=1), rhs))
        y_diag = jnp.concatenate(pieces, axis=1)
        new_states = _dot_tn(b_g, xdec[:, seg])
        state_ref[g] = state_ref[g] * chunk_decay[:, seg] + new_states
        y = y_diag + y_off + xs[:, seg] * dskip_ref[:, seg]
        y = y * gate[:, seg]
        y = y * lax.rsqrt(jnp.mean(y * y, axis=-1, keepdims=True) + EPS)
        o_ref[:, D_MODEL + g * GDIM:D_MODEL + (g + 1) * GDIM] = (y * snorm_ref[:, seg]).astype(BF16)


def _even_mix(proj, dt_raw, ln_g, ln_b, ws, bs_t, conv_w, conv_b, dt_bias, a_log, d_skip_e, ssd_norm, e3, shift):
    m = proj.shape[0]
    full = lambda shape: pl.BlockSpec(shape, lambda c: (0,) * len(shape))
    return pl.pallas_call(
        _even_mix_body,
        grid=(m // CHUNK,),
        in_specs=[
            pl.BlockSpec((CHUNK, D_MODEL), lambda c: (c, 0)),
            pl.BlockSpec((CHUNK, D_MODEL), lambda c: (c, 1)),
            pl.BlockSpec((CHUNK, D_MODEL), lambda c: (c, 2)),
            pl.BlockSpec((CHUNK, CONV_DIM), lambda c: (c, 2)),
            pl.BlockSpec((CHUNK, LANES), lambda c: (c, 0)),
            full((1, D_MODEL)), full((1, D_MODEL)),
            full((GROUPS, CHUNK, CHUNK)), full((CHUNK, GROUPS)),
            full((SSD_CONV, CONV_DIM)), full((1, CONV_DIM)),
            full((1, LANES)), full((1, LANES)),
            full((1, D_MODEL)), full((1, D_MODEL)),
            full((3 * LANES, D_MODEL)),
            full(((SSD_CONV - 1) * CHUNK, 2 * CHUNK)),
        ],
        out_specs=pl.BlockSpec((CHUNK, 2 * D_MODEL), lambda c: (c, 0)),
        out_shape=jax.ShapeDtypeStruct((m, 2 * D_MODEL), BF16),
        scratch_shapes=[
            pltpu.VMEM((GROUPS, SSD_STATE, GDIM), F32),
            pltpu.VMEM((2 * CHUNK, CONV_DIM), BF16),
        ],
        compiler_params=_params(("arbitrary",), 48),
        name="even_mix",
    )(proj, proj, proj, proj, dt_raw, ln_g, ln_b, ws, bs_t, conv_w, conv_b, dt_bias, a_log, d_skip_e, ssd_norm,
      e3, shift)


def _swa_body(sink_ref, q_ref, kv_ref, kvp_ref, o_ref):
    n = pl.program_id(0)
    w = CHUNK
    row = lax.broadcasted_iota(jnp.int32, (w, w), 0)
    col = lax.broadcasted_iota(jnp.int32, (w, w), 1)
    own = col <= row
    lo_half = lax.broadcasted_iota(jnp.int32, (w, LANES), 1) < HEAD_DIM
    prev_bias = jnp.where(n > 0, 0.0, -jnp.inf)

    def head_tiles(ref, base, k):
        t = ref[:, base + (k // 2) * LANES:base + (k // 2 + 1) * LANES].astype(F32)
        r = pltpu.roll(t, HEAD_DIM, 1)
        return (t, r) if k % 2 == 0 else (r, t)

    for k in range(GROUPS):
        k_lo, k_hi = head_tiles(kv_ref, 0, k)
        kp_lo, kp_hi = head_tiles(kvp_ref, 0, k)
        v_lo, v_hi = head_tiles(kv_ref, KV_WIDTH, k)
        vp_lo, vp_hi = head_tiles(kvp_ref, KV_WIDTH, k)
        kk = jnp.where(lo_half, k_lo, k_hi).astype(BF16)
        kkp = jnp.where(lo_half, kp_lo, kp_hi).astype(BF16)
        out = []
        for parity in range(2):
            if parity == 0:
                vv = jnp.concatenate([jnp.where(lo_half, v_lo, 1.0), jnp.where(lo_half, vp_lo, 1.0)], axis=0)
            else:
                vv = jnp.concatenate([jnp.where(lo_half, 1.0, v_hi), jnp.where(lo_half, 1.0, vp_hi)], axis=0)
            lhs = []
            for p in range(ATT_REP // 2):
                q2 = q_ref[:, k * GDIM + p * LANES:k * GDIM + (p + 1) * LANES]
                zero = jnp.zeros_like(q2)
                lhs.append(jnp.where(lo_half, q2, zero) if parity == 0 else jnp.where(lo_half, zero, q2))
            lhs = jnp.concatenate(lhs, axis=0)
            s_own = _dot_nt(lhs, kk)
            s_prev = _dot_nt(lhs, kkp)
            probs, esink = [], []
            for p in range(ATT_REP // 2):
                sink = sink_ref[k * ATT_REP + 2 * p + parity]
                s = jnp.where(own, s_own[p * w:(p + 1) * w], s_prev[p * w:(p + 1) * w] + prev_bias)
                mx = jnp.maximum(jnp.max(s, axis=-1, keepdims=True), sink)
                e = jnp.exp(s - mx)
                probs.append(jnp.concatenate([jnp.where(own, e, 0.0).astype(BF16),
                                              jnp.where(own, 0.0, e).astype(BF16)], axis=1))
                esink.append(jnp.exp(sink - mx))
            o = _dot(jnp.concatenate(probs, axis=0), vv.astype(BF16))
            out.append((o, esink))
        for p in range(ATT_REP // 2):
            o_even = out[0][0][p * w:(p + 1) * w]
            o_odd = out[1][0][p * w:(p + 1) * w]
            num = jnp.where(lo_half, o_even, o_odd)
            den = pltpu.roll(jnp.where(lo_half, o_odd, o_even), HEAD_DIM, 1)
            den = den + jnp.where(lo_half, out[0][1][p], out[1][1][p])
            o_ref[:, k * GDIM + p * LANES:k * GDIM + (p + 1) * LANES] = (num / den).astype(BF16)


def _swa(qkv, sinks):
    m = qkv.shape[0]
    kv_block = ATT_HEADS * HEAD_DIM // (2 * KV_WIDTH)
    return pl.pallas_call(
        _swa_body,
        grid=(m // CHUNK,),
        in_specs=[
            pl.BlockSpec(memory_space=pltpu.SMEM),
            pl.BlockSpec((CHUNK, D_MODEL), lambda n: (n, 0)),
            pl.BlockSpec((CHUNK, 2 * KV_WIDTH), lambda n: (n, kv_block)),
            pl.BlockSpec((CHUNK, 2 * KV_WIDTH), lambda n: (jnp.maximum(n - 1, 0), kv_block)),
        ],
        out_specs=pl.BlockSpec((CHUNK, D_MODEL), lambda n: (n, 0)),
        out_shape=jax.ShapeDtypeStruct((m, D_MODEL), BF16),
        compiler_params=_params(("parallel",), 32),
        name="swa",
    )(sinks, qkv, qkv, qkv)


def _xattn_body(x_ref, g_ref, wq_ref, kv_ref, wo_ref, o_ref, wq_b, wo_b):
    @pl.when(pl.program_id(0) == 0)
    def _():
        wq_b[...] = wq_ref[...].astype(BF16)
        wo_b[...] = wo_ref[...].astype(BF16)

    x = x_ref[...]
    h = _rms(x, g_ref[...]).astype(BF16)
    q = _dot(h, wq_b[...]).astype(BF16)
    outs = []
    for hd in range(X_HEADS):
        seg = slice(hd * X_HEAD_DIM, (hd + 1) * X_HEAD_DIM)
        k = kv_ref[:, seg]
        v = kv_ref[:, X_WIDTH + hd * X_HEAD_DIM:X_WIDTH + (hd + 1) * X_HEAD_DIM]
        s = _dot_nt(q[:, seg], k) * X_SCALE
        e = jnp.exp(s - jnp.max(s, axis=-1, keepdims=True))
        o = _dot(e.astype(BF16), v) * (1.0 / jnp.sum(e, axis=-1, keepdims=True))
        outs.append(o.astype(BF16))
    o_ref[...] = x + _dot(jnp.concatenate(outs, axis=1), wo_b[...])


def _xattn(x, gain, w_q, kv, w_o, layer):
    m = x.shape[0]
    bm = 512
    return pl.pallas_call(
        _xattn_body,
        grid=(m // bm,),
        in_specs=[
            pl.BlockSpec((bm, D_MODEL), lambda i: (i, 0)),
            pl.BlockSpec((1, D_MODEL), lambda i: (0, 0)),
            pl.BlockSpec((None, D_MODEL, X_WIDTH), lambda i: (layer, 0, 0)),
            pl.BlockSpec((N_MEM, 2 * X_WIDTH), lambda i: (0, 0)),
            pl.BlockSpec((None, X_WIDTH, D_MODEL), lambda i: (layer, 0, 0)),
        ],
        out_specs=pl.BlockSpec((bm, D_MODEL), lambda i: (i, 0)),
        out_shape=jax.ShapeDtypeStruct((m, D_MODEL), F32),
        scratch_shapes=[pltpu.VMEM((D_MODEL, X_WIDTH), BF16), pltpu.VMEM((X_WIDTH, D_MODEL), BF16)],
        compiler_params=_params(("arbitrary",), 48),
        name="xattn",
    )(x, gain, w_q, kv, w_o)


def _final_norm_body(x_ref, g_ref, o_ref):
    o_ref[...] = _rms(x_ref[...], g_ref[...])


def _final_norm(x, gain):
    m = x.shape[0]
    bm = 512
    return pl.pallas_call(
        _final_norm_body,
        grid=(m // bm,),
        in_specs=[pl.BlockSpec((bm, D_MODEL), lambda i: (i, 0)), pl.BlockSpec((1, D_MODEL), lambda i: (0, 0))],
        out_specs=pl.BlockSpec((bm, D_MODEL), lambda i: (i, 0)),
        out_shape=jax.ShapeDtypeStruct((m, D_MODEL), F32),
        compiler_params=_params(("parallel",), 32),
        name="final_norm",
    )(x, gain)


DOWN_SLAB = 64
WIDE_SLAB = 32


def _ffn(x, gain, w_gu, w_down, layer, name, more_casts=()):
    casts = [(w_down, layer, DOWN_SLAB)] + list(more_casts)
    act, w_down_b, *copies = _ffn_up(x, gain.reshape(1, -1), w_gu, layer, casts)
    return _mm_res(act, w_down_b, x, 0.5, name), copies


def _pad_lanes(v):
    return jnp.pad(v.reshape(1, -1), ((0, 0), (0, LANES - v.shape[-1])))


def kernel(x, mem, positions, norm_ffn1, w_ffn1_gu, w_ffn1_down, norm_mix, w_in_even, gm_ln_g, gm_ln_b, gm_ws, gm_bs, conv_w, conv_b, dt_bias, a_log, d_skip, ssd_norm, w_out_even, w_qkv, b_qkv, sinks, w_o_odd, norm_xq, norm_mem, w_xq, w_xkv, w_xo, norm_ffn2, w_ffn2_gu, w_ffn2_down, final_norm):
    bsz, seq, d = x.shape
    assert (bsz, seq, d) == (1, SEQ, D_MODEL)
    xr = x.reshape(seq, d)
    memr = mem.reshape(N_MEM, d)
    inv_freq = ROPE_THETA ** (-jnp.arange(0, ROT_DIM, 2, dtype=F32) / ROT_DIM)
    cos_t, sin_t = _rope_table(positions.reshape(1, seq), inv_freq.reshape(ROT_HALF, 1))
    head_of_lane = jnp.arange(D_MODEL, dtype=jnp.int32) // HEAD_DIM
    e1 = (jnp.arange(LANES, dtype=jnp.int32)[:, None] == head_of_lane[None, :]).astype(BF16)
    e3 = jnp.concatenate([e1, e1, e1], axis=0)
    sel_row = jnp.arange((SSD_CONV - 1) * CHUNK, dtype=jnp.int32)[:, None]
    sel_col = jnp.arange(2 * CHUNK, dtype=jnp.int32)[None, :]
    shift = (sel_col == CHUNK + sel_row % CHUNK - (SSD_CONV - 1) + sel_row // CHUNK).astype(BF16)

    w_in_t = jnp.swapaxes(w_in_even, 1, 2)

    w_gu1 = w_ffn1_gu
    for i in range(DEPTH):
        j = i // 2
        xr, (w_gu2,) = _ffn(xr, norm_ffn1[i], w_gu1, w_ffn1_down, i, "ffn1_down",
                            [(w_ffn2_gu, i, WIDE_SLAB)])
        if i % 2 == 0:
            proj, dt_raw, w_out_b = _even_in(xr, norm_mix[i].reshape(1, -1), w_in_t, w_out_even, j)
            mix = _even_mix(
                proj, dt_raw, gm_ln_g[j].reshape(1, -1), gm_ln_b[j].reshape(1, -1), gm_ws[j], gm_bs[j].T,
                conv_w[j], conv_b[j].reshape(1, -1), _pad_lanes(dt_bias[j]), _pad_lanes(a_log[j]),
                jnp.repeat(d_skip[j], HEAD_DIM).reshape(1, -1), ssd_norm[j].reshape(1, -1), e3, shift)
            xr = _mm_res(mix, w_out_b, xr, 1.0, "even_out")
        else:
            qkv, w_o_b = _qkv(xr, norm_mix[i].reshape(1, -1), w_qkv_b, w_o_odd, j, b_qkv[j].reshape(1, -1),
                              cos_t, sin_t)
            att = _swa(qkv, sinks[j])
            xr = _mm_res(att, w_o_b, xr, 1.0, "odd_out")
        kv = _norm_mm(memr, norm_mem[i].reshape(1, -1), w_xkv, i, N_MEM, BF16, "mem_kv")
        xr = _xattn(xr, norm_xq[i].reshape(1, -1), w_xq, kv, w_xo, i)
        ahead = []
        if i + 1 < DEPTH:
            ahead.append((w_ffn1_gu, i + 1, WIDE_SLAB))
            if (i + 1) % 2 == 1:
                ahead.append((w_qkv, (i + 1) // 2, WIDE_SLAB))
        xr, copies = _ffn(xr, norm_ffn2[i], w_gu2, w_ffn2_down, i, "ffn2_down", ahead)
        if copies:
            w_gu1 = copies[0]
            w_qkv_b = copies[1] if len(copies) > 1 else None
    return _final_norm(xr, final_norm.reshape(1, -1)).reshape(bsz, seq, d)
```

```python
import functools

import jax
import jax.numpy as jnp
from jax import lax
from jax.experimental import pallas as pl
from jax.experimental.pallas import tpu as pltpu

F32 = jnp.float32
BF16 = jnp.bfloat16

D_MODEL = 2048
SEQ = 8192
DEPTH = 2
EPS = 1e-5
N_MEM = 256
D_FF = 5632
CHUNK = 128
GROUPS = 4
GDIM = D_MODEL // GROUPS
HEAD_DIM = 64
SSD_HEADS = 32
SSD_STATE = 128
SSD_CONV = 4
CONV_DIM = D_MODEL + 2 * GROUPS * SSD_STATE
EVEN_MAIN = 2 * D_MODEL + D_MODEL + CONV_DIM
ATT_HEADS = 32
ATT_REP = ATT_HEADS // GROUPS
ATT_SCALE = HEAD_DIM ** -0.5
ROT_DIM = HEAD_DIM // 4
ROT_HALF = ROT_DIM // 2
ROPE_THETA = 500000.0
KV_WIDTH = GROUPS * HEAD_DIM
ODD_IN = (ATT_HEADS + 2 * GROUPS) * HEAD_DIM
X_HEADS = 4
X_HEAD_DIM = 128
X_WIDTH = X_HEADS * X_HEAD_DIM
X_SCALE = X_HEAD_DIM ** -0.5

LANES = 128
SUBLANES = 8
CONV_TAIL = 16
BM = 1024
BN = 512
MIB = 1024 * 1024


def _params(semantics, vmem_mib):
    return pltpu.CompilerParams(dimension_semantics=semantics, vmem_limit_bytes=vmem_mib * MIB)


def _rms(x, g):
    ms = jnp.mean(x * x, axis=-1, keepdims=True)
    return x * lax.rsqrt(ms + EPS) * g


def _silu(x):
    return x * jax.nn.sigmoid(x)


def _gelu(x):
    return 0.5 * x * (1.0 + lax.erf(x * (2.0 ** -0.5)))


def _dot(a, b):
    return jnp.dot(a, b, preferred_element_type=F32)


def _dot_nt(a, b):
    return lax.dot_general(a, b, (((1,), (1,)), ((), ())), preferred_element_type=F32)


def _dot_tn(a, b):
    return lax.dot_general(a, b, (((0,), (0,)), ((), ())), preferred_element_type=F32)


def _snake(i, j, nj):
    return jnp.where(i % 2 == 0, j, nj - 1 - j)


def _cast_specs(w, layer, rows, steps_per_row_block):
    _, r, c = w.shape
    n_slabs = pl.cdiv(r, rows)
    slab = lambda i, j: jnp.minimum(i * steps_per_row_block + j, n_slabs - 1)
    return (pl.BlockSpec((None, rows, c), lambda i, j: (layer, slab(i, j), 0)),
            pl.BlockSpec((rows, c), lambda i, j: (slab(i, j), 0)),
            jax.ShapeDtypeStruct((r, c), BF16))


def _ffn_up_body(x_ref, g_ref, wg_ref, wu_ref, *refs, n_cast):
    cast_in, o_ref, cast_out = refs[:n_cast], refs[n_cast], refs[n_cast + 1:2 * n_cast + 1]
    h_ref, w_ref = refs[2 * n_cast + 1:]

    @pl.when(pl.program_id(1) == 0)
    def _():
        h_ref[...] = _rms(x_ref[...], g_ref[...]).astype(BF16)

    for src, dst in zip(cast_in, cast_out):
        dst[...] = src[...].astype(BF16)
    w_ref[:, :BN] = wg_ref[...].astype(BF16)
    w_ref[:, BN:] = wu_ref[...].astype(BF16)
    gu = _dot(h_ref[...], w_ref[...])
    o_ref[...] = (_silu(gu[:, :BN]) * gu[:, BN:]).astype(BF16)


def _ffn_up(x, gain, w_gu, layer, casts):
    m = x.shape[0]
    nj = D_FF // BN
    if w_gu.ndim == 3:
        w_spec = lambda off: pl.BlockSpec((None, D_MODEL, BN), lambda i, j: (layer, 0, _snake(i, j, nj) + off))
    else:
        w_spec = lambda off: pl.BlockSpec((D_MODEL, BN), lambda i, j: (0, _snake(i, j, nj) + off))
    cast_specs = [_cast_specs(w, l, rows, nj) for w, l, rows in casts]
    return pl.pallas_call(
        functools.partial(_ffn_up_body, n_cast=len(casts)),
        grid=(m // BM, nj),
        in_specs=[
            pl.BlockSpec((BM, D_MODEL), lambda i, j: (i, 0)),
            pl.BlockSpec((1, D_MODEL), lambda i, j: (0, 0)),
            w_spec(0), w_spec(nj),
        ] + [s[0] for s in cast_specs],
        out_specs=[pl.BlockSpec((BM, BN), lambda i, j: (i, _snake(i, j, nj)))] + [s[1] for s in cast_specs],
        out_shape=[jax.ShapeDtypeStruct((m, D_FF), BF16)] + [s[2] for s in cast_specs],
        scratch_shapes=[pltpu.VMEM((BM, D_MODEL), BF16), pltpu.VMEM((D_MODEL, 2 * BN), BF16)],
        compiler_params=_params(("arbitrary", "arbitrary"), 56),
        name="ffn_up",
    )(x, gain, w_gu, w_gu, *[w for w, _, _ in casts])


def _mm_res_body(a_ref, w_ref, r_ref, o_ref, *, scale):
    o_ref[...] = r_ref[...] + scale * _dot(a_ref[...], w_ref[...])


def _mm_res(a, w, res, scale, name):
    m, k = a.shape
    n = w.shape[1]
    bn = BN if k > 2 * D_MODEL else 2 * BN
    nj = n // bn
    return pl.pallas_call(
        functools.partial(_mm_res_body, scale=scale),
        grid=(m // BM, nj),
        in_specs=[
            pl.BlockSpec((BM, k), lambda i, j: (i, 0)),
            pl.BlockSpec((k, bn), lambda i, j: (0, _snake(i, j, nj))),
            pl.BlockSpec((BM, bn), lambda i, j: (i, _snake(i, j, nj))),
        ],
        out_specs=pl.BlockSpec((BM, bn), lambda i, j: (i, _snake(i, j, nj))),
        out_shape=jax.ShapeDtypeStruct((m, n), F32),
        compiler_params=_params(("arbitrary", "arbitrary"), 56),
        name=name,
    )(a, w, res)


def _norm_mm_body(x_ref, g_ref, w_ref, o_ref, h_ref):
    @pl.when(pl.program_id(1) == 0)
    def _():
        h_ref[...] = _rms(x_ref[...], g_ref[...]).astype(BF16)

    o_ref[...] = _dot(h_ref[...], w_ref[...].astype(BF16)).astype(o_ref.dtype)


def _norm_mm(x, gain, w, layer, bm, out_dtype, name):
    m, k = x.shape
    n = w.shape[2]
    return pl.pallas_call(
        _norm_mm_body,
        grid=(m // bm, n // BN),
        in_specs=[
            pl.BlockSpec((bm, k), lambda i, j: (i, 0)),
            pl.BlockSpec((1, k), lambda i, j: (0, 0)),
            pl.BlockSpec((None, k, BN), lambda i, j: (layer, 0, j)),
        ],
        out_specs=pl.BlockSpec((bm, BN), lambda i, j: (i, j)),
        out_shape=jax.ShapeDtypeStruct((m, n), out_dtype),
        scratch_shapes=[pltpu.VMEM((bm, k), BF16)],
        compiler_params=_params(("parallel", "arbitrary"), 40),
        name=name,
    )(x, gain, w)


EVEN_BN = 2 * BN


def _even_uv_body(x_ref, g_ref, w_ref, o_ref, h_ref):
    @pl.when(pl.program_id(1) == 0)
    def _():
        h_ref[...] = _rms(x_ref[...], g_ref[...]).astype(BF16)

    o_ref[...] = _gelu(_dot_nt(h_ref[...], w_ref[...].astype(BF16))).astype(BF16)


def _even_rest_body(x_ref, g_ref, w_ref, wdt_ref, wo_ref, o_ref, dt_ref, wo_out_ref, h_ref):
    @pl.when(pl.program_id(1) == 0)
    def _():
        h = _rms(x_ref[...], g_ref[...]).astype(BF16)
        h_ref[...] = h
        row = lax.broadcasted_iota(jnp.int32, (LANES, 1), 0)
        dt_ref[...] = _dot_nt(h, jnp.where(row < SSD_HEADS, wdt_ref[...], 0.0).astype(BF16))

    wo_out_ref[...] = wo_ref[...].astype(BF16)
    o_ref[...] = _dot_nt(h_ref[...], w_ref[...].astype(BF16)).astype(BF16)


def _even_in(x, gain, w_in_t, w_out, layer):
    m = x.shape[0]
    bn = EVEN_BN
    x_spec = pl.BlockSpec((BM, D_MODEL), lambda i, j: (i, 0))
    g_spec = pl.BlockSpec((1, D_MODEL), lambda i, j: (0, 0))
    h_scratch = [pltpu.VMEM((BM, D_MODEL), BF16)]

    nj = 2 * D_MODEL // bn
    uv = pl.pallas_call(
        _even_uv_body,
        grid=(m // BM, nj),
        in_specs=[x_spec, g_spec,
                  pl.BlockSpec((None, bn, D_MODEL), lambda i, j: (layer, _snake(i, j, nj), 0))],
        out_specs=pl.BlockSpec((BM, bn), lambda i, j: (i, _snake(i, j, nj))),
        out_shape=jax.ShapeDtypeStruct((m, 2 * D_MODEL), BF16),
        scratch_shapes=h_scratch,
        compiler_params=_params(("arbitrary", "arbitrary"), 56),
        name="even_in_uv",
    )(x, gain, w_in_t)

    n_xbc, n_z = CONV_DIM // bn, D_MODEL // bn
    nr = n_xbc + n_z
    z_first, xbc_first = 2 * D_MODEL // bn, 3 * D_MODEL // bn

    def w_block(i, j):
        jc = _snake(i, j, nr)
        return jnp.where(jc < n_xbc, xbc_first + jc, z_first + jc - n_xbc)

    wo_in_spec, wo_out_spec, wo_shape = _cast_specs(w_out, layer, 128, nr)
    rest, dt_raw, w_out_b = pl.pallas_call(
        _even_rest_body,
        grid=(m // BM, nr),
        in_specs=[x_spec, g_spec,
                  pl.BlockSpec((None, bn, D_MODEL), lambda i, j: (layer, w_block(i, j), 0)),
                  pl.BlockSpec((None, LANES, D_MODEL), lambda i, j: (layer, EVEN_MAIN // LANES, 0)),
                  wo_in_spec],
        out_specs=[pl.BlockSpec((BM, bn), lambda i, j: (i, _snake(i, j, nr))),
                   pl.BlockSpec((BM, LANES), lambda i, j: (i, 0)),
                   wo_out_spec],
        out_shape=[jax.ShapeDtypeStruct((m, CONV_DIM + D_MODEL), BF16),
                   jax.ShapeDtypeStruct((m, LANES), F32),
                   wo_shape],
        scratch_shapes=h_scratch,
        compiler_params=_params(("arbitrary", "arbitrary"), 56),
        name="even_in_rest",
    )(x, gain, w_in_t, w_in_t, w_out)
    return uv, rest, dt_raw, w_out_b


def _rope_table_body(pos_ref, invf_ref, cos_ref, sin_ref):
    ang = pos_ref[...].astype(F32) * invf_ref[...]
    cos_ref[...] = jnp.cos(ang)
    sin_ref[...] = jnp.sin(ang)


def _rope_table(pos_row, invf_col):
    shape = jax.ShapeDtypeStruct((ROT_HALF, pos_row.shape[1]), F32)
    return pl.pallas_call(_rope_table_body, out_shape=[shape, shape], name="rope_table")(pos_row, invf_col)


def _qkv_body(x_ref, g_ref, w_ref, b_ref, cost_ref, sint_ref, wo_ref, o_ref, wo_out_ref,
              h_ref, cos_ref, sn_ref, sp_ref, *, nj):
    j = pl.program_id(1)

    @pl.when(j == 0)
    def _():
        h_ref[...] = _rms(x_ref[...], g_ref[...]).astype(BF16)
        reps = LANES // ROT_HALF
        cos = jnp.concatenate([cost_ref[...]] * reps, axis=0).T
        sin = jnp.concatenate([sint_ref[...]] * reps, axis=0).T
        lane = lax.broadcasted_iota(jnp.int32, (1, LANES), 1) % HEAD_DIM
        first = lane < ROT_HALF
        second = (lane >= ROT_HALF) & (lane < ROT_DIM)
        cos_ref[...] = jnp.where(first | second, cos, 1.0)
        sn_ref[...] = jnp.where(first, -sin, 0.0)
        sp_ref[...] = jnp.where(second, sin, 0.0)

    wo_out_ref[...] = wo_ref[...].astype(BF16)
    acc = _dot(h_ref[...], w_ref[...]) + b_ref[...]

    def rope(a):
        return (a * cos_ref[...] + pltpu.roll(a, LANES - ROT_HALF, 1) * sn_ref[...]
                + pltpu.roll(a, ROT_HALF, 1) * sp_ref[...])

    col_block = _snake(pl.program_id(0), j, nj)
    tiles = o_ref.shape[1] // LANES

    def kind(tile):
        return "q" if tile < ATT_HEADS * HEAD_DIM // LANES else "k" if tile < (ODD_IN - KV_WIDTH) // LANES else "v"

    for t in range(tiles):
        a = acc[:, t * LANES:(t + 1) * LANES]
        kinds = [kind(jb * tiles + t) for jb in range(nj)]
        roped = rope(a) if set(kinds) != {"v"} else None
        by_kind = {"q": lambda: roped * ATT_SCALE, "k": lambda: roped, "v": lambda: a}
        val = by_kind[kinds[-1]]()
        for jb in range(nj - 2, -1, -1):
            if kinds[jb] != kinds[jb + 1]:
                val = jnp.where(col_block <= jb, by_kind[kinds[jb]](), val)
        o_ref[:, t * LANES:(t + 1) * LANES] = val.astype(BF16)


def _qkv(x, gain, w_b, w_o, layer, b, cos_t, sin_t):
    m = x.shape[0]
    nj = 2
    bn = ODD_IN // nj
    wo_in_spec, wo_out_spec, wo_shape = _cast_specs(w_o, layer, 128, nj)
    return pl.pallas_call(
        functools.partial(_qkv_body, nj=nj),
        grid=(m // BM, nj),
        in_specs=[
            pl.BlockSpec((BM, D_MODEL), lambda i, j: (i, 0)),
            pl.BlockSpec((1, D_MODEL), lambda i, j: (0, 0)),
            pl.BlockSpec((D_MODEL, bn), lambda i, j: (0, _snake(i, j, nj))),
            pl.BlockSpec((1, bn), lambda i, j: (0, _snake(i, j, nj))),
            pl.BlockSpec((ROT_HALF, BM), lambda i, j: (0, i)),
            pl.BlockSpec((ROT_HALF, BM), lambda i, j: (0, i)),
            wo_in_spec,
        ],
        out_specs=[pl.BlockSpec((BM, bn), lambda i, j: (i, _snake(i, j, nj))), wo_out_spec],
        out_shape=[jax.ShapeDtypeStruct((m, ODD_IN), BF16), wo_shape],
        scratch_shapes=[
            pltpu.VMEM((BM, D_MODEL), BF16),
            pltpu.VMEM((BM, LANES), F32),
            pltpu.VMEM((BM, LANES), F32),
            pltpu.VMEM((BM, LANES), F32),
        ],
        compiler_params=_params(("arbitrary", "arbitrary"), 56),
        name="qkv_rope",
    )(x, gain, w_b, b, cos_t, sin_t, w_o)


def _split3(x):
    hi = x.astype(BF16)
    r1 = x - hi.astype(F32)
    mid = r1.astype(BF16)
    lo = (r1 - mid.astype(F32)).astype(BF16)
    return hi, mid, lo


def _even_mix_body(u_ref, v_ref, z0_ref, z1_ref, xbc_ref, dt_ref, lng_ref, lnb_ref, ws_ref, bs_ref, cw_ref, cb_ref,
                   dtb_ref, alog_ref, dskip_ref, snorm_ref, e3_ref, shift_ref, o_ref, state_ref, xx_ref):
    c = pl.program_id(0)
    q = CHUNK

    @pl.when(c == 0)
    def _():
        state_ref[...] = jnp.zeros_like(state_ref)
        xx_ref[pl.ds(0, q), :] = jnp.zeros((q, CONV_DIM), BF16)

    row = lax.broadcasted_iota(jnp.int32, (q, q), 0)
    col = lax.broadcasted_iota(jnp.int32, (q, q), 1)
    causal = col <= row

    for g in range(GROUPS):
        seg = slice(g * GDIM, (g + 1) * GDIM)
        vg = v_ref[:, seg].astype(F32)
        mu = jnp.mean(vg, axis=-1, keepdims=True)
        d = vg - mu
        var = jnp.mean(d * d, axis=-1, keepdims=True)
        vn = d * lax.rsqrt(var + EPS) * lng_ref[:, seg] + lnb_ref[:, seg]
        w = jnp.where(causal, ws_ref[g], 0.0).astype(BF16)
        s = _dot(w, vn.astype(BF16)) + bs_ref[:, g:g + 1]
        o_ref[:, seg] = (u_ref[:, seg].astype(F32) * s).astype(BF16)

    x_cur = xbc_ref[...]
    xx_ref[pl.ds(q, q), :] = x_cur
    shifted = _dot(shift_ref[...], xx_ref[...])
    conv = cb_ref[...] + cw_ref[SSD_CONV - 1:SSD_CONV, :] * x_cur.astype(F32)
    for k in range(SSD_CONV - 1):
        conv = conv + cw_ref[k:k + 1, :] * shifted[k * q:(k + 1) * q]
    xx_ref[pl.ds(q - CONV_TAIL, CONV_TAIL), :] = x_cur[q - CONV_TAIL:, :]
    xbc = _silu(conv)
    xs = xbc[:, :D_MODEL]

    dt = jax.nn.softplus(dt_ref[...] + dtb_ref[...])
    a = dt * (-jnp.exp(alog_ref[...]))
    tri = jnp.where(causal, 1.0, 0.0).astype(BF16)
    a_hi, a_mid, a_lo = _split3(a)
    acs = _dot(tri, a_hi) + _dot(tri, a_mid) + _dot(tri, a_lo)
    acs_t = acs.T
    both = jnp.concatenate([dt, acs], axis=0)
    b_hi, b_mid, b_lo = _split3(both)
    both_e = _dot(jnp.concatenate([b_hi, b_mid, b_lo], axis=1), e3_ref[...])
    dt_e = both_e[:q]
    acs_e = both_e[q:]
    last_e = acs_e[q - 1:q, :]
    xdt = xs * dt_e
    xdec = (xdt * jnp.exp(last_e - acs_e)).astype(BF16)
    xdt_b = xdt.astype(BF16)
    grow_e = jnp.exp(acs_e)
    chunk_decay = jnp.exp(last_e)

    lane = lax.broadcasted_iota(jnp.int32, (q, LANES), 1)
    lo_half = lane < HEAD_DIM
    gate = _silu(jnp.concatenate([z0_ref[...], z1_ref[...]], axis=1).astype(F32))

    for g in range(GROUPS):
        seg = slice(g * GDIM, (g + 1) * GDIM)
        b_g = xbc[:, D_MODEL + g * SSD_STATE:D_MODEL + (g + 1) * SSD_STATE].astype(BF16)
        c_g = xbc[:, D_MODEL + GROUPS * SSD_STATE + g * SSD_STATE:
                  D_MODEL + GROUPS * SSD_STATE + (g + 1) * SSD_STATE].astype(BF16)
        cb_causal = jnp.where(causal, _dot_nt(c_g, b_g), 0.0)
        y_off = _dot(c_g, state_ref[g].astype(BF16)) * grow_e[:, seg]
        pieces = []
        for p in range(GDIM // LANES):
            mats = []
            for hh in range(2):
                h = g * (GDIM // HEAD_DIM) + 2 * p + hh
                seg_ij = jnp.minimum(acs[:, h:h + 1] - acs_t[h:h + 1, :], 0.0)
                mats.append((cb_causal * jnp.exp(seg_ij)).astype(BF16))
            x2 = xdt_b[:, g * GDIM + p * LANES:g * GDIM + (p + 1) * LANES]
            zero = jnp.zeros_like(x2)
            rhs = jnp.concatenate([jnp.where(lo_half, x2, zero), jnp.where(lo_half, zero, x2)], axis=0)
            pieces.append(_dot(jnp.concatenate(mats, axis=1), rhs))
        y_diag = jnp.concatenate(pieces, axis=1)
        new_states = _dot_tn(b_g, xdec[:, seg])
        state_ref[g] = state_ref[g] * chunk_decay[:, seg] + new_states
        y = y_diag + y_off + xs[:, seg] * dskip_ref[:, seg]
        y = y * gate[:, seg]
        y = y * lax.rsqrt(jnp.mean(y * y, axis=-1, keepdims=True) + EPS)
        o_ref[:, D_MODEL + g * GDIM:D_MODEL + (g + 1) * GDIM] = (y * snorm_ref[:, seg]).astype(BF16)


def _even_mix(uv, rest, dt_raw, ln_g, ln_b, ws, bs_t, conv_w, conv_b, dt_bias, a_log, d_skip_e, ssd_norm, e3,
              shift):
    m = uv.shape[0]
    full = lambda shape: pl.BlockSpec(shape, lambda c: (0,) * len(shape))
    z_block = CONV_DIM // EVEN_BN
    return pl.pallas_call(
        _even_mix_body,
        grid=(m // CHUNK,),
        in_specs=[
            pl.BlockSpec((CHUNK, D_MODEL), lambda c: (c, 0)),
            pl.BlockSpec((CHUNK, D_MODEL), lambda c: (c, 1)),
            pl.BlockSpec((CHUNK, EVEN_BN), lambda c: (c, z_block)),
            pl.BlockSpec((CHUNK, EVEN_BN), lambda c: (c, z_block + 1)),
            pl.BlockSpec((CHUNK, CONV_DIM), lambda c: (c, 0)),
            pl.BlockSpec((CHUNK, LANES), lambda c: (c, 0)),
            full((1, D_MODEL)), full((1, D_MODEL)),
            full((GROUPS, CHUNK, CHUNK)), full((CHUNK, GROUPS)),
            full((SSD_CONV, CONV_DIM)), full((1, CONV_DIM)),
            full((1, LANES)), full((1, LANES)),
            full((1, D_MODEL)), full((1, D_MODEL)),
            full((3 * LANES, D_MODEL)),
            full(((SSD_CONV - 1) * CHUNK, 2 * CHUNK)),
        ],
        out_specs=pl.BlockSpec((CHUNK, 2 * D_MODEL), lambda c: (c, 0)),
        out_shape=jax.ShapeDtypeStruct((m, 2 * D_MODEL), BF16),
        scratch_shapes=[
            pltpu.VMEM((GROUPS, SSD_STATE, GDIM), F32),
            pltpu.VMEM((2 * CHUNK, CONV_DIM), BF16),
        ],
        compiler_params=_params(("arbitrary",), 48),
        name="even_mix",
    )(uv, uv, rest, rest, rest, dt_raw, ln_g, ln_b, ws, bs_t, conv_w, conv_b, dt_bias, a_log, d_skip_e, ssd_norm,
      e3, shift)


def _swa_body(sink_ref, q_ref, kv_ref, kvp_ref, o_ref):
    n = pl.program_id(0)
    w = CHUNK
    row = lax.broadcasted_iota(jnp.int32, (w, w), 0)
    col = lax.broadcasted_iota(jnp.int32, (w, w), 1)
    own = col <= row
    lo_half = lax.broadcasted_iota(jnp.int32, (w, LANES), 1) < HEAD_DIM
    prev_bias = jnp.where(n > 0, 0.0, -jnp.inf)

    def head_tiles(ref, base, k):
        t = ref[:, base + (k // 2) * LANES:base + (k // 2 + 1) * LANES].astype(F32)
        r = pltpu.roll(t, HEAD_DIM, 1)
        return (t, r) if k % 2 == 0 else (r, t)

    for k in range(GROUPS):
        k_lo, k_hi = head_tiles(kv_ref, 0, k)
        kp_lo, kp_hi = head_tiles(kvp_ref, 0, k)
        v_lo, v_hi = head_tiles(kv_ref, KV_WIDTH, k)
        vp_lo, vp_hi = head_tiles(kvp_ref, KV_WIDTH, k)
        kk = jnp.where(lo_half, k_lo, k_hi).astype(BF16)
        kkp = jnp.where(lo_half, kp_lo, kp_hi).astype(BF16)
        out = []
        for parity in range(2):
            if parity == 0:
                vv = jnp.concatenate([jnp.where(lo_half, v_lo, 1.0), jnp.where(lo_half, vp_lo, 1.0)], axis=0)
            else:
                vv = jnp.concatenate([jnp.where(lo_half, 1.0, v_hi), jnp.where(lo_half, 1.0, vp_hi)], axis=0)
            lhs = []
            for p in range(ATT_REP // 2):
                q2 = q_ref[:, k * GDIM + p * LANES:k * GDIM + (p + 1) * LANES]
                zero = jnp.zeros_like(q2)
                lhs.append(jnp.where(lo_half, q2, zero) if parity == 0 else jnp.where(lo_half, zero, q2))
            lhs = jnp.concatenate(lhs, axis=0)
            s_own = _dot_nt(lhs, kk)
            s_prev = _dot_nt(lhs, kkp)
            probs, esink = [], []
            for p in range(ATT_REP // 2):
                sink = sink_ref[k * ATT_REP + 2 * p + parity]
                s = jnp.where(own, s_own[p * w:(p + 1) * w], s_prev[p * w:(p + 1) * w] + prev_bias)
                mx = jnp.maximum(jnp.max(s, axis=-1, keepdims=True), sink)
                e = jnp.exp(s - mx)
                probs.append(jnp.concatenate([jnp.where(own, e, 0.0).astype(BF16),
                                              jnp.where(own, 0.0, e).astype(BF16)], axis=1))
                esink.append(jnp.exp(sink - mx))
            o = _dot(jnp.concatenate(probs, axis=0), vv.astype(BF16))
            out.append((o, esink))
        for p in range(ATT_REP // 2):
            o_even = out[0][0][p * w:(p + 1) * w]
            o_odd = out[1][0][p * w:(p + 1) * w]
            num = jnp.where(lo_half, o_even, o_odd)
            den = pltpu.roll(jnp.where(lo_half, o_odd, o_even), HEAD_DIM, 1)
            den = den + jnp.where(lo_half, out[0][1][p], out[1][1][p])
            o_ref[:, k * GDIM + p * LANES:k * GDIM + (p + 1) * LANES] = (num / den).astype(BF16)


def _swa(qkv, sinks):
    m = qkv.shape[0]
    kv_block = ATT_HEADS * HEAD_DIM // (2 * KV_WIDTH)
    return pl.pallas_call(
        _swa_body,
        grid=(m // CHUNK,),
        in_specs=[
            pl.BlockSpec(memory_space=pltpu.SMEM),
            pl.BlockSpec((CHUNK, D_MODEL), lambda n: (n, 0)),
            pl.BlockSpec((CHUNK, 2 * KV_WIDTH), lambda n: (n, kv_block)),
            pl.BlockSpec((CHUNK, 2 * KV_WIDTH), lambda n: (jnp.maximum(n - 1, 0), kv_block)),
        ],
        out_specs=pl.BlockSpec((CHUNK, D_MODEL), lambda n: (n, 0)),
        out_shape=jax.ShapeDtypeStruct((m, D_MODEL), BF16),
        compiler_params=_params(("parallel",), 32),
        name="swa",
    )(sinks, qkv, qkv, qkv)


def _xattn_body(x_ref, g_ref, wq_ref, kv_ref, wo_ref, o_ref, wq_b, wo_b):
    @pl.when(pl.program_id(0) == 0)
    def _():
        wq_b[...] = wq_ref[...].astype(BF16)
        wo_b[...] = wo_ref[...].astype(BF16)

    x = x_ref[...]
    h = _rms(x, g_ref[...]).astype(BF16)
    q = _dot(h, wq_b[...]).astype(BF16)
    outs = []
    for hd in range(X_HEADS):
        seg = slice(hd * X_HEAD_DIM, (hd + 1) * X_HEAD_DIM)
        k = kv_ref[:, seg]
        v = kv_ref[:, X_WIDTH + hd * X_HEAD_DIM:X_WIDTH + (hd + 1) * X_HEAD_DIM]
        s = _dot_nt(q[:, seg], k) * X_SCALE
        e = jnp.exp(s - jnp.max(s, axis=-1, keepdims=True))
        o = _dot(e.astype(BF16), v) * (1.0 / jnp.sum(e, axis=-1, keepdims=True))
        outs.append(o.astype(BF16))
    o_ref[...] = x + _dot(jnp.concatenate(outs, axis=1), wo_b[...])


def _xattn(x, gain, w_q, kv, w_o, layer):
    m = x.shape[0]
    bm = 512
    return pl.pallas_call(
        _xattn_body,
        grid=(m // bm,),
        in_specs=[
            pl.BlockSpec((bm, D_MODEL), lambda i: (i, 0)),
            pl.BlockSpec((1, D_MODEL), lambda i: (0, 0)),
            pl.BlockSpec((None, D_MODEL, X_WIDTH), lambda i: (layer, 0, 0)),
            pl.BlockSpec((N_MEM, 2 * X_WIDTH), lambda i: (0, 0)),
            pl.BlockSpec((None, X_WIDTH, D_MODEL), lambda i: (layer, 0, 0)),
        ],
        out_specs=pl.BlockSpec((bm, D_MODEL), lambda i: (i, 0)),
        out_shape=jax.ShapeDtypeStruct((m, D_MODEL), F32),
        scratch_shapes=[pltpu.VMEM((D_MODEL, X_WIDTH), BF16), pltpu.VMEM((X_WIDTH, D_MODEL), BF16)],
        compiler_params=_params(("arbitrary",), 48),
        name="xattn",
    )(x, gain, w_q, kv, w_o)


def _final_norm_body(x_ref, g_ref, o_ref):
    o_ref[...] = _rms(x_ref[...], g_ref[...])


def _final_norm(x, gain):
    m = x.shape[0]
    bm = 512
    return pl.pallas_call(
        _final_norm_body,
        grid=(m // bm,),
        in_specs=[pl.BlockSpec((bm, D_MODEL), lambda i: (i, 0)), pl.BlockSpec((1, D_MODEL), lambda i: (0, 0))],
        out_specs=pl.BlockSpec((bm, D_MODEL), lambda i: (i, 0)),
        out_shape=jax.ShapeDtypeStruct((m, D_MODEL), F32),
        compiler_params=_params(("parallel",), 32),
        name="final_norm",
    )(x, gain)


DOWN_SLAB = 64
WIDE_SLAB = 32


def _ffn(x, gain, w_gu, w_down, layer, name, more_casts=()):
    casts = [(w_down, layer, DOWN_SLAB)] + list(more_casts)
    act, w_down_b, *copies = _ffn_up(x, gain.reshape(1, -1), w_gu, layer, casts)
    return _mm_res(act, w_down_b, x, 0.5, name), copies


def _pad_lanes(v):
    return jnp.pad(v.reshape(1, -1), ((0, 0), (0, LANES - v.shape[-1])))


def kernel(x, mem, positions, norm_ffn1, w_ffn1_gu, w_ffn1_down, norm_mix, w_in_even, gm_ln_g, gm_ln_b, gm_ws, gm_bs, conv_w, conv_b, dt_bias, a_log, d_skip, ssd_norm, w_out_even, w_qkv, b_qkv, sinks, w_o_odd, norm_xq, norm_mem, w_xq, w_xkv, w_xo, norm_ffn2, w_ffn2_gu, w_ffn2_down, final_norm):
    bsz, seq, d = x.shape
    assert (bsz, seq, d) == (1, SEQ, D_MODEL)
    xr = x.reshape(seq, d)
    memr = mem.reshape(N_MEM, d)
    inv_freq = ROPE_THETA ** (-jnp.arange(0, ROT_DIM, 2, dtype=F32) / ROT_DIM)
    cos_t, sin_t = _rope_table(positions.reshape(1, seq), inv_freq.reshape(ROT_HALF, 1))
    head_of_lane = jnp.arange(D_MODEL, dtype=jnp.int32) // HEAD_DIM
    e1 = (jnp.arange(LANES, dtype=jnp.int32)[:, None] == head_of_lane[None, :]).astype(BF16)
    e3 = jnp.concatenate([e1, e1, e1], axis=0)
    sel_row = jnp.arange((SSD_CONV - 1) * CHUNK, dtype=jnp.int32)[:, None]
    sel_col = jnp.arange(2 * CHUNK, dtype=jnp.int32)[None, :]
    shift = (sel_col == CHUNK + sel_row % CHUNK - (SSD_CONV - 1) + sel_row // CHUNK).astype(BF16)

    w_in_t = jnp.swapaxes(w_in_even, 1, 2)

    w_gu1 = w_ffn1_gu
    for i in range(DEPTH):
        j = i // 2
        xr, (w_gu2,) = _ffn(xr, norm_ffn1[i], w_gu1, w_ffn1_down, i, "ffn1_down",
                            [(w_ffn2_gu, i, WIDE_SLAB)])
        if i % 2 == 0:
            uv, rest, dt_raw, w_out_b = _even_in(xr, norm_mix[i].reshape(1, -1), w_in_t, w_out_even, j)
            mix = _even_mix(
                uv, rest, dt_raw, gm_ln_g[j].reshape(1, -1), gm_ln_b[j].reshape(1, -1), gm_ws[j], gm_bs[j].T,
                conv_w[j], conv_b[j].reshape(1, -1), _pad_lanes(dt_bias[j]), _pad_lanes(a_log[j]),
                jnp.repeat(d_skip[j], HEAD_DIM).reshape(1, -1), ssd_norm[j].reshape(1, -1), e3, shift)
            xr = _mm_res(mix, w_out_b, xr, 1.0, "even_out")
        else:
            qkv, w_o_b = _qkv(xr, norm_mix[i].reshape(1, -1), w_qkv_b, w_o_odd, j, b_qkv[j].reshape(1, -1),
                              cos_t, sin_t)
            att = _swa(qkv, sinks[j])
            xr = _mm_res(att, w_o_b, xr, 1.0, "odd_out")
        kv = _norm_mm(memr, norm_mem[i].reshape(1, -1), w_xkv, i, N_MEM, BF16, "mem_kv")
        xr = _xattn(xr, norm_xq[i].reshape(1, -1), w_xq, kv, w_xo, i)
        ahead = []
        if i + 1 < DEPTH:
            ahead.append((w_ffn1_gu, i + 1, WIDE_SLAB))
            if (i + 1) % 2 == 1:
                ahead.append((w_qkv, (i + 1) // 2, WIDE_SLAB))
        xr, copies = _ffn(xr, norm_ffn2[i], w_gu2, w_ffn2_down, i, "ffn2_down", ahead)
        if copies:
            w_gu1 = copies[0]
            w_qkv_b = copies[1] if len(copies) > 1 else None
    return _final_norm(xr, final_norm.reshape(1, -1)).reshape(bsz, seq, d)
```

```python
import functools

import jax
import jax.numpy as jnp
from jax import lax
from jax.experimental import pallas as pl
from jax.experimental.pallas import tpu as pltpu

F32 = jnp.float32
BF16 = jnp.bfloat16

D_MODEL = 2048
SEQ = 8192
DEPTH = 2
EPS = 1e-5
N_MEM = 256
D_FF = 5632
CHUNK = 128
GROUPS = 4
GDIM = D_MODEL // GROUPS
HEAD_DIM = 64
SSD_HEADS = 32
SSD_STATE = 128
SSD_CONV = 4
CONV_DIM = D_MODEL + 2 * GROUPS * SSD_STATE
EVEN_MAIN = 2 * D_MODEL + D_MODEL + CONV_DIM
ATT_HEADS = 32
ATT_REP = ATT_HEADS // GROUPS
ATT_SCALE = HEAD_DIM ** -0.5
ROT_DIM = HEAD_DIM // 4
ROT_HALF = ROT_DIM // 2
ROPE_THETA = 500000.0
KV_WIDTH = GROUPS * HEAD_DIM
ODD_IN = (ATT_HEADS + 2 * GROUPS) * HEAD_DIM
X_HEADS = 4
X_HEAD_DIM = 128
X_WIDTH = X_HEADS * X_HEAD_DIM
X_SCALE = X_HEAD_DIM ** -0.5

LANES = 128
SUBLANES = 8
CONV_TAIL = 16
BM = 1024
BN = 512
MIB = 1024 * 1024


def _params(semantics, vmem_mib):
    return pltpu.CompilerParams(dimension_semantics=semantics, vmem_limit_bytes=vmem_mib * MIB)


def _rms(x, g):
    ms = jnp.mean(x * x, axis=-1, keepdims=True)
    return x * lax.rsqrt(ms + EPS) * g


def _silu(x):
    return x * jax.nn.sigmoid(x)


def _gelu(x):
    return 0.5 * x * (1.0 + lax.erf(x * (2.0 ** -0.5)))


def _dot(a, b):
    return jnp.dot(a, b, preferred_element_type=F32)


def _dot_nt(a, b):
    return lax.dot_general(a, b, (((1,), (1,)), ((), ())), preferred_element_type=F32)


def _dot_tn(a, b):
    return lax.dot_general(a, b, (((0,), (0,)), ((), ())), preferred_element_type=F32)


def _snake(i, j, nj):
    return jnp.where(i % 2 == 0, j, nj - 1 - j)


def _cast_specs(w, layer, rows, steps_per_row_block):
    _, r, c = w.shape
    n_slabs = pl.cdiv(r, rows)
    slab = lambda i, j: jnp.minimum(i * steps_per_row_block + j, n_slabs - 1)
    return (pl.BlockSpec((None, rows, c), lambda i, j: (layer, slab(i, j), 0)),
            pl.BlockSpec((rows, c), lambda i, j: (slab(i, j), 0)),
            jax.ShapeDtypeStruct((r, c), BF16))


def _ffn_up_body(x_ref, g_ref, wg_ref, wu_ref, *refs, n_cast):
    cast_in, o_ref, cast_out = refs[:n_cast], refs[n_cast], refs[n_cast + 1:2 * n_cast + 1]
    h_ref, w_ref = refs[2 * n_cast + 1:]

    @pl.when(pl.program_id(1) == 0)
    def _():
        h_ref[...] = _rms(x_ref[...], g_ref[...]).astype(BF16)

    for src, dst in zip(cast_in, cast_out):
        dst[...] = src[...].astype(BF16)
    w_ref[:, :BN] = wg_ref[...].astype(BF16)
    w_ref[:, BN:] = wu_ref[...].astype(BF16)
    gu = _dot(h_ref[...], w_ref[...])
    o_ref[...] = (_silu(gu[:, :BN]) * gu[:, BN:]).astype(BF16)


def _ffn_up(x, gain, w_gu, layer, casts):
    m = x.shape[0]
    nj = D_FF // BN
    if w_gu.ndim == 3:
        w_spec = lambda off: pl.BlockSpec((None, D_MODEL, BN), lambda i, j: (layer, 0, _snake(i, j, nj) + off))
    else:
        w_spec = lambda off: pl.BlockSpec((D_MODEL, BN), lambda i, j: (0, _snake(i, j, nj) + off))
    cast_specs = [_cast_specs(w, l, rows, nj) for w, l, rows in casts]
    return pl.pallas_call(
        functools.partial(_ffn_up_body, n_cast=len(casts)),
        grid=(m // BM, nj),
        in_specs=[
            pl.BlockSpec((BM, D_MODEL), lambda i, j: (i, 0)),
            pl.BlockSpec((1, D_MODEL), lambda i, j: (0, 0)),
            w_spec(0), w_spec(nj),
        ] + [s[0] for s in cast_specs],
        out_specs=[pl.BlockSpec((BM, BN), lambda i, j: (i, _snake(i, j, nj)))] + [s[1] for s in cast_specs],
        out_shape=[jax.ShapeDtypeStruct((m, D_FF), BF16)] + [s[2] for s in cast_specs],
        scratch_shapes=[pltpu.VMEM((BM, D_MODEL), BF16), pltpu.VMEM((D_MODEL, 2 * BN), BF16)],
        compiler_params=_params(("arbitrary", "arbitrary"), 56),
        name="ffn_up",
    )(x, gain, w_gu, w_gu, *[w for w, _, _ in casts])


def _mm_rows_body(a_ref, w_ref, r_ref, *refs, scale):
    y = r_ref[...] + scale * _dot(a_ref[...], w_ref[...])
    if len(refs) == 2:
        y = _rms(y, refs[0][...])
    refs[-1][...] = y


ROWS_BM = 512


def _mm_rows(a, w, res, scale, name, out_gain=None):
    m, k = a.shape
    n = w.shape[1]
    gains = [] if out_gain is None else [out_gain]
    return pl.pallas_call(
        functools.partial(_mm_rows_body, scale=scale),
        grid=(m // ROWS_BM,),
        in_specs=[
            pl.BlockSpec((ROWS_BM, k), lambda i: (i, 0)),
            pl.BlockSpec((k, n), lambda i: (0, 0), pipeline_mode=pl.Buffered(1)),
            pl.BlockSpec((ROWS_BM, n), lambda i: (i, 0)),
        ] + [pl.BlockSpec((1, n), lambda i: (0, 0))] * len(gains),
        out_specs=pl.BlockSpec((ROWS_BM, n), lambda i: (i, 0)),
        out_shape=jax.ShapeDtypeStruct((m, n), F32),
        compiler_params=_params(("arbitrary",), 60),
        name=name,
    )(a, w, res, *gains)


def _norm_mm_body(x_ref, g_ref, w_ref, o_ref, h_ref):
    @pl.when(pl.program_id(1) == 0)
    def _():
        h_ref[...] = _rms(x_ref[...], g_ref[...]).astype(BF16)

    o_ref[...] = _dot(h_ref[...], w_ref[...].astype(BF16)).astype(o_ref.dtype)


def _norm_mm(x, gain, w, layer, bm, out_dtype, name):
    m, k = x.shape
    n = w.shape[2]
    return pl.pallas_call(
        _norm_mm_body,
        grid=(m // bm, n // BN),
        in_specs=[
            pl.BlockSpec((bm, k), lambda i, j: (i, 0)),
            pl.BlockSpec((1, k), lambda i, j: (0, 0)),
            pl.BlockSpec((None, k, BN), lambda i, j: (layer, 0, j)),
        ],
        out_specs=pl.BlockSpec((bm, BN), lambda i, j: (i, j)),
        out_shape=jax.ShapeDtypeStruct((m, n), out_dtype),
        scratch_shapes=[pltpu.VMEM((bm, k), BF16)],
        compiler_params=_params(("parallel", "arbitrary"), 40),
        name=name,
    )(x, gain, w)


EVEN_BN = 2 * BN


def _even_uv_body(x_ref, g_ref, w_ref, o_ref, h_ref):
    @pl.when(pl.program_id(1) == 0)
    def _():
        h_ref[...] = _rms(x_ref[...], g_ref[...]).astype(BF16)

    o_ref[...] = _gelu(_dot_nt(h_ref[...], w_ref[...].astype(BF16))).astype(BF16)


def _even_rest_body(x_ref, g_ref, w_ref, wdt_ref, wo_ref, o_ref, dt_ref, wo_out_ref, h_ref):
    @pl.when(pl.program_id(1) == 0)
    def _():
        h = _rms(x_ref[...], g_ref[...]).astype(BF16)
        h_ref[...] = h
        row = lax.broadcasted_iota(jnp.int32, (LANES, 1), 0)
        dt_ref[...] = _dot_nt(h, jnp.where(row < SSD_HEADS, wdt_ref[...], 0.0).astype(BF16))

    wo_out_ref[...] = wo_ref[...].astype(BF16)
    o_ref[...] = _dot_nt(h_ref[...], w_ref[...].astype(BF16)).astype(BF16)


def _even_in(x, gain, w_in_t, w_out, layer):
    m = x.shape[0]
    bn = EVEN_BN
    x_spec = pl.BlockSpec((BM, D_MODEL), lambda i, j: (i, 0))
    g_spec = pl.BlockSpec((1, D_MODEL), lambda i, j: (0, 0))
    h_scratch = [pltpu.VMEM((BM, D_MODEL), BF16)]

    nj = 2 * D_MODEL // bn
    uv = pl.pallas_call(
        _even_uv_body,
        grid=(m // BM, nj),
        in_specs=[x_spec, g_spec,
                  pl.BlockSpec((None, bn, D_MODEL), lambda i, j: (layer, _snake(i, j, nj), 0))],
        out_specs=pl.BlockSpec((BM, bn), lambda i, j: (i, _snake(i, j, nj))),
        out_shape=jax.ShapeDtypeStruct((m, 2 * D_MODEL), BF16),
        scratch_shapes=h_scratch,
        compiler_params=_params(("arbitrary", "arbitrary"), 56),
        name="even_in_uv",
    )(x, gain, w_in_t)

    n_xbc, n_z = CONV_DIM // bn, D_MODEL // bn
    nr = n_xbc + n_z
    z_first, xbc_first = 2 * D_MODEL // bn, 3 * D_MODEL // bn

    def w_block(i, j):
        jc = _snake(i, j, nr)
        return jnp.where(jc < n_xbc, xbc_first + jc, z_first + jc - n_xbc)

    wo_in_spec, wo_out_spec, wo_shape = _cast_specs(w_out, layer, 128, nr)
    rest, dt_raw, w_out_b = pl.pallas_call(
        _even_rest_body,
        grid=(m // BM, nr),
        in_specs=[x_spec, g_spec,
                  pl.BlockSpec((None, bn, D_MODEL), lambda i, j: (layer, w_block(i, j), 0)),
                  pl.BlockSpec((None, LANES, D_MODEL), lambda i, j: (layer, EVEN_MAIN // LANES, 0)),
                  wo_in_spec],
        out_specs=[pl.BlockSpec((BM, bn), lambda i, j: (i, _snake(i, j, nr))),
                   pl.BlockSpec((BM, LANES), lambda i, j: (i, 0)),
                   wo_out_spec],
        out_shape=[jax.ShapeDtypeStruct((m, CONV_DIM + D_MODEL), BF16),
                   jax.ShapeDtypeStruct((m, LANES), F32),
                   wo_shape],
        scratch_shapes=h_scratch,
        compiler_params=_params(("arbitrary", "arbitrary"), 56),
        name="even_in_rest",
    )(x, gain, w_in_t, w_in_t, w_out)
    return uv, rest, dt_raw, w_out_b


def _rope_table_body(pos_ref, invf_ref, cos_ref, sin_ref):
    ang = pos_ref[...].astype(F32) * invf_ref[...]
    cos_ref[...] = jnp.cos(ang)
    sin_ref[...] = jnp.sin(ang)


def _rope_table(pos_row, invf_col):
    shape = jax.ShapeDtypeStruct((ROT_HALF, pos_row.shape[1]), F32)
    return pl.pallas_call(_rope_table_body, out_shape=[shape, shape], name="rope_table")(pos_row, invf_col)


def _qkv_body(x_ref, g_ref, w_ref, b_ref, cost_ref, sint_ref, wo_ref, o_ref, wo_out_ref,
              h_ref, cos_ref, sn_ref, sp_ref, *, nj):
    j = pl.program_id(1)

    @pl.when(j == 0)
    def _():
        h_ref[...] = _rms(x_ref[...], g_ref[...]).astype(BF16)
        reps = LANES // ROT_HALF
        cos = jnp.concatenate([cost_ref[...]] * reps, axis=0).T
        sin = jnp.concatenate([sint_ref[...]] * reps, axis=0).T
        lane = lax.broadcasted_iota(jnp.int32, (1, LANES), 1) % HEAD_DIM
        first = lane < ROT_HALF
        second = (lane >= ROT_HALF) & (lane < ROT_DIM)
        cos_ref[...] = jnp.where(first | second, cos, 1.0)
        sn_ref[...] = jnp.where(first, -sin, 0.0)
        sp_ref[...] = jnp.where(second, sin, 0.0)

    wo_out_ref[...] = wo_ref[...].astype(BF16)
    acc = _dot(h_ref[...], w_ref[...]) + b_ref[...]

    def rope(a):
        return (a * cos_ref[...] + pltpu.roll(a, LANES - ROT_HALF, 1) * sn_ref[...]
                + pltpu.roll(a, ROT_HALF, 1) * sp_ref[...])

    col_block = _snake(pl.program_id(0), j, nj)
    tiles = o_ref.shape[1] // LANES

    def kind(tile):
        return "q" if tile < ATT_HEADS * HEAD_DIM // LANES else "k" if tile < (ODD_IN - KV_WIDTH) // LANES else "v"

    for t in range(tiles):
        a = acc[:, t * LANES:(t + 1) * LANES]
        kinds = [kind(jb * tiles + t) for jb in range(nj)]
        roped = rope(a) if set(kinds) != {"v"} else None
        by_kind = {"q": lambda: roped * ATT_SCALE, "k": lambda: roped, "v": lambda: a}
        val = by_kind[kinds[-1]]()
        for jb in range(nj - 2, -1, -1):
            if kinds[jb] != kinds[jb + 1]:
                val = jnp.where(col_block <= jb, by_kind[kinds[jb]](), val)
        o_ref[:, t * LANES:(t + 1) * LANES] = val.astype(BF16)


def _qkv(x, gain, w_b, w_o, layer, b, cos_t, sin_t):
    m = x.shape[0]
    nj = 2
    bn = ODD_IN // nj
    wo_in_spec, wo_out_spec, wo_shape = _cast_specs(w_o, layer, 128, nj)
    return pl.pallas_call(
        functools.partial(_qkv_body, nj=nj),
        grid=(m // BM, nj),
        in_specs=[
            pl.BlockSpec((BM, D_MODEL), lambda i, j: (i, 0)),
            pl.BlockSpec((1, D_MODEL), lambda i, j: (0, 0)),
            pl.BlockSpec((D_MODEL, bn), lambda i, j: (0, _snake(i, j, nj))),
            pl.BlockSpec((1, bn), lambda i, j: (0, _snake(i, j, nj))),
            pl.BlockSpec((ROT_HALF, BM), lambda i, j: (0, i)),
            pl.BlockSpec((ROT_HALF, BM), lambda i, j: (0, i)),
            wo_in_spec,
        ],
        out_specs=[pl.BlockSpec((BM, bn), lambda i, j: (i, _snake(i, j, nj))), wo_out_spec],
        out_shape=[jax.ShapeDtypeStruct((m, ODD_IN), BF16), wo_shape],
        scratch_shapes=[
            pltpu.VMEM((BM, D_MODEL), BF16),
            pltpu.VMEM((BM, LANES), F32),
            pltpu.VMEM((BM, LANES), F32),
            pltpu.VMEM((BM, LANES), F32),
        ],
        compiler_params=_params(("arbitrary", "arbitrary"), 56),
        name="qkv_rope",
    )(x, gain, w_b, b, cos_t, sin_t, w_o)


def _split3(x):
    hi = x.astype(BF16)
    r1 = x - hi.astype(F32)
    mid = r1.astype(BF16)
    lo = (r1 - mid.astype(F32)).astype(BF16)
    return hi, mid, lo


def _even_mix_body(u_ref, v_ref, z0_ref, z1_ref, xbc_ref, dt_ref, lng_ref, lnb_ref, ws_ref, bs_ref, cw_ref, cb_ref,
                   dtb_ref, alog_ref, dskip_ref, snorm_ref, e3_ref, shift_ref, o_ref, state_ref, xx_ref):
    c = pl.program_id(0)
    q = CHUNK

    @pl.when(c == 0)
    def _():
        state_ref[...] = jnp.zeros_like(state_ref)
        xx_ref[pl.ds(0, q), :] = jnp.zeros((q, CONV_DIM), BF16)

    row = lax.broadcasted_iota(jnp.int32, (q, q), 0)
    col = lax.broadcasted_iota(jnp.int32, (q, q), 1)
    causal = col <= row

    for g in range(GROUPS):
        seg = slice(g * GDIM, (g + 1) * GDIM)
        vg = v_ref[:, seg].astype(F32)
        mu = jnp.mean(vg, axis=-1, keepdims=True)
        d = vg - mu
        var = jnp.mean(d * d, axis=-1, keepdims=True)
        vn = d * lax.rsqrt(var + EPS) * lng_ref[:, seg] + lnb_ref[:, seg]
        w = jnp.where(causal, ws_ref[g], 0.0).astype(BF16)
        s = _dot(w, vn.astype(BF16)) + bs_ref[:, g:g + 1]
        o_ref[:, seg] = (u_ref[:, seg].astype(F32) * s).astype(BF16)

    x_cur = xbc_ref[...]
    xx_ref[pl.ds(q, q), :] = x_cur
    shifted = _dot(shift_ref[...], xx_ref[...])
    conv = cb_ref[...] + cw_ref[SSD_CONV - 1:SSD_CONV, :] * x_cur.astype(F32)
    for k in range(SSD_CONV - 1):
        conv = conv + cw_ref[k:k + 1, :] * shifted[k * q:(k + 1) * q]
    xx_ref[pl.ds(q - CONV_TAIL, CONV_TAIL), :] = x_cur[q - CONV_TAIL:, :]
    xbc = _silu(conv)
    xs = xbc[:, :D_MODEL]

    dt = jax.nn.softplus(dt_ref[...] + dtb_ref[...])
    a = dt * (-jnp.exp(alog_ref[...]))
    tri = jnp.where(causal, 1.0, 0.0).astype(BF16)
    a_hi, a_mid, a_lo = _split3(a)
    acs = _dot(tri, a_hi) + _dot(tri, a_mid) + _dot(tri, a_lo)
    acs_t = acs.T
    both = jnp.concatenate([dt, acs], axis=0)
    b_hi, b_mid, b_lo = _split3(both)
    both_e = _dot(jnp.concatenate([b_hi, b_mid, b_lo], axis=1), e3_ref[...])
    dt_e = both_e[:q]
    acs_e = both_e[q:]
    last_e = acs_e[q - 1:q, :]
    xdt = xs * dt_e
    xdec = (xdt * jnp.exp(last_e - acs_e)).astype(BF16)
    xdt_b = xdt.astype(BF16)
    grow_e = jnp.exp(acs_e)
    chunk_decay = jnp.exp(last_e)

    lane = lax.broadcasted_iota(jnp.int32, (q, LANES), 1)
    lo_half = lane < HEAD_DIM
    gate = _silu(jnp.concatenate([z0_ref[...], z1_ref[...]], axis=1).astype(F32))

    for g in range(GROUPS):
        seg = slice(g * GDIM, (g + 1) * GDIM)
        b_g = xbc[:, D_MODEL + g * SSD_STATE:D_MODEL + (g + 1) * SSD_STATE].astype(BF16)
        c_g = xbc[:, D_MODEL + GROUPS * SSD_STATE + g * SSD_STATE:
                  D_MODEL + GROUPS * SSD_STATE + (g + 1) * SSD_STATE].astype(BF16)
        cb_causal = jnp.where(causal, _dot_nt(c_g, b_g), 0.0)
        y_off = _dot(c_g, state_ref[g].astype(BF16)) * grow_e[:, seg]
        pieces = []
        for p in range(GDIM // LANES):
            mats = []
            for hh in range(2):
                h = g * (GDIM // HEAD_DIM) + 2 * p + hh
                seg_ij = jnp.minimum(acs[:, h:h + 1] - acs_t[h:h + 1, :], 0.0)
                mats.append((cb_causal * jnp.exp(seg_ij)).astype(BF16))
            x2 = xdt_b[:, g * GDIM + p * LANES:g * GDIM + (p + 1) * LANES]
            zero = jnp.zeros_like(x2)
            rhs = jnp.concatenate([jnp.where(lo_half, x2, zero), jnp.where(lo_half, zero, x2)], axis=0)
            pieces.append(_dot(jnp.concatenate(mats, axis=1), rhs))
        y_diag = jnp.concatenate(pieces, axis=1)
        new_states = _dot_tn(b_g, xdec[:, seg])
        state_ref[g] = state_ref[g] * chunk_decay[:, seg] + new_states
        y = y_diag + y_off + xs[:, seg] * dskip_ref[:, seg]
        y = y * gate[:, seg]
        y = y * lax.rsqrt(jnp.mean(y * y, axis=-1, keepdims=True) + EPS)
        o_ref[:, D_MODEL + g * GDIM:D_MODEL + (g + 1) * GDIM] = (y * snorm_ref[:, seg]).astype(BF16)


def _even_mix(uv, rest, dt_raw, ln_g, ln_b, ws, bs_t, conv_w, conv_b, dt_bias, a_log, d_skip_e, ssd_norm, e3,
              shift):
    m = uv.shape[0]
    full = lambda shape: pl.BlockSpec(shape, lambda c: (0,) * len(shape))
    z_block = CONV_DIM // EVEN_BN
    return pl.pallas_call(
        _even_mix_body,
        grid=(m // CHUNK,),
        in_specs=[
            pl.BlockSpec((CHUNK, D_MODEL), lambda c: (c, 0)),
            pl.BlockSpec((CHUNK, D_MODEL), lambda c: (c, 1)),
            pl.BlockSpec((CHUNK, EVEN_BN), lambda c: (c, z_block)),
            pl.BlockSpec((CHUNK, EVEN_BN), lambda c: (c, z_block + 1)),
            pl.BlockSpec((CHUNK, CONV_DIM), lambda c: (c, 0)),
            pl.BlockSpec((CHUNK, LANES), lambda c: (c, 0)),
            full((1, D_MODEL)), full((1, D_MODEL)),
            full((GROUPS, CHUNK, CHUNK)), full((CHUNK, GROUPS)),
            full((SSD_CONV, CONV_DIM)), full((1, CONV_DIM)),
            full((1, LANES)), full((1, LANES)),
            full((1, D_MODEL)), full((1, D_MODEL)),
            full((3 * LANES, D_MODEL)),
            full(((SSD_CONV - 1) * CHUNK, 2 * CHUNK)),
        ],
        out_specs=pl.BlockSpec((CHUNK, 2 * D_MODEL), lambda c: (c, 0)),
        out_shape=jax.ShapeDtypeStruct((m, 2 * D_MODEL), BF16),
        scratch_shapes=[
            pltpu.VMEM((GROUPS, SSD_STATE, GDIM), F32),
            pltpu.VMEM((2 * CHUNK, CONV_DIM), BF16),
        ],
        compiler_params=_params(("arbitrary",), 48),
        name="even_mix",
    )(uv, uv, rest, rest, rest, dt_raw, ln_g, ln_b, ws, bs_t, conv_w, conv_b, dt_bias, a_log, d_skip_e, ssd_norm,
      e3, shift)


def _swa_body(sink_ref, q_ref, kv_ref, kvp_ref, o_ref):
    n = pl.program_id(0)
    w = CHUNK
    row = lax.broadcasted_iota(jnp.int32, (w, w), 0)
    col = lax.broadcasted_iota(jnp.int32, (w, w), 1)
    own = col <= row
    lo_half = lax.broadcasted_iota(jnp.int32, (w, LANES), 1) < HEAD_DIM
    prev_bias = jnp.where(n > 0, 0.0, -jnp.inf)

    def head_tiles(ref, base, k):
        t = ref[:, base + (k // 2) * LANES:base + (k // 2 + 1) * LANES].astype(F32)
        r = pltpu.roll(t, HEAD_DIM, 1)
        return (t, r) if k % 2 == 0 else (r, t)

    for k in range(GROUPS):
        k_lo, k_hi = head_tiles(kv_ref, 0, k)
        kp_lo, kp_hi = head_tiles(kvp_ref, 0, k)
        v_lo, v_hi = head_tiles(kv_ref, KV_WIDTH, k)
        vp_lo, vp_hi = head_tiles(kvp_ref, KV_WIDTH, k)
        kk = jnp.where(lo_half, k_lo, k_hi).astype(BF16)
        kkp = jnp.where(lo_half, kp_lo, kp_hi).astype(BF16)
        out = []
        for parity in range(2):
            if parity == 0:
                vv = jnp.concatenate([jnp.where(lo_half, v_lo, 1.0), jnp.where(lo_half, vp_lo, 1.0)], axis=0)
            else:
                vv = jnp.concatenate([jnp.where(lo_half, 1.0, v_hi), jnp.where(lo_half, 1.0, vp_hi)], axis=0)
            lhs = []
            for p in range(ATT_REP // 2):
                q2 = q_ref[:, k * GDIM + p * LANES:k * GDIM + (p + 1) * LANES]
                zero = jnp.zeros_like(q2)
                lhs.append(jnp.where(lo_half, q2, zero) if parity == 0 else jnp.where(lo_half, zero, q2))
            lhs = jnp.concatenate(lhs, axis=0)
            s_own = _dot_nt(lhs, kk)
            s_prev = _dot_nt(lhs, kkp)
            probs, esink = [], []
            for p in range(ATT_REP // 2):
                sink = sink_ref[k * ATT_REP + 2 * p + parity]
                s = jnp.where(own, s_own[p * w:(p + 1) * w], s_prev[p * w:(p + 1) * w] + prev_bias)
                mx = jnp.maximum(jnp.max(s, axis=-1, keepdims=True), sink)
                e = jnp.exp(s - mx)
                probs.append(jnp.concatenate([jnp.where(own, e, 0.0).astype(BF16),
                                              jnp.where(own, 0.0, e).astype(BF16)], axis=1))
                esink.append(jnp.exp(sink - mx))
            o = _dot(jnp.concatenate(probs, axis=0), vv.astype(BF16))
            out.append((o, esink))
        for p in range(ATT_REP // 2):
            o_even = out[0][0][p * w:(p + 1) * w]
            o_odd = out[1][0][p * w:(p + 1) * w]
            num = jnp.where(lo_half, o_even, o_odd)
            den = pltpu.roll(jnp.where(lo_half, o_odd, o_even), HEAD_DIM, 1)
            den = den + jnp.where(lo_half, out[0][1][p], out[1][1][p])
            o_ref[:, k * GDIM + p * LANES:k * GDIM + (p + 1) * LANES] = (num / den).astype(BF16)


def _swa(qkv, sinks):
    m = qkv.shape[0]
    kv_block = ATT_HEADS * HEAD_DIM // (2 * KV_WIDTH)
    return pl.pallas_call(
        _swa_body,
        grid=(m // CHUNK,),
        in_specs=[
            pl.BlockSpec(memory_space=pltpu.SMEM),
            pl.BlockSpec((CHUNK, D_MODEL), lambda n: (n, 0)),
            pl.BlockSpec((CHUNK, 2 * KV_WIDTH), lambda n: (n, kv_block)),
            pl.BlockSpec((CHUNK, 2 * KV_WIDTH), lambda n: (jnp.maximum(n - 1, 0), kv_block)),
        ],
        out_specs=pl.BlockSpec((CHUNK, D_MODEL), lambda n: (n, 0)),
        out_shape=jax.ShapeDtypeStruct((m, D_MODEL), BF16),
        compiler_params=_params(("parallel",), 32),
        name="swa",
    )(sinks, qkv, qkv, qkv)


def _xattn_body(x_ref, g_ref, wq_ref, kv_ref, wo_ref, o_ref, wq_b, wo_b):
    @pl.when(pl.program_id(0) == 0)
    def _():
        wq_b[...] = wq_ref[...].astype(BF16)
        wo_b[...] = wo_ref[...].astype(BF16)

    x = x_ref[...]
    h = _rms(x, g_ref[...]).astype(BF16)
    q = _dot(h, wq_b[...]).astype(BF16)
    outs = []
    for hd in range(X_HEADS):
        seg = slice(hd * X_HEAD_DIM, (hd + 1) * X_HEAD_DIM)
        k = kv_ref[:, seg]
        v = kv_ref[:, X_WIDTH + hd * X_HEAD_DIM:X_WIDTH + (hd + 1) * X_HEAD_DIM]
        s = _dot_nt(q[:, seg], k) * X_SCALE
        e = jnp.exp(s - jnp.max(s, axis=-1, keepdims=True))
        o = _dot(e.astype(BF16), v) * (1.0 / jnp.sum(e, axis=-1, keepdims=True))
        outs.append(o.astype(BF16))
    o_ref[...] = x + _dot(jnp.concatenate(outs, axis=1), wo_b[...])


def _xattn(x, gain, w_q, kv, w_o, layer):
    m = x.shape[0]
    bm = 512
    return pl.pallas_call(
        _xattn_body,
        grid=(m // bm,),
        in_specs=[
            pl.BlockSpec((bm, D_MODEL), lambda i: (i, 0)),
            pl.BlockSpec((1, D_MODEL), lambda i: (0, 0)),
            pl.BlockSpec((None, D_MODEL, X_WIDTH), lambda i: (layer, 0, 0)),
            pl.BlockSpec((N_MEM, 2 * X_WIDTH), lambda i: (0, 0)),
            pl.BlockSpec((None, X_WIDTH, D_MODEL), lambda i: (layer, 0, 0)),
        ],
        out_specs=pl.BlockSpec((bm, D_MODEL), lambda i: (i, 0)),
        out_shape=jax.ShapeDtypeStruct((m, D_MODEL), F32),
        scratch_shapes=[pltpu.VMEM((D_MODEL, X_WIDTH), BF16), pltpu.VMEM((X_WIDTH, D_MODEL), BF16)],
        compiler_params=_params(("arbitrary",), 48),
        name="xattn",
    )(x, gain, w_q, kv, w_o)


DOWN_SLAB = 64
WIDE_SLAB = 32


def _ffn(x, gain, w_gu, w_down, layer, name, more_casts=(), out_gain=None):
    casts = [(w_down, layer, DOWN_SLAB)] + list(more_casts)
    act, w_down_b, *copies = _ffn_up(x, gain.reshape(1, -1), w_gu, layer, casts)
    return _mm_rows(act, w_down_b, x, 0.5, name, out_gain), copies


def _pad_lanes(v):
    return jnp.pad(v.reshape(1, -1), ((0, 0), (0, LANES - v.shape[-1])))


def kernel(x, mem, positions, norm_ffn1, w_ffn1_gu, w_ffn1_down, norm_mix, w_in_even, gm_ln_g, gm_ln_b, gm_ws, gm_bs, conv_w, conv_b, dt_bias, a_log, d_skip, ssd_norm, w_out_even, w_qkv, b_qkv, sinks, w_o_odd, norm_xq, norm_mem, w_xq, w_xkv, w_xo, norm_ffn2, w_ffn2_gu, w_ffn2_down, final_norm):
    bsz, seq, d = x.shape
    assert (bsz, seq, d) == (1, SEQ, D_MODEL)
    xr = x.reshape(seq, d)
    memr = mem.reshape(N_MEM, d)
    inv_freq = ROPE_THETA ** (-jnp.arange(0, ROT_DIM, 2, dtype=F32) / ROT_DIM)
    cos_t, sin_t = _rope_table(positions.reshape(1, seq), inv_freq.reshape(ROT_HALF, 1))
    head_of_lane = jnp.arange(D_MODEL, dtype=jnp.int32) // HEAD_DIM
    e1 = (jnp.arange(LANES, dtype=jnp.int32)[:, None] == head_of_lane[None, :]).astype(BF16)
    e3 = jnp.concatenate([e1, e1, e1], axis=0)
    sel_row = jnp.arange((SSD_CONV - 1) * CHUNK, dtype=jnp.int32)[:, None]
    sel_col = jnp.arange(2 * CHUNK, dtype=jnp.int32)[None, :]
    shift = (sel_col == CHUNK + sel_row % CHUNK - (SSD_CONV - 1) + sel_row // CHUNK).astype(BF16)

    w_in_t = jnp.swapaxes(w_in_even, 1, 2)

    w_gu1 = w_ffn1_gu
    for i in range(DEPTH):
        j = i // 2
        xr, (w_gu2,) = _ffn(xr, norm_ffn1[i], w_gu1, w_ffn1_down, i, "ffn1_down",
                            [(w_ffn2_gu, i, WIDE_SLAB)])
        if i % 2 == 0:
            uv, rest, dt_raw, w_out_b = _even_in(xr, norm_mix[i].reshape(1, -1), w_in_t, w_out_even, j)
            mix = _even_mix(
                uv, rest, dt_raw, gm_ln_g[j].reshape(1, -1), gm_ln_b[j].reshape(1, -1), gm_ws[j], gm_bs[j].T,
                conv_w[j], conv_b[j].reshape(1, -1), _pad_lanes(dt_bias[j]), _pad_lanes(a_log[j]),
                jnp.repeat(d_skip[j], HEAD_DIM).reshape(1, -1), ssd_norm[j].reshape(1, -1), e3, shift)
            xr = _mm_rows(mix, w_out_b, xr, 1.0, "even_out")
        else:
            qkv, w_o_b = _qkv(xr, norm_mix[i].reshape(1, -1), w_qkv_b, w_o_odd, j, b_qkv[j].reshape(1, -1),
                              cos_t, sin_t)
            att = _swa(qkv, sinks[j])
            xr = _mm_rows(att, w_o_b, xr, 1.0, "odd_out")
        kv = _norm_mm(memr, norm_mem[i].reshape(1, -1), w_xkv, i, N_MEM, BF16, "mem_kv")
        xr = _xattn(xr, norm_xq[i].reshape(1, -1), w_xq, kv, w_xo, i)
        ahead = []
        if i + 1 < DEPTH:
            ahead.append((w_ffn1_gu, i + 1, WIDE_SLAB))
            if (i + 1) % 2 == 1:
                ahead.append((w_qkv, (i + 1) // 2, WIDE_SLAB))
        out_gain = final_norm.reshape(1, -1) if i + 1 == DEPTH else None
        xr, copies = _ffn(xr, norm_ffn2[i], w_gu2, w_ffn2_down, i, "ffn2_down", ahead, out_gain)
        if copies:
            w_gu1 = copies[0]
            w_qkv_b = copies[1] if len(copies) > 1 else None
    return xr.reshape(bsz, seq, d)
```

```python
import functools

import jax
import jax.numpy as jnp
from jax import lax
from jax.experimental import pallas as pl
from jax.experimental.pallas import tpu as pltpu

F32 = jnp.float32
BF16 = jnp.bfloat16

D_MODEL = 2048
SEQ = 8192
DEPTH = 2
EPS = 1e-5
N_MEM = 256
D_FF = 5632
CHUNK = 128
GROUPS = 4
GDIM = D_MODEL // GROUPS
HEAD_DIM = 64
SSD_HEADS = 32
SSD_STATE = 128
SSD_CONV = 4
CONV_DIM = D_MODEL + 2 * GROUPS * SSD_STATE
EVEN_MAIN = 2 * D_MODEL + D_MODEL + CONV_DIM
ATT_HEADS = 32
ATT_REP = ATT_HEADS // GROUPS
ATT_SCALE = HEAD_DIM ** -0.5
ROT_DIM = HEAD_DIM // 4
ROT_HALF = ROT_DIM // 2
ROPE_THETA = 500000.0
KV_WIDTH = GROUPS * HEAD_DIM
ODD_IN = (ATT_HEADS + 2 * GROUPS) * HEAD_DIM
X_HEADS = 4
X_HEAD_DIM = 128
X_WIDTH = X_HEADS * X_HEAD_DIM
X_SCALE = X_HEAD_DIM ** -0.5

LANES = 128
SUBLANES = 8
CONV_TAIL = 16
BM = 1024
BN = 512
MIB = 1024 * 1024


def _params(semantics, vmem_mib):
    return pltpu.CompilerParams(dimension_semantics=semantics, vmem_limit_bytes=vmem_mib * MIB)


def _rms(x, g):
    ms = jnp.mean(x * x, axis=-1, keepdims=True)
    return x * lax.rsqrt(ms + EPS) * g


def _silu(x):
    return x * jax.nn.sigmoid(x)


def _gelu(x):
    return 0.5 * x * (1.0 + lax.erf(x * (2.0 ** -0.5)))


def _dot(a, b):
    return jnp.dot(a, b, preferred_element_type=F32)


def _dot_nt(a, b):
    return lax.dot_general(a, b, (((1,), (1,)), ((), ())), preferred_element_type=F32)


def _dot_tn(a, b):
    return lax.dot_general(a, b, (((0,), (0,)), ((), ())), preferred_element_type=F32)


def _snake(i, j, nj):
    return jnp.where(i % 2 == 0, j, nj - 1 - j)


def _cast_specs(w, layer, rows, steps_per_row_block):
    _, r, c = w.shape
    n_slabs = pl.cdiv(r, rows)
    slab = lambda i, j: jnp.minimum(i * steps_per_row_block + j, n_slabs - 1)
    return (pl.BlockSpec((None, rows, c), lambda i, j: (layer, slab(i, j), 0)),
            pl.BlockSpec((rows, c), lambda i, j: (slab(i, j), 0)),
            jax.ShapeDtypeStruct((r, c), BF16))


def _slab_rows(w, steps):
    tile = 2 * SUBLANES
    return tile * pl.cdiv(w.shape[1], tile * steps)


def _ffn_up_body(*refs, n_cast, normed_input):
    n_in = 3 if normed_input else 4
    wg_ref, wu_ref = refs[n_in - 2:n_in]
    cast_in, o_ref = refs[n_in:n_in + n_cast], refs[n_in + n_cast]
    cast_out = refs[n_in + n_cast + 1:n_in + 2 * n_cast + 1]
    scratch = refs[n_in + 2 * n_cast + 1:]
    w_ref = scratch[0]
    if normed_input:
        h_ref = refs[0]
    else:
        x_ref, g_ref, h_ref = refs[0], refs[1], scratch[1]

        @pl.when(pl.program_id(1) == 0)
        def _():
            h_ref[...] = _rms(x_ref[...], g_ref[...]).astype(BF16)

    for src, dst in zip(cast_in, cast_out):
        dst[...] = src[...].astype(BF16)
    w_ref[:, :BN] = wg_ref[...].astype(BF16)
    w_ref[:, BN:] = wu_ref[...].astype(BF16)
    gu = _dot(h_ref[...], w_ref[...])
    o_ref[...] = (_silu(gu[:, :BN]) * gu[:, BN:]).astype(BF16)


def _ffn_up(xh, gain, w_gu, layer, cast_weights):
    m = xh.shape[0]
    normed_input = gain is None
    bm = 2 * BM if normed_input else BM
    nj = D_FF // BN
    steps = (m // bm) * nj
    if w_gu.ndim == 3:
        w_spec = lambda off: pl.BlockSpec((None, D_MODEL, BN), lambda i, j: (layer, 0, _snake(i, j, nj) + off))
    else:
        w_spec = lambda off: pl.BlockSpec((D_MODEL, BN), lambda i, j: (0, _snake(i, j, nj) + off))
    cast_specs = [_cast_specs(w, l, _slab_rows(w, steps), nj) for w, l in cast_weights]
    row_specs = [pl.BlockSpec((bm, D_MODEL), lambda i, j: (i, 0))]
    scratch = [pltpu.VMEM((D_MODEL, 2 * BN), BF16)]
    operands = [xh]
    if not normed_input:
        row_specs.append(pl.BlockSpec((1, D_MODEL), lambda i, j: (0, 0)))
        scratch.append(pltpu.VMEM((bm, D_MODEL), BF16))
        operands.append(gain)
    return pl.pallas_call(
        functools.partial(_ffn_up_body, n_cast=len(cast_weights), normed_input=normed_input),
        grid=(m // bm, nj),
        in_specs=row_specs + [w_spec(0), w_spec(nj)] + [s[0] for s in cast_specs],
        out_specs=[pl.BlockSpec((bm, BN), lambda i, j: (i, _snake(i, j, nj)))] + [s[1] for s in cast_specs],
        out_shape=[jax.ShapeDtypeStruct((m, D_FF), BF16)] + [s[2] for s in cast_specs],
        scratch_shapes=scratch,
        compiler_params=_params(("arbitrary", "arbitrary"), 56),
        name="ffn_up",
    )(*operands, w_gu, w_gu, *[w for w, _ in cast_weights])


def _mm_rows_body(a_ref, w_ref, r_ref, *refs, scale, gain_use):
    y = r_ref[...] + scale * _dot(a_ref[...], w_ref[...])
    o_ref = refs[1] if gain_use else refs[0]
    if gain_use == "final":
        y = _rms(y, refs[0][...])
    elif gain_use == "next":
        refs[2][...] = _rms(y, refs[0][...]).astype(BF16)
    o_ref[...] = y


ROWS_BM = 512


def _mm_rows(a, w, res, scale, name, gain=None, gain_use=None):
    m, k = a.shape
    n = w.shape[1]
    row_out = pl.BlockSpec((ROWS_BM, n), lambda i: (i, 0))
    out_specs, out_shape = [row_out], [jax.ShapeDtypeStruct((m, n), F32)]
    if gain_use == "next":
        out_specs.append(row_out)
        out_shape.append(jax.ShapeDtypeStruct((m, n), BF16))
    gains = [] if gain is None else [gain]
    out = pl.pallas_call(
        functools.partial(_mm_rows_body, scale=scale, gain_use=gain_use),
        grid=(m // ROWS_BM,),
        in_specs=[
            pl.BlockSpec((ROWS_BM, k), lambda i: (i, 0)),
            pl.BlockSpec((k, n), lambda i: (0, 0), pipeline_mode=pl.Buffered(1)),
            pl.BlockSpec((ROWS_BM, n), lambda i: (i, 0)),
        ] + [pl.BlockSpec((1, n), lambda i: (0, 0))] * len(gains),
        out_specs=out_specs,
        out_shape=out_shape,
        compiler_params=_params(("arbitrary",), 60),
        name=name,
    )(a, w, res, *gains)
    return out if gain_use == "next" else out[0]


def _norm_mm_body(x_ref, g_ref, w_ref, o_ref, h_ref):
    @pl.when(pl.program_id(1) == 0)
    def _():
        h_ref[...] = _rms(x_ref[...], g_ref[...]).astype(BF16)

    o_ref[...] = _dot(h_ref[...], w_ref[...].astype(BF16)).astype(o_ref.dtype)


def _norm_mm(x, gain, w, layer, bm, out_dtype, name):
    m, k = x.shape
    n = w.shape[2]
    return pl.pallas_call(
        _norm_mm_body,
        grid=(m // bm, n // BN),
        in_specs=[
            pl.BlockSpec((bm, k), lambda i, j: (i, 0)),
            pl.BlockSpec((1, k), lambda i, j: (0, 0)),
            pl.BlockSpec((None, k, BN), lambda i, j: (layer, 0, j)),
        ],
        out_specs=pl.BlockSpec((bm, BN), lambda i, j: (i, j)),
        out_shape=jax.ShapeDtypeStruct((m, n), out_dtype),
        scratch_shapes=[pltpu.VMEM((bm, k), BF16)],
        compiler_params=_params(("parallel", "arbitrary"), 40),
        name=name,
    )(x, gain, w)


EVEN_BN = 2 * BN


def _even_uv_body(h_ref, w_ref, o_ref):
    o_ref[...] = _gelu(_dot_nt(h_ref[...], w_ref[...].astype(BF16))).astype(BF16)


def _even_rest_body(h_ref, w_ref, wdt_ref, wo_ref, o_ref, dt_ref, wo_out_ref):
    @pl.when(pl.program_id(1) == 0)
    def _():
        row = lax.broadcasted_iota(jnp.int32, (LANES, 1), 0)
        dt_ref[...] = _dot_nt(h_ref[...], jnp.where(row < SSD_HEADS, wdt_ref[...], 0.0).astype(BF16))

    wo_out_ref[...] = wo_ref[...].astype(BF16)
    o_ref[...] = _dot_nt(h_ref[...], w_ref[...].astype(BF16)).astype(BF16)


def _even_in(h, w_in_t, w_out, layer):
    m = h.shape[0]
    bn = EVEN_BN
    x_spec = pl.BlockSpec((BM, D_MODEL), lambda i, j: (i, 0))

    nj = 2 * D_MODEL // bn
    uv = pl.pallas_call(
        _even_uv_body,
        grid=(m // BM, nj),
        in_specs=[x_spec,
                  pl.BlockSpec((None, bn, D_MODEL), lambda i, j: (layer, _snake(i, j, nj), 0))],
        out_specs=pl.BlockSpec((BM, bn), lambda i, j: (i, _snake(i, j, nj))),
        out_shape=jax.ShapeDtypeStruct((m, 2 * D_MODEL), BF16),
        compiler_params=_params(("arbitrary", "arbitrary"), 56),
        name="even_in_uv",
    )(h, w_in_t)

    n_xbc, n_z = CONV_DIM // bn, D_MODEL // bn
    nr = n_xbc + n_z
    z_first, xbc_first = 2 * D_MODEL // bn, 3 * D_MODEL // bn

    def w_block(i, j):
        jc = _snake(i, j, nr)
        return jnp.where(jc < n_xbc, xbc_first + jc, z_first + jc - n_xbc)

    wo_in_spec, wo_out_spec, wo_shape = _cast_specs(w_out, layer, 128, nr)
    rest, dt_raw, w_out_b = pl.pallas_call(
        _even_rest_body,
        grid=(m // BM, nr),
        in_specs=[x_spec,
                  pl.BlockSpec((None, bn, D_MODEL), lambda i, j: (layer, w_block(i, j), 0)),
                  pl.BlockSpec((None, LANES, D_MODEL), lambda i, j: (layer, EVEN_MAIN // LANES, 0)),
                  wo_in_spec],
        out_specs=[pl.BlockSpec((BM, bn), lambda i, j: (i, _snake(i, j, nr))),
                   pl.BlockSpec((BM, LANES), lambda i, j: (i, 0)),
                   wo_out_spec],
        out_shape=[jax.ShapeDtypeStruct((m, CONV_DIM + D_MODEL), BF16),
                   jax.ShapeDtypeStruct((m, LANES), F32),
                   wo_shape],
        compiler_params=_params(("arbitrary", "arbitrary"), 56),
        name="even_in_rest",
    )(h, w_in_t, w_in_t, w_out)
    return uv, rest, dt_raw, w_out_b


def _rope_table_body(pos_ref, invf_ref, cos_ref, sin_ref):
    ang = pos_ref[...].astype(F32) * invf_ref[...]
    cos_ref[...] = jnp.cos(ang)
    sin_ref[...] = jnp.sin(ang)


def _rope_table(pos_row, invf_col):
    shape = jax.ShapeDtypeStruct((ROT_HALF, pos_row.shape[1]), F32)
    return pl.pallas_call(_rope_table_body, out_shape=[shape, shape], name="rope_table")(pos_row, invf_col)


def _qkv_body(h_ref, w_ref, b_ref, cost_ref, sint_ref, wo_ref, o_ref, wo_out_ref,
              cos_ref, sn_ref, sp_ref, *, nj):
    j = pl.program_id(1)

    @pl.when(j == 0)
    def _():
        reps = LANES // ROT_HALF
        cos = jnp.concatenate([cost_ref[...]] * reps, axis=0).T
        sin = jnp.concatenate([sint_ref[...]] * reps, axis=0).T
        lane = lax.broadcasted_iota(jnp.int32, (1, LANES), 1) % HEAD_DIM
        first = lane < ROT_HALF
        second = (lane >= ROT_HALF) & (lane < ROT_DIM)
        cos_ref[...] = jnp.where(first | second, cos, 1.0)
        sn_ref[...] = jnp.where(first, -sin, 0.0)
        sp_ref[...] = jnp.where(second, sin, 0.0)

    wo_out_ref[...] = wo_ref[...].astype(BF16)
    acc = _dot(h_ref[...], w_ref[...]) + b_ref[...]

    def rope(a):
        return (a * cos_ref[...] + pltpu.roll(a, LANES - ROT_HALF, 1) * sn_ref[...]
                + pltpu.roll(a, ROT_HALF, 1) * sp_ref[...])

    col_block = _snake(pl.program_id(0), j, nj)
    tiles = o_ref.shape[1] // LANES

    def kind(tile):
        return "q" if tile < ATT_HEADS * HEAD_DIM // LANES else "k" if tile < (ODD_IN - KV_WIDTH) // LANES else "v"

    for t in range(tiles):
        a = acc[:, t * LANES:(t + 1) * LANES]
        kinds = [kind(jb * tiles + t) for jb in range(nj)]
        roped = rope(a) if set(kinds) != {"v"} else None
        by_kind = {"q": lambda: roped * ATT_SCALE, "k": lambda: roped, "v": lambda: a}
        val = by_kind[kinds[-1]]()
        for jb in range(nj - 2, -1, -1):
            if kinds[jb] != kinds[jb + 1]:
                val = jnp.where(col_block <= jb, by_kind[kinds[jb]](), val)
        o_ref[:, t * LANES:(t + 1) * LANES] = val.astype(BF16)


def _qkv(h, w_b, w_o, layer, b, cos_t, sin_t):
    m = h.shape[0]
    nj = 2
    bn = ODD_IN // nj
    wo_in_spec, wo_out_spec, wo_shape = _cast_specs(w_o, layer, 128, nj)
    return pl.pallas_call(
        functools.partial(_qkv_body, nj=nj),
        grid=(m // BM, nj),
        in_specs=[
            pl.BlockSpec((BM, D_MODEL), lambda i, j: (i, 0)),
            pl.BlockSpec((D_MODEL, bn), lambda i, j: (0, _snake(i, j, nj))),
            pl.BlockSpec((1, bn), lambda i, j: (0, _snake(i, j, nj))),
            pl.BlockSpec((ROT_HALF, BM), lambda i, j: (0, i)),
            pl.BlockSpec((ROT_HALF, BM), lambda i, j: (0, i)),
            wo_in_spec,
        ],
        out_specs=[pl.BlockSpec((BM, bn), lambda i, j: (i, _snake(i, j, nj))), wo_out_spec],
        out_shape=[jax.ShapeDtypeStruct((m, ODD_IN), BF16), wo_shape],
        scratch_shapes=[
            pltpu.VMEM((BM, LANES), F32),
            pltpu.VMEM((BM, LANES), F32),
            pltpu.VMEM((BM, LANES), F32),
        ],
        compiler_params=_params(("arbitrary", "arbitrary"), 56),
        name="qkv_rope",
    )(h, w_b, b, cos_t, sin_t, w_o)


def _split3(x):
    hi = x.astype(BF16)
    r1 = x - hi.astype(F32)
    mid = r1.astype(BF16)
    lo = (r1 - mid.astype(F32)).astype(BF16)
    return hi, mid, lo


def _even_mix_body(u_ref, v_ref, z0_ref, z1_ref, xbc_ref, dt_ref, lng_ref, lnb_ref, ws_ref, bs_ref, cw_ref, cb_ref,
                   dtb_ref, alog_ref, dskip_ref, snorm_ref, e3_ref, shift_ref, o_ref, state_ref, xx_ref):
    c = pl.program_id(0)
    q = CHUNK

    @pl.when(c == 0)
    def _():
        state_ref[...] = jnp.zeros_like(state_ref)
        xx_ref[pl.ds(0, q), :] = jnp.zeros((q, CONV_DIM), BF16)

    row = lax.broadcasted_iota(jnp.int32, (q, q), 0)
    col = lax.broadcasted_iota(jnp.int32, (q, q), 1)
    causal = col <= row

    for g in range(GROUPS):
        seg = slice(g * GDIM, (g + 1) * GDIM)
        vg = v_ref[:, seg].astype(F32)
        mu = jnp.mean(vg, axis=-1, keepdims=True)
        d = vg - mu
        var = jnp.mean(d * d, axis=-1, keepdims=True)
        vn = d * lax.rsqrt(var + EPS) * lng_ref[:, seg] + lnb_ref[:, seg]
        w = jnp.where(causal, ws_ref[g], 0.0).astype(BF16)
        s = _dot(w, vn.astype(BF16)) + bs_ref[:, g:g + 1]
        o_ref[:, seg] = (u_ref[:, seg].astype(F32) * s).astype(BF16)

    x_cur = xbc_ref[...]
    xx_ref[pl.ds(q, q), :] = x_cur
    shifted = _dot(shift_ref[...], xx_ref[...])
    conv = cb_ref[...] + cw_ref[SSD_CONV - 1:SSD_CONV, :] * x_cur.astype(F32)
    for k in range(SSD_CONV - 1):
        conv = conv + cw_ref[k:k + 1, :] * shifted[k * q:(k + 1) * q]
    xx_ref[pl.ds(q - CONV_TAIL, CONV_TAIL), :] = x_cur[q - CONV_TAIL:, :]
    xbc = _silu(conv)
    xs = xbc[:, :D_MODEL]

    dt = jax.nn.softplus(dt_ref[...] + dtb_ref[...])
    a = dt * (-jnp.exp(alog_ref[...]))
    tri = jnp.where(causal, 1.0, 0.0).astype(BF16)
    a_hi, a_mid, a_lo = _split3(a)
    acs = _dot(tri, a_hi) + _dot(tri, a_mid) + _dot(tri, a_lo)
    acs_t = acs.T
    both = jnp.concatenate([dt, acs], axis=0)
    b_hi, b_mid, b_lo = _split3(both)
    both_e = _dot(jnp.concatenate([b_hi, b_mid, b_lo], axis=1), e3_ref[...])
    dt_e = both_e[:q]
    acs_e = both_e[q:]
    last_e = acs_e[q - 1:q, :]
    xdt = xs * dt_e
    xdec = (xdt * jnp.exp(last_e - acs_e)).astype(BF16)
    xdt_b = xdt.astype(BF16)
    grow_e = jnp.exp(acs_e)
    chunk_decay = jnp.exp(last_e)

    lane = lax.broadcasted_iota(jnp.int32, (q, LANES), 1)
    lo_half = lane < HEAD_DIM
    gate = _silu(jnp.concatenate([z0_ref[...], z1_ref[...]], axis=1).astype(F32))

    for g in range(GROUPS):
        seg = slice(g * GDIM, (g + 1) * GDIM)
        b_g = xbc[:, D_MODEL + g * SSD_STATE:D_MODEL + (g + 1) * SSD_STATE].astype(BF16)
        c_g = xbc[:, D_MODEL + GROUPS * SSD_STATE + g * SSD_STATE:
                  D_MODEL + GROUPS * SSD_STATE + (g + 1) * SSD_STATE].astype(BF16)
        cb_causal = jnp.where(causal, _dot_nt(c_g, b_g), 0.0)
        y_off = _dot(c_g, state_ref[g].astype(BF16)) * grow_e[:, seg]
        pieces = []
        for p in range(GDIM // LANES):
            mats = []
            for hh in range(2):
                h = g * (GDIM // HEAD_DIM) + 2 * p + hh
                seg_ij = jnp.minimum(acs[:, h:h + 1] - acs_t[h:h + 1, :], 0.0)
                mats.append((cb_causal * jnp.exp(seg_ij)).astype(BF16))
            x2 = xdt_b[:, g * GDIM + p * LANES:g * GDIM + (p + 1) * LANES]
            zero = jnp.zeros_like(x2)
            rhs = jnp.concatenate([jnp.where(lo_half, x2, zero), jnp.where(lo_half, zero, x2)], axis=0)
            pieces.append(_dot(jnp.concatenate(mats, axis=1), rhs))
        y_diag = jnp.concatenate(pieces, axis=1)
        new_states = _dot_tn(b_g, xdec[:, seg])
        state_ref[g] = state_ref[g] * chunk_decay[:, seg] + new_states
        y = y_diag + y_off + xs[:, seg] * dskip_ref[:, seg]
        y = y * gate[:, seg]
        y = y * lax.rsqrt(jnp.mean(y * y, axis=-1, keepdims=True) + EPS)
        o_ref[:, D_MODEL + g * GDIM:D_MODEL + (g + 1) * GDIM] = (y * snorm_ref[:, seg]).astype(BF16)


def _even_mix(uv, rest, dt_raw, ln_g, ln_b, ws, bs_t, conv_w, conv_b, dt_bias, a_log, d_skip_e, ssd_norm, e3,
              shift):
    m = uv.shape[0]
    full = lambda shape: pl.BlockSpec(shape, lambda c: (0,) * len(shape))
    z_block = CONV_DIM // EVEN_BN
    return pl.pallas_call(
        _even_mix_body,
        grid=(m // CHUNK,),
        in_specs=[
            pl.BlockSpec((CHUNK, D_MODEL), lambda c: (c, 0)),
            pl.BlockSpec((CHUNK, D_MODEL), lambda c: (c, 1)),
            pl.BlockSpec((CHUNK, EVEN_BN), lambda c: (c, z_block)),
            pl.BlockSpec((CHUNK, EVEN_BN), lambda c: (c, z_block + 1)),
            pl.BlockSpec((CHUNK, CONV_DIM), lambda c: (c, 0)),
            pl.BlockSpec((CHUNK, LANES), lambda c: (c, 0)),
            full((1, D_MODEL)), full((1, D_MODEL)),
            full((GROUPS, CHUNK, CHUNK)), full((CHUNK, GROUPS)),
            full((SSD_CONV, CONV_DIM)), full((1, CONV_DIM)),
            full((1, LANES)), full((1, LANES)),
            full((1, D_MODEL)), full((1, D_MODEL)),
            full((3 * LANES, D_MODEL)),
            full(((SSD_CONV - 1) * CHUNK, 2 * CHUNK)),
        ],
        out_specs=pl.BlockSpec((CHUNK, 2 * D_MODEL), lambda c: (c, 0)),
        out_shape=jax.ShapeDtypeStruct((m, 2 * D_MODEL), BF16),
        scratch_shapes=[
            pltpu.VMEM((GROUPS, SSD_STATE, GDIM), F32),
            pltpu.VMEM((2 * CHUNK, CONV_DIM), BF16),
        ],
        compiler_params=_params(("arbitrary",), 48),
        name="even_mix",
    )(uv, uv, rest, rest, rest, dt_raw, ln_g, ln_b, ws, bs_t, conv_w, conv_b, dt_bias, a_log, d_skip_e, ssd_norm,
      e3, shift)


def _swa_body(sink_ref, q_ref, kv_ref, kvp_ref, o_ref):
    n = pl.program_id(0)
    w = CHUNK
    row = lax.broadcasted_iota(jnp.int32, (w, w), 0)
    col = lax.broadcasted_iota(jnp.int32, (w, w), 1)
    own = col <= row
    lo_half = lax.broadcasted_iota(jnp.int32, (w, LANES), 1) < HEAD_DIM
    prev_bias = jnp.where(n > 0, 0.0, -jnp.inf)

    def head_tiles(ref, base, k):
        t = ref[:, base + (k // 2) * LANES:base + (k // 2 + 1) * LANES].astype(F32)
        r = pltpu.roll(t, HEAD_DIM, 1)
        return (t, r) if k % 2 == 0 else (r, t)

    for k in range(GROUPS):
        k_lo, k_hi = head_tiles(kv_ref, 0, k)
        kp_lo, kp_hi = head_tiles(kvp_ref, 0, k)
        v_lo, v_hi = head_tiles(kv_ref, KV_WIDTH, k)
        vp_lo, vp_hi = head_tiles(kvp_ref, KV_WIDTH, k)
        kk = jnp.where(lo_half, k_lo, k_hi).astype(BF16)
        kkp = jnp.where(lo_half, kp_lo, kp_hi).astype(BF16)
        out = []
        for parity in range(2):
            if parity == 0:
                vv = jnp.concatenate([jnp.where(lo_half, v_lo, 1.0), jnp.where(lo_half, vp_lo, 1.0)], axis=0)
            else:
                vv = jnp.concatenate([jnp.where(lo_half, 1.0, v_hi), jnp.where(lo_half, 1.0, vp_hi)], axis=0)
            lhs = []
            for p in range(ATT_REP // 2):
                q2 = q_ref[:, k * GDIM + p * LANES:k * GDIM + (p + 1) * LANES]
                zero = jnp.zeros_like(q2)
                lhs.append(jnp.where(lo_half, q2, zero) if parity == 0 else jnp.where(lo_half, zero, q2))
            lhs = jnp.concatenate(lhs, axis=0)
            s_own = _dot_nt(lhs, kk)
            s_prev = _dot_nt(lhs, kkp)
            probs, esink = [], []
            for p in range(ATT_REP // 2):
                sink = sink_ref[k * ATT_REP + 2 * p + parity]
                s = jnp.where(own, s_own[p * w:(p + 1) * w], s_prev[p * w:(p + 1) * w] + prev_bias)
                mx = jnp.maximum(jnp.max(s, axis=-1, keepdims=True), sink)
                e = jnp.exp(s - mx)
                probs.append(jnp.concatenate([jnp.where(own, e, 0.0).astype(BF16),
                                              jnp.where(own, 0.0, e).astype(BF16)], axis=1))
                esink.append(jnp.exp(sink - mx))
            o = _dot(jnp.concatenate(probs, axis=0), vv.astype(BF16))
            out.append((o, esink))
        for p in range(ATT_REP // 2):
            o_even = out[0][0][p * w:(p + 1) * w]
            o_odd = out[1][0][p * w:(p + 1) * w]
            num = jnp.where(lo_half, o_even, o_odd)
            den = pltpu.roll(jnp.where(lo_half, o_odd, o_even), HEAD_DIM, 1)
            den = den + jnp.where(lo_half, out[0][1][p], out[1][1][p])
            o_ref[:, k * GDIM + p * LANES:k * GDIM + (p + 1) * LANES] = (num / den).astype(BF16)


def _swa(qkv, sinks):
    m = qkv.shape[0]
    kv_block = ATT_HEADS * HEAD_DIM // (2 * KV_WIDTH)
    return pl.pallas_call(
        _swa_body,
        grid=(m // CHUNK,),
        in_specs=[
            pl.BlockSpec(memory_space=pltpu.SMEM),
            pl.BlockSpec((CHUNK, D_MODEL), lambda n: (n, 0)),
            pl.BlockSpec((CHUNK, 2 * KV_WIDTH), lambda n: (n, kv_block)),
            pl.BlockSpec((CHUNK, 2 * KV_WIDTH), lambda n: (jnp.maximum(n - 1, 0), kv_block)),
        ],
        out_specs=pl.BlockSpec((CHUNK, D_MODEL), lambda n: (n, 0)),
        out_shape=jax.ShapeDtypeStruct((m, D_MODEL), BF16),
        compiler_params=_params(("parallel",), 32),
        name="swa",
    )(sinks, qkv, qkv, qkv)


def _xattn_body(x_ref, g_ref, wq_ref, kv_ref, wo_ref, gn_ref, o_ref, hn_ref, wq_b, wo_b):
    @pl.when(pl.program_id(0) == 0)
    def _():
        wq_b[...] = wq_ref[...].astype(BF16)
        wo_b[...] = wo_ref[...].astype(BF16)

    x = x_ref[...]
    h = _rms(x, g_ref[...]).astype(BF16)
    q = _dot(h, wq_b[...]).astype(BF16)
    outs = []
    for hd in range(X_HEADS):
        seg = slice(hd * X_HEAD_DIM, (hd + 1) * X_HEAD_DIM)
        k = kv_ref[:, seg]
        v = kv_ref[:, X_WIDTH + hd * X_HEAD_DIM:X_WIDTH + (hd + 1) * X_HEAD_DIM]
        s = _dot_nt(q[:, seg], k) * X_SCALE
        e = jnp.exp(s - jnp.max(s, axis=-1, keepdims=True))
        o = _dot(e.astype(BF16), v) * (1.0 / jnp.sum(e, axis=-1, keepdims=True))
        outs.append(o.astype(BF16))
    y = x + _dot(jnp.concatenate(outs, axis=1), wo_b[...])
    o_ref[...] = y
    hn_ref[...] = _rms(y, gn_ref[...]).astype(BF16)


def _xattn(x, gain, w_q, kv, w_o, layer, next_gain):
    m = x.shape[0]
    bm = 512
    row = pl.BlockSpec((bm, D_MODEL), lambda i: (i, 0))
    vec = pl.BlockSpec((1, D_MODEL), lambda i: (0, 0))
    return pl.pallas_call(
        _xattn_body,
        grid=(m // bm,),
        in_specs=[
            row, vec,
            pl.BlockSpec((None, D_MODEL, X_WIDTH), lambda i: (layer, 0, 0)),
            pl.BlockSpec((N_MEM, 2 * X_WIDTH), lambda i: (0, 0)),
            pl.BlockSpec((None, X_WIDTH, D_MODEL), lambda i: (layer, 0, 0)),
            vec,
        ],
        out_specs=[row, row],
        out_shape=[jax.ShapeDtypeStruct((m, D_MODEL), F32), jax.ShapeDtypeStruct((m, D_MODEL), BF16)],
        scratch_shapes=[pltpu.VMEM((D_MODEL, X_WIDTH), BF16), pltpu.VMEM((X_WIDTH, D_MODEL), BF16)],
        compiler_params=_params(("arbitrary",), 48),
        name="xattn",
    )(x, gain, w_q, kv, w_o, next_gain)


def _ffn(x, h, gain, w_gu, w_down, layer, name, more_casts, out_gain, gain_use):
    casts = [(w_down, layer)] + list(more_casts)
    if h is None:
        act, w_down_b, *copies = _ffn_up(x, gain.reshape(1, -1), w_gu, layer, casts)
    else:
        act, w_down_b, *copies = _ffn_up(h, None, w_gu, layer, casts)
    out = _mm_rows(act, w_down_b, x, 0.5, name, out_gain.reshape(1, -1), gain_use)
    y, h_next = out if gain_use == "next" else (out, None)
    return y, h_next, copies


def _pad_lanes(v):
    return jnp.pad(v.reshape(1, -1), ((0, 0), (0, LANES - v.shape[-1])))


def kernel(x, mem, positions, norm_ffn1, w_ffn1_gu, w_ffn1_down, norm_mix, w_in_even, gm_ln_g, gm_ln_b, gm_ws, gm_bs, conv_w, conv_b, dt_bias, a_log, d_skip, ssd_norm, w_out_even, w_qkv, b_qkv, sinks, w_o_odd, norm_xq, norm_mem, w_xq, w_xkv, w_xo, norm_ffn2, w_ffn2_gu, w_ffn2_down, final_norm):
    bsz, seq, d = x.shape
    assert (bsz, seq, d) == (1, SEQ, D_MODEL)
    xr = x.reshape(seq, d)
    memr = mem.reshape(N_MEM, d)
    inv_freq = ROPE_THETA ** (-jnp.arange(0, ROT_DIM, 2, dtype=F32) / ROT_DIM)
    cos_t, sin_t = _rope_table(positions.reshape(1, seq), inv_freq.reshape(ROT_HALF, 1))
    head_of_lane = jnp.arange(D_MODEL, dtype=jnp.int32) // HEAD_DIM
    e1 = (jnp.arange(LANES, dtype=jnp.int32)[:, None] == head_of_lane[None, :]).astype(BF16)
    e3 = jnp.concatenate([e1, e1, e1], axis=0)
    sel_row = jnp.arange((SSD_CONV - 1) * CHUNK, dtype=jnp.int32)[:, None]
    sel_col = jnp.arange(2 * CHUNK, dtype=jnp.int32)[None, :]
    shift = (sel_col == CHUNK + sel_row % CHUNK - (SSD_CONV - 1) + sel_row // CHUNK).astype(BF16)

    w_in_t = jnp.swapaxes(w_in_even, 1, 2)

    w_gu1 = w_ffn1_gu
    h = None
    for i in range(DEPTH):
        j = i // 2
        xr, h, (w_gu2,) = _ffn(xr, h, norm_ffn1[i], w_gu1, w_ffn1_down, i, "ffn1_down",
                               [(w_ffn2_gu, i)], norm_mix[i], "next")
        if i % 2 == 0:
            uv, rest, dt_raw, w_out_b = _even_in(h, w_in_t, w_out_even, j)
            mix = _even_mix(
                uv, rest, dt_raw, gm_ln_g[j].reshape(1, -1), gm_ln_b[j].reshape(1, -1), gm_ws[j], gm_bs[j].T,
                conv_w[j], conv_b[j].reshape(1, -1), _pad_lanes(dt_bias[j]), _pad_lanes(a_log[j]),
                jnp.repeat(d_skip[j], HEAD_DIM).reshape(1, -1), ssd_norm[j].reshape(1, -1), e3, shift)
            xr = _mm_rows(mix, w_out_b, xr, 1.0, "even_out")
        else:
            qkv, w_o_b = _qkv(h, w_qkv_b, w_o_odd, j, b_qkv[j].reshape(1, -1), cos_t, sin_t)
            att = _swa(qkv, sinks[j])
            xr = _mm_rows(att, w_o_b, xr, 1.0, "odd_out")
        kv = _norm_mm(memr, norm_mem[i].reshape(1, -1), w_xkv, i, N_MEM, BF16, "mem_kv")
        xr, h = _xattn(xr, norm_xq[i].reshape(1, -1), w_xq, kv, w_xo, i, norm_ffn2[i].reshape(1, -1))
        last = i + 1 == DEPTH
        ahead = []
        if not last:
            ahead.append((w_ffn1_gu, i + 1))
            if (i + 1) % 2 == 1:
                ahead.append((w_qkv, (i + 1) // 2))
        xr, h, copies = _ffn(xr, h, norm_ffn2[i], w_gu2, w_ffn2_down, i, "ffn2_down", ahead,
                             final_norm if last else norm_ffn1[i + 1], "final" if last else "next")
        if copies:
            w_gu1 = copies[0]
            w_qkv_b = copies[1] if len(copies) > 1 else None
    return xr.reshape(bsz, seq, d)
```

```python
import functools

import jax
import jax.numpy as jnp
from jax import lax
from jax.experimental import pallas as pl
from jax.experimental.pallas import tpu as pltpu

F32 = jnp.float32
BF16 = jnp.bfloat16

D_MODEL = 2048
SEQ = 8192
DEPTH = 2
EPS = 1e-5
N_MEM = 256
D_FF = 5632
CHUNK = 128
GROUPS = 4
GDIM = D_MODEL // GROUPS
HEAD_DIM = 64
SSD_HEADS = 32
SSD_STATE = 128
SSD_CONV = 4
CONV_DIM = D_MODEL + 2 * GROUPS * SSD_STATE
EVEN_MAIN = 2 * D_MODEL + D_MODEL + CONV_DIM
ATT_HEADS = 32
ATT_REP = ATT_HEADS // GROUPS
ATT_SCALE = HEAD_DIM ** -0.5
ROT_DIM = HEAD_DIM // 4
ROT_HALF = ROT_DIM // 2
ROPE_THETA = 500000.0
KV_WIDTH = GROUPS * HEAD_DIM
ODD_IN = (ATT_HEADS + 2 * GROUPS) * HEAD_DIM
X_HEADS = 4
X_HEAD_DIM = 128
X_WIDTH = X_HEADS * X_HEAD_DIM
X_SCALE = X_HEAD_DIM ** -0.5

LANES = 128
SUBLANES = 8
MXU_COLS = 256
CONV_TAIL = 16
BM = 1024
BN = 512
MIB = 1024 * 1024


def _params(semantics, vmem_mib):
    return pltpu.CompilerParams(dimension_semantics=semantics, vmem_limit_bytes=vmem_mib * MIB)


def _rms(x, g):
    ms = jnp.mean(x * x, axis=-1, keepdims=True)
    return x * lax.rsqrt(ms + EPS) * g


def _silu(x):
    return x * jax.nn.sigmoid(x)


def _gelu(x):
    return 0.5 * x * (1.0 + lax.erf(x * (2.0 ** -0.5)))


def _dot(a, b):
    return jnp.dot(a, b, preferred_element_type=F32)


def _dot_nt(a, b):
    return lax.dot_general(a, b, (((1,), (1,)), ((), ())), preferred_element_type=F32)


def _dot_tn(a, b):
    return lax.dot_general(a, b, (((0,), (0,)), ((), ())), preferred_element_type=F32)


def _snake(i, j, nj):
    return jnp.where(i % 2 == 0, j, nj - 1 - j)


def _cast_specs(w, layer, rows, steps_per_row_block):
    _, r, c = w.shape
    n_slabs = pl.cdiv(r, rows)
    slab = lambda i, j: jnp.minimum(i * steps_per_row_block + j, n_slabs - 1)
    return (pl.BlockSpec((None, rows, c), lambda i, j: (layer, slab(i, j), 0)),
            pl.BlockSpec((rows, c), lambda i, j: (slab(i, j), 0)),
            jax.ShapeDtypeStruct((r, c), BF16))


def _slab_rows(w, steps):
    tile = 2 * SUBLANES
    return tile * pl.cdiv(w.shape[1], tile * steps)


def _ffn_up_body(*refs, n_cast, normed_input):
    n_in = 3 if normed_input else 4
    wg_ref, wu_ref = refs[n_in - 2:n_in]
    cast_in, o_ref = refs[n_in:n_in + n_cast], refs[n_in + n_cast]
    cast_out = refs[n_in + n_cast + 1:n_in + 2 * n_cast + 1]
    scratch = refs[n_in + 2 * n_cast + 1:]
    w_ref = scratch[0]
    if normed_input:
        h_ref = refs[0]
    else:
        x_ref, g_ref, h_ref = refs[0], refs[1], scratch[1]

        @pl.when(pl.program_id(1) == 0)
        def _():
            h_ref[...] = _rms(x_ref[...], g_ref[...]).astype(BF16)

    for src, dst in zip(cast_in, cast_out):
        dst[...] = src[...].astype(BF16)
    groups = BN // MXU_COLS
    for c in range(groups):
        w_ref[:, (2 * c) * MXU_COLS:(2 * c + 1) * MXU_COLS] = wg_ref[:, c * MXU_COLS:(c + 1) * MXU_COLS].astype(BF16)
        w_ref[:, (2 * c + 1) * MXU_COLS:(2 * c + 2) * MXU_COLS] = wu_ref[:, c * MXU_COLS:(c + 1) * MXU_COLS].astype(BF16)
    gu = _dot(h_ref[...], w_ref[...])
    for c in range(groups):
        g = gu[:, (2 * c) * MXU_COLS:(2 * c + 1) * MXU_COLS]
        u = gu[:, (2 * c + 1) * MXU_COLS:(2 * c + 2) * MXU_COLS]
        o_ref[:, c * MXU_COLS:(c + 1) * MXU_COLS] = (_silu(g) * u).astype(BF16)


def _ffn_up(xh, gain, w_gu, layer, cast_weights):
    m = xh.shape[0]
    normed_input = gain is None
    bm = 2 * BM if normed_input else BM
    nj = D_FF // BN
    steps = (m // bm) * nj
    if w_gu.ndim == 3:
        w_spec = lambda off: pl.BlockSpec((None, D_MODEL, BN), lambda i, j: (layer, 0, _snake(i, j, nj) + off))
    else:
        w_spec = lambda off: pl.BlockSpec((D_MODEL, BN), lambda i, j: (0, _snake(i, j, nj) + off))
    cast_specs = [_cast_specs(w, l, _slab_rows(w, steps), nj) for w, l in cast_weights]
    row_specs = [pl.BlockSpec((bm, D_MODEL), lambda i, j: (i, 0))]
    scratch = [pltpu.VMEM((D_MODEL, 2 * BN), BF16)]
    operands = [xh]
    if not normed_input:
        row_specs.append(pl.BlockSpec((1, D_MODEL), lambda i, j: (0, 0)))
        scratch.append(pltpu.VMEM((bm, D_MODEL), BF16))
        operands.append(gain)
    return pl.pallas_call(
        functools.partial(_ffn_up_body, n_cast=len(cast_weights), normed_input=normed_input),
        grid=(m // bm, nj),
        in_specs=row_specs + [w_spec(0), w_spec(nj)] + [s[0] for s in cast_specs],
        out_specs=[pl.BlockSpec((bm, BN), lambda i, j: (i, _snake(i, j, nj)))] + [s[1] for s in cast_specs],
        out_shape=[jax.ShapeDtypeStruct((m, D_FF), BF16)] + [s[2] for s in cast_specs],
        scratch_shapes=scratch,
        compiler_params=_params(("arbitrary", "arbitrary"), 56),
        name="ffn_up",
    )(*operands, w_gu, w_gu, *[w for w, _ in cast_weights])


def _mm_rows_body(a_ref, w_ref, r_ref, *refs, scale, gain_use):
    y = r_ref[...] + scale * _dot(a_ref[...], w_ref[...])
    o_ref = refs[1] if gain_use else refs[0]
    if gain_use == "final":
        y = _rms(y, refs[0][...])
    elif gain_use == "next":
        refs[2][...] = _rms(y, refs[0][...]).astype(BF16)
    o_ref[...] = y


ROWS_BM = 512


def _mm_rows(a, w, res, scale, name, gain=None, gain_use=None):
    m, k = a.shape
    n = w.shape[1]
    row_out = pl.BlockSpec((ROWS_BM, n), lambda i: (i, 0))
    out_specs, out_shape = [row_out], [jax.ShapeDtypeStruct((m, n), F32)]
    if gain_use == "next":
        out_specs.append(row_out)
        out_shape.append(jax.ShapeDtypeStruct((m, n), BF16))
    gains = [] if gain is None else [gain]
    out = pl.pallas_call(
        functools.partial(_mm_rows_body, scale=scale, gain_use=gain_use),
        grid=(m // ROWS_BM,),
        in_specs=[
            pl.BlockSpec((ROWS_BM, k), lambda i: (i, 0)),
            pl.BlockSpec((k, n), lambda i: (0, 0), pipeline_mode=pl.Buffered(1)),
            pl.BlockSpec((ROWS_BM, n), lambda i: (i, 0)),
        ] + [pl.BlockSpec((1, n), lambda i: (0, 0))] * len(gains),
        out_specs=out_specs,
        out_shape=out_shape,
        compiler_params=_params(("arbitrary",), 60),
        name=name,
    )(a, w, res, *gains)
    return out if gain_use == "next" else out[0]


def _norm_mm_body(x_ref, g_ref, w_ref, o_ref, h_ref):
    @pl.when(pl.program_id(1) == 0)
    def _():
        h_ref[...] = _rms(x_ref[...], g_ref[...]).astype(BF16)

    o_ref[...] = _dot(h_ref[...], w_ref[...].astype(BF16)).astype(o_ref.dtype)


def _norm_mm(x, gain, w, layer, bm, out_dtype, name):
    m, k = x.shape
    n = w.shape[2]
    return pl.pallas_call(
        _norm_mm_body,
        grid=(m // bm, n // BN),
        in_specs=[
            pl.BlockSpec((bm, k), lambda i, j: (i, 0)),
            pl.BlockSpec((1, k), lambda i, j: (0, 0)),
            pl.BlockSpec((None, k, BN), lambda i, j: (layer, 0, j)),
        ],
        out_specs=pl.BlockSpec((bm, BN), lambda i, j: (i, j)),
        out_shape=jax.ShapeDtypeStruct((m, n), out_dtype),
        scratch_shapes=[pltpu.VMEM((bm, k), BF16)],
        compiler_params=_params(("parallel", "arbitrary"), 40),
        name=name,
    )(x, gain, w)


EVEN_BN = 2 * BN


def _even_uv_body(h_ref, w_ref, o_ref):
    o_ref[...] = _gelu(_dot_nt(h_ref[...], w_ref[...].astype(BF16))).astype(BF16)


def _even_rest_body(h_ref, w_ref, wdt_ref, wo_ref, o_ref, dt_ref, wo_out_ref):
    @pl.when(pl.program_id(1) == 0)
    def _():
        row = lax.broadcasted_iota(jnp.int32, (LANES, 1), 0)
        dt_ref[...] = _dot_nt(h_ref[...], jnp.where(row < SSD_HEADS, wdt_ref[...], 0.0).astype(BF16))

    wo_out_ref[...] = wo_ref[...].astype(BF16)
    o_ref[...] = _dot_nt(h_ref[...], w_ref[...].astype(BF16)).astype(BF16)


def _even_in(h, w_in_t, w_out, layer):
    m = h.shape[0]
    bn = EVEN_BN
    x_spec = pl.BlockSpec((BM, D_MODEL), lambda i, j: (i, 0))

    nj = 2 * D_MODEL // bn
    uv = pl.pallas_call(
        _even_uv_body,
        grid=(m // BM, nj),
        in_specs=[x_spec,
                  pl.BlockSpec((None, bn, D_MODEL), lambda i, j: (layer, _snake(i, j, nj), 0))],
        out_specs=pl.BlockSpec((BM, bn), lambda i, j: (i, _snake(i, j, nj))),
        out_shape=jax.ShapeDtypeStruct((m, 2 * D_MODEL), BF16),
        compiler_params=_params(("arbitrary", "arbitrary"), 56),
        name="even_in_uv",
    )(h, w_in_t)

    n_xbc, n_z = CONV_DIM // bn, D_MODEL // bn
    nr = n_xbc + n_z
    z_first, xbc_first = 2 * D_MODEL // bn, 3 * D_MODEL // bn

    def w_block(i, j):
        jc = _snake(i, j, nr)
        return jnp.where(jc < n_xbc, xbc_first + jc, z_first + jc - n_xbc)

    wo_in_spec, wo_out_spec, wo_shape = _cast_specs(w_out, layer, 128, nr)
    rest, dt_raw, w_out_b = pl.pallas_call(
        _even_rest_body,
        grid=(m // BM, nr),
        in_specs=[x_spec,
                  pl.BlockSpec((None, bn, D_MODEL), lambda i, j: (layer, w_block(i, j), 0)),
                  pl.BlockSpec((None, LANES, D_MODEL), lambda i, j: (layer, EVEN_MAIN // LANES, 0)),
                  wo_in_spec],
        out_specs=[pl.BlockSpec((BM, bn), lambda i, j: (i, _snake(i, j, nr))),
                   pl.BlockSpec((BM, LANES), lambda i, j: (i, 0)),
                   wo_out_spec],
        out_shape=[jax.ShapeDtypeStruct((m, CONV_DIM + D_MODEL), BF16),
                   jax.ShapeDtypeStruct((m, LANES), F32),
                   wo_shape],
        compiler_params=_params(("arbitrary", "arbitrary"), 56),
        name="even_in_rest",
    )(h, w_in_t, w_in_t, w_out)
    return uv, rest, dt_raw, w_out_b


def _rope_table_body(pos_ref, invf_ref, cos_ref, sin_ref):
    ang = pos_ref[...].astype(F32) * invf_ref[...]
    cos_ref[...] = jnp.cos(ang)
    sin_ref[...] = jnp.sin(ang)


def _rope_table(pos_row, invf_col):
    shape = jax.ShapeDtypeStruct((ROT_HALF, pos_row.shape[1]), F32)
    return pl.pallas_call(_rope_table_body, out_shape=[shape, shape], name="rope_table")(pos_row, invf_col)


def _qkv_body(h_ref, w_ref, b_ref, cost_ref, sint_ref, wo_ref, o_ref, wo_out_ref,
              cos_ref, sn_ref, sp_ref, *, nj):
    j = pl.program_id(1)

    @pl.when(j == 0)
    def _():
        reps = LANES // ROT_HALF
        cos = jnp.concatenate([cost_ref[...]] * reps, axis=0).T
        sin = jnp.concatenate([sint_ref[...]] * reps, axis=0).T
        lane = lax.broadcasted_iota(jnp.int32, (1, LANES), 1) % HEAD_DIM
        first = lane < ROT_HALF
        second = (lane >= ROT_HALF) & (lane < ROT_DIM)
        cos_ref[...] = jnp.where(first | second, cos, 1.0)
        sn_ref[...] = jnp.where(first, -sin, 0.0)
        sp_ref[...] = jnp.where(second, sin, 0.0)

    wo_out_ref[...] = wo_ref[...].astype(BF16)
    acc = _dot(h_ref[...], w_ref[...]) + b_ref[...]

    def rope(a):
        return (a * cos_ref[...] + pltpu.roll(a, LANES - ROT_HALF, 1) * sn_ref[...]
                + pltpu.roll(a, ROT_HALF, 1) * sp_ref[...])

    col_block = _snake(pl.program_id(0), j, nj)
    tiles = o_ref.shape[1] // LANES

    def kind(tile):
        return "q" if tile < ATT_HEADS * HEAD_DIM // LANES else "k" if tile < (ODD_IN - KV_WIDTH) // LANES else "v"

    for t in range(tiles):
        a = acc[:, t * LANES:(t + 1) * LANES]
        kinds = [kind(jb * tiles + t) for jb in range(nj)]
        roped = rope(a) if set(kinds) != {"v"} else None
        by_kind = {"q": lambda: roped * ATT_SCALE, "k": lambda: roped, "v": lambda: a}
        val = by_kind[kinds[-1]]()
        for jb in range(nj - 2, -1, -1):
            if kinds[jb] != kinds[jb + 1]:
                val = jnp.where(col_block <= jb, by_kind[kinds[jb]](), val)
        o_ref[:, t * LANES:(t + 1) * LANES] = val.astype(BF16)


def _qkv(h, w_b, w_o, layer, b, cos_t, sin_t):
    m = h.shape[0]
    nj = 2
    bn = ODD_IN // nj
    wo_in_spec, wo_out_spec, wo_shape = _cast_specs(w_o, layer, 128, nj)
    return pl.pallas_call(
        functools.partial(_qkv_body, nj=nj),
        grid=(m // BM, nj),
        in_specs=[
            pl.BlockSpec((BM, D_MODEL), lambda i, j: (i, 0)),
            pl.BlockSpec((D_MODEL, bn), lambda i, j: (0, _snake(i, j, nj))),
            pl.BlockSpec((1, bn), lambda i, j: (0, _snake(i, j, nj))),
            pl.BlockSpec((ROT_HALF, BM), lambda i, j: (0, i)),
            pl.BlockSpec((ROT_HALF, BM), lambda i, j: (0, i)),
            wo_in_spec,
        ],
        out_specs=[pl.BlockSpec((BM, bn), lambda i, j: (i, _snake(i, j, nj))), wo_out_spec],
        out_shape=[jax.ShapeDtypeStruct((m, ODD_IN), BF16), wo_shape],
        scratch_shapes=[
            pltpu.VMEM((BM, LANES), F32),
            pltpu.VMEM((BM, LANES), F32),
            pltpu.VMEM((BM, LANES), F32),
        ],
        compiler_params=_params(("arbitrary", "arbitrary"), 56),
        name="qkv_rope",
    )(h, w_b, b, cos_t, sin_t, w_o)


def _split3(x):
    hi = x.astype(BF16)
    r1 = x - hi.astype(F32)
    mid = r1.astype(BF16)
    lo = (r1 - mid.astype(F32)).astype(BF16)
    return hi, mid, lo


def _even_mix_body(u_ref, v_ref, z0_ref, z1_ref, xbc_ref, dt_ref, lng_ref, lnb_ref, ws_ref, bs_ref, cw_ref, cb_ref,
                   dtb_ref, alog_ref, dskip_ref, snorm_ref, e3_ref, shift_ref, o_ref, state_ref, xx_ref):
    c = pl.program_id(0)
    q = CHUNK

    @pl.when(c == 0)
    def _():
        state_ref[...] = jnp.zeros_like(state_ref)
        xx_ref[pl.ds(0, q), :] = jnp.zeros((q, CONV_DIM), BF16)

    row = lax.broadcasted_iota(jnp.int32, (q, q), 0)
    col = lax.broadcasted_iota(jnp.int32, (q, q), 1)
    causal = col <= row

    for g in range(GROUPS):
        seg = slice(g * GDIM, (g + 1) * GDIM)
        vg = v_ref[:, seg].astype(F32)
        mu = jnp.mean(vg, axis=-1, keepdims=True)
        d = vg - mu
        var = jnp.mean(d * d, axis=-1, keepdims=True)
        vn = d * lax.rsqrt(var + EPS) * lng_ref[:, seg] + lnb_ref[:, seg]
        w = jnp.where(causal, ws_ref[g], 0.0).astype(BF16)
        s = _dot(w, vn.astype(BF16)) + bs_ref[:, g:g + 1]
        o_ref[:, seg] = (u_ref[:, seg].astype(F32) * s).astype(BF16)

    x_cur = xbc_ref[...]
    xx_ref[pl.ds(q, q), :] = x_cur
    shifted = _dot(shift_ref[...], xx_ref[...])
    conv = cb_ref[...] + cw_ref[SSD_CONV - 1:SSD_CONV, :] * x_cur.astype(F32)
    for k in range(SSD_CONV - 1):
        conv = conv + cw_ref[k:k + 1, :] * shifted[k * q:(k + 1) * q]
    xx_ref[pl.ds(q - CONV_TAIL, CONV_TAIL), :] = x_cur[q - CONV_TAIL:, :]
    xbc = _silu(conv)
    xs = xbc[:, :D_MODEL]

    dt = jax.nn.softplus(dt_ref[...] + dtb_ref[...])
    a = dt * (-jnp.exp(alog_ref[...]))
    tri = jnp.where(causal, 1.0, 0.0).astype(BF16)
    a_hi, a_mid, a_lo = _split3(a)
    acs = _dot(tri, a_hi) + _dot(tri, a_mid) + _dot(tri, a_lo)
    acs_t = acs.T
    both = jnp.concatenate([dt, acs], axis=0)
    b_hi, b_mid, b_lo = _split3(both)
    both_e = _dot(jnp.concatenate([b_hi, b_mid, b_lo], axis=1), e3_ref[...])
    dt_e = both_e[:q]
    acs_e = both_e[q:]
    last_e = acs_e[q - 1:q, :]
    xdt = xs * dt_e
    xdec = (xdt * jnp.exp(last_e - acs_e)).astype(BF16)
    xdt_b = xdt.astype(BF16)
    grow_e = jnp.exp(acs_e)
    chunk_decay = jnp.exp(last_e)

    lane = lax.broadcasted_iota(jnp.int32, (q, LANES), 1)
    lo_half = lane < HEAD_DIM
    gate = _silu(jnp.concatenate([z0_ref[...], z1_ref[...]], axis=1).astype(F32))

    for g in range(GROUPS):
        seg = slice(g * GDIM, (g + 1) * GDIM)
        b_g = xbc[:, D_MODEL + g * SSD_STATE:D_MODEL + (g + 1) * SSD_STATE].astype(BF16)
        c_g = xbc[:, D_MODEL + GROUPS * SSD_STATE + g * SSD_STATE:
                  D_MODEL + GROUPS * SSD_STATE + (g + 1) * SSD_STATE].astype(BF16)
        cb_causal = jnp.where(causal, _dot_nt(c_g, b_g), 0.0)
        y_off = _dot(c_g, state_ref[g].astype(BF16)) * grow_e[:, seg]
        pieces = []
        for p in range(GDIM // LANES):
            mats = []
            for hh in range(2):
                h = g * (GDIM // HEAD_DIM) + 2 * p + hh
                seg_ij = jnp.minimum(acs[:, h:h + 1] - acs_t[h:h + 1, :], 0.0)
                mats.append((cb_causal * jnp.exp(seg_ij)).astype(BF16))
            x2 = xdt_b[:, g * GDIM + p * LANES:g * GDIM + (p + 1) * LANES]
            zero = jnp.zeros_like(x2)
            rhs = jnp.concatenate([jnp.where(lo_half, x2, zero), jnp.where(lo_half, zero, x2)], axis=0)
            pieces.append(_dot(jnp.concatenate(mats, axis=1), rhs))
        y_diag = jnp.concatenate(pieces, axis=1)
        new_states = _dot_tn(b_g, xdec[:, seg])
        state_ref[g] = state_ref[g] * chunk_decay[:, seg] + new_states
        y = y_diag + y_off + xs[:, seg] * dskip_ref[:, seg]
        y = y * gate[:, seg]
        y = y * lax.rsqrt(jnp.mean(y * y, axis=-1, keepdims=True) + EPS)
        o_ref[:, D_MODEL + g * GDIM:D_MODEL + (g + 1) * GDIM] = (y * snorm_ref[:, seg]).astype(BF16)


def _even_mix(uv, rest, dt_raw, ln_g, ln_b, ws, bs_t, conv_w, conv_b, dt_bias, a_log, d_skip_e, ssd_norm, e3,
              shift):
    m = uv.shape[0]
    full = lambda shape: pl.BlockSpec(shape, lambda c: (0,) * len(shape))
    z_block = CONV_DIM // EVEN_BN
    return pl.pallas_call(
        _even_mix_body,
        grid=(m // CHUNK,),
        in_specs=[
            pl.BlockSpec((CHUNK, D_MODEL), lambda c: (c, 0)),
            pl.BlockSpec((CHUNK, D_MODEL), lambda c: (c, 1)),
            pl.BlockSpec((CHUNK, EVEN_BN), lambda c: (c, z_block)),
            pl.BlockSpec((CHUNK, EVEN_BN), lambda c: (c, z_block + 1)),
            pl.BlockSpec((CHUNK, CONV_DIM), lambda c: (c, 0)),
            pl.BlockSpec((CHUNK, LANES), lambda c: (c, 0)),
            full((1, D_MODEL)), full((1, D_MODEL)),
            full((GROUPS, CHUNK, CHUNK)), full((CHUNK, GROUPS)),
            full((SSD_CONV, CONV_DIM)), full((1, CONV_DIM)),
            full((1, LANES)), full((1, LANES)),
            full((1, D_MODEL)), full((1, D_MODEL)),
            full((3 * LANES, D_MODEL)),
            full(((SSD_CONV - 1) * CHUNK, 2 * CHUNK)),
        ],
        out_specs=pl.BlockSpec((CHUNK, 2 * D_MODEL), lambda c: (c, 0)),
        out_shape=jax.ShapeDtypeStruct((m, 2 * D_MODEL), BF16),
        scratch_shapes=[
            pltpu.VMEM((GROUPS, SSD_STATE, GDIM), F32),
            pltpu.VMEM((2 * CHUNK, CONV_DIM), BF16),
        ],
        compiler_params=_params(("arbitrary",), 48),
        name="even_mix",
    )(uv, uv, rest, rest, rest, dt_raw, ln_g, ln_b, ws, bs_t, conv_w, conv_b, dt_bias, a_log, d_skip_e, ssd_norm,
      e3, shift)


def _swa_body(sink_ref, q_ref, kv_ref, kvp_ref, o_ref):
    n = pl.program_id(0)
    w = CHUNK
    row = lax.broadcasted_iota(jnp.int32, (w, w), 0)
    col = lax.broadcasted_iota(jnp.int32, (w, w), 1)
    own = col <= row
    lo_half = lax.broadcasted_iota(jnp.int32, (w, LANES), 1) < HEAD_DIM
    prev_bias = jnp.where(n > 0, 0.0, -jnp.inf)

    def head_tiles(ref, base, k):
        t = ref[:, base + (k // 2) * LANES:base + (k // 2 + 1) * LANES].astype(F32)
        r = pltpu.roll(t, HEAD_DIM, 1)
        return (t, r) if k % 2 == 0 else (r, t)

    for k in range(GROUPS):
        k_lo, k_hi = head_tiles(kv_ref, 0, k)
        kp_lo, kp_hi = head_tiles(kvp_ref, 0, k)
        v_lo, v_hi = head_tiles(kv_ref, KV_WIDTH, k)
        vp_lo, vp_hi = head_tiles(kvp_ref, KV_WIDTH, k)
        kk = jnp.where(lo_half, k_lo, k_hi).astype(BF16)
        kkp = jnp.where(lo_half, kp_lo, kp_hi).astype(BF16)
        out = []
        for parity in range(2):
            if parity == 0:
                vv = jnp.concatenate([jnp.where(lo_half, v_lo, 1.0), jnp.where(lo_half, vp_lo, 1.0)], axis=0)
            else:
                vv = jnp.concatenate([jnp.where(lo_half, 1.0, v_hi), jnp.where(lo_half, 1.0, vp_hi)], axis=0)
            lhs = []
            for p in range(ATT_REP // 2):
                q2 = q_ref[:, k * GDIM + p * LANES:k * GDIM + (p + 1) * LANES]
                zero = jnp.zeros_like(q2)
                lhs.append(jnp.where(lo_half, q2, zero) if parity == 0 else jnp.where(lo_half, zero, q2))
            lhs = jnp.concatenate(lhs, axis=0)
            s_own = _dot_nt(lhs, kk)
            s_prev = _dot_nt(lhs, kkp)
            probs, esink = [], []
            for p in range(ATT_REP // 2):
                sink = sink_ref[k * ATT_REP + 2 * p + parity]
                s = jnp.where(own, s_own[p * w:(p + 1) * w], s_prev[p * w:(p + 1) * w] + prev_bias)
                mx = jnp.maximum(jnp.max(s, axis=-1, keepdims=True), sink)
                e = jnp.exp(s - mx)
                probs.append(jnp.concatenate([jnp.where(own, e, 0.0).astype(BF16),
                                              jnp.where(own, 0.0, e).astype(BF16)], axis=1))
                esink.append(jnp.exp(sink - mx))
            o = _dot(jnp.concatenate(probs, axis=0), vv.astype(BF16))
            out.append((o, esink))
        for p in range(ATT_REP // 2):
            o_even = out[0][0][p * w:(p + 1) * w]
            o_odd = out[1][0][p * w:(p + 1) * w]
            num = jnp.where(lo_half, o_even, o_odd)
            den = pltpu.roll(jnp.where(lo_half, o_odd, o_even), HEAD_DIM, 1)
            den = den + jnp.where(lo_half, out[0][1][p], out[1][1][p])
            o_ref[:, k * GDIM + p * LANES:k * GDIM + (p + 1) * LANES] = (num / den).astype(BF16)


def _swa(qkv, sinks):
    m = qkv.shape[0]
    kv_block = ATT_HEADS * HEAD_DIM // (2 * KV_WIDTH)
    return pl.pallas_call(
        _swa_body,
        grid=(m // CHUNK,),
        in_specs=[
            pl.BlockSpec(memory_space=pltpu.SMEM),
            pl.BlockSpec((CHUNK, D_MODEL), lambda n: (n, 0)),
            pl.BlockSpec((CHUNK, 2 * KV_WIDTH), lambda n: (n, kv_block)),
            pl.BlockSpec((CHUNK, 2 * KV_WIDTH), lambda n: (jnp.maximum(n - 1, 0), kv_block)),
        ],
        out_specs=pl.BlockSpec((CHUNK, D_MODEL), lambda n: (n, 0)),
        out_shape=jax.ShapeDtypeStruct((m, D_MODEL), BF16),
        compiler_params=_params(("parallel",), 32),
        name="swa",
    )(sinks, qkv, qkv, qkv)


def _xattn_body(x_ref, g_ref, wq_ref, kv_ref, wo_ref, gn_ref, o_ref, hn_ref, wq_b, wo_b):
    @pl.when(pl.program_id(0) == 0)
    def _():
        wq_b[...] = wq_ref[...].astype(BF16)
        wo_b[...] = wo_ref[...].astype(BF16)

    x = x_ref[...]
    h = _rms(x, g_ref[...]).astype(BF16)
    q = _dot(h, wq_b[...]).astype(BF16)
    outs = []
    for hd in range(X_HEADS):
        seg = slice(hd * X_HEAD_DIM, (hd + 1) * X_HEAD_DIM)
        k = kv_ref[:, seg]
        v = kv_ref[:, X_WIDTH + hd * X_HEAD_DIM:X_WIDTH + (hd + 1) * X_HEAD_DIM]
        s = _dot_nt(q[:, seg], k) * X_SCALE
        e = jnp.exp(s - jnp.max(s, axis=-1, keepdims=True))
        o = _dot(e.astype(BF16), v) * (1.0 / jnp.sum(e, axis=-1, keepdims=True))
        outs.append(o.astype(BF16))
    y = x + _dot(jnp.concatenate(outs, axis=1), wo_b[...])
    o_ref[...] = y
    hn_ref[...] = _rms(y, gn_ref[...]).astype(BF16)


def _xattn(x, gain, w_q, kv, w_o, layer, next_gain):
    m = x.shape[0]
    bm = 512
    row = pl.BlockSpec((bm, D_MODEL), lambda i: (i, 0))
    vec = pl.BlockSpec((1, D_MODEL), lambda i: (0, 0))
    return pl.pallas_call(
        _xattn_body,
        grid=(m // bm,),
        in_specs=[
            row, vec,
            pl.BlockSpec((None, D_MODEL, X_WIDTH), lambda i: (layer, 0, 0)),
            pl.BlockSpec((N_MEM, 2 * X_WIDTH), lambda i: (0, 0)),
            pl.BlockSpec((None, X_WIDTH, D_MODEL), lambda i: (layer, 0, 0)),
            vec,
        ],
        out_specs=[row, row],
        out_shape=[jax.ShapeDtypeStruct((m, D_MODEL), F32), jax.ShapeDtypeStruct((m, D_MODEL), BF16)],
        scratch_shapes=[pltpu.VMEM((D_MODEL, X_WIDTH), BF16), pltpu.VMEM((X_WIDTH, D_MODEL), BF16)],
        compiler_params=_params(("arbitrary",), 48),
        name="xattn",
    )(x, gain, w_q, kv, w_o, next_gain)


def _ffn(x, h, gain, w_gu, w_down, layer, name, more_casts, out_gain, gain_use):
    casts = [(w_down, layer)] + list(more_casts)
    if h is None:
        act, w_down_b, *copies = _ffn_up(x, gain.reshape(1, -1), w_gu, layer, casts)
    else:
        act, w_down_b, *copies = _ffn_up(h, None, w_gu, layer, casts)
    out = _mm_rows(act, w_down_b, x, 0.5, name, out_gain.reshape(1, -1), gain_use)
    y, h_next = out if gain_use == "next" else (out, None)
    return y, h_next, copies


def _pad_lanes(v):
    return jnp.pad(v.reshape(1, -1), ((0, 0), (0, LANES - v.shape[-1])))


def kernel(x, mem, positions, norm_ffn1, w_ffn1_gu, w_ffn1_down, norm_mix, w_in_even, gm_ln_g, gm_ln_b, gm_ws, gm_bs, conv_w, conv_b, dt_bias, a_log, d_skip, ssd_norm, w_out_even, w_qkv, b_qkv, sinks, w_o_odd, norm_xq, norm_mem, w_xq, w_xkv, w_xo, norm_ffn2, w_ffn2_gu, w_ffn2_down, final_norm):
    bsz, seq, d = x.shape
    assert (bsz, seq, d) == (1, SEQ, D_MODEL)
    xr = x.reshape(seq, d)
    memr = mem.reshape(N_MEM, d)
    inv_freq = ROPE_THETA ** (-jnp.arange(0, ROT_DIM, 2, dtype=F32) / ROT_DIM)
    cos_t, sin_t = _rope_table(positions.reshape(1, seq), inv_freq.reshape(ROT_HALF, 1))
    head_of_lane = jnp.arange(D_MODEL, dtype=jnp.int32) // HEAD_DIM
    e1 = (jnp.arange(LANES, dtype=jnp.int32)[:, None] == head_of_lane[None, :]).astype(BF16)
    e3 = jnp.concatenate([e1, e1, e1], axis=0)
    sel_row = jnp.arange((SSD_CONV - 1) * CHUNK, dtype=jnp.int32)[:, None]
    sel_col = jnp.arange(2 * CHUNK, dtype=jnp.int32)[None, :]
    shift = (sel_col == CHUNK + sel_row % CHUNK - (SSD_CONV - 1) + sel_row // CHUNK).astype(BF16)

    w_in_t = jnp.swapaxes(w_in_even, 1, 2)

    w_gu1 = w_ffn1_gu
    h = None
    for i in range(DEPTH):
        j = i // 2
        xr, h, (w_gu2,) = _ffn(xr, h, norm_ffn1[i], w_gu1, w_ffn1_down, i, "ffn1_down",
                               [(w_ffn2_gu, i)], norm_mix[i], "next")
        if i % 2 == 0:
            uv, rest, dt_raw, w_out_b = _even_in(h, w_in_t, w_out_even, j)
            mix = _even_mix(
                uv, rest, dt_raw, gm_ln_g[j].reshape(1, -1), gm_ln_b[j].reshape(1, -1), gm_ws[j], gm_bs[j].T,
                conv_w[j], conv_b[j].reshape(1, -1), _pad_lanes(dt_bias[j]), _pad_lanes(a_log[j]),
                jnp.repeat(d_skip[j], HEAD_DIM).reshape(1, -1), ssd_norm[j].reshape(1, -1), e3, shift)
            xr = _mm_rows(mix, w_out_b, xr, 1.0, "even_out")
        else:
            qkv, w_o_b = _qkv(h, w_qkv_b, w_o_odd, j, b_qkv[j].reshape(1, -1), cos_t, sin_t)
            att = _swa(qkv, sinks[j])
            xr = _mm_rows(att, w_o_b, xr, 1.0, "odd_out")
        kv = _norm_mm(memr, norm_mem[i].reshape(1, -1), w_xkv, i, N_MEM, BF16, "mem_kv")
        xr, h = _xattn(xr, norm_xq[i].reshape(1, -1), w_xq, kv, w_xo, i, norm_ffn2[i].reshape(1, -1))
        last = i + 1 == DEPTH
        ahead = []
        if not last:
            ahead.append((w_ffn1_gu, i + 1))
            if (i + 1) % 2 == 1:
                ahead.append((w_qkv, (i + 1) // 2))
        xr, h, copies = _ffn(xr, h, norm_ffn2[i], w_gu2, w_ffn2_down, i, "ffn2_down", ahead,
                             final_norm if last else norm_ffn1[i + 1], "final" if last else "next")
        if copies:
            w_gu1 = copies[0]
            w_qkv_b = copies[1] if len(copies) > 1 else None
    return xr.reshape(bsz, seq, d)
```

```python
import functools

import jax
import jax.numpy as jnp
from jax import lax
from jax.experimental import pallas as pl
from jax.experimental.pallas import tpu as pltpu

F32 = jnp.float32
BF16 = jnp.bfloat16

D_MODEL = 2048
SEQ = 8192
DEPTH = 2
EPS = 1e-5
N_MEM = 256
D_FF = 5632
CHUNK = 128
GROUPS = 4
GDIM = D_MODEL // GROUPS
HEAD_DIM = 64
SSD_HEADS = 32
SSD_STATE = 128
SSD_CONV = 4
CONV_DIM = D_MODEL + 2 * GROUPS * SSD_STATE
EVEN_MAIN = 2 * D_MODEL + D_MODEL + CONV_DIM
ATT_HEADS = 32
ATT_REP = ATT_HEADS // GROUPS
ATT_SCALE = HEAD_DIM ** -0.5
ROT_DIM = HEAD_DIM // 4
ROT_HALF = ROT_DIM // 2
ROPE_THETA = 500000.0
KV_WIDTH = GROUPS * HEAD_DIM
ODD_IN = (ATT_HEADS + 2 * GROUPS) * HEAD_DIM
X_HEADS = 4
X_HEAD_DIM = 128
X_WIDTH = X_HEADS * X_HEAD_DIM
X_SCALE = X_HEAD_DIM ** -0.5

LANES = 128
SUBLANES = 8
MXU_COLS = 256
CONV_TAIL = 16
BM = 1024
BN = 512
MIB = 1024 * 1024


def _params(semantics, vmem_mib):
    return pltpu.CompilerParams(dimension_semantics=semantics, vmem_limit_bytes=vmem_mib * MIB)


def _rms(x, g):
    ms = jnp.mean(x * x, axis=-1, keepdims=True)
    return x * lax.rsqrt(ms + EPS) * g


def _silu(x):
    return x * jax.nn.sigmoid(x)


def _gelu(x):
    return 0.5 * x * (1.0 + lax.erf(x * (2.0 ** -0.5)))


def _dot(a, b):
    return jnp.dot(a, b, preferred_element_type=F32)


def _dot_nt(a, b):
    return lax.dot_general(a, b, (((1,), (1,)), ((), ())), preferred_element_type=F32)


def _dot_tn(a, b):
    return lax.dot_general(a, b, (((0,), (0,)), ((), ())), preferred_element_type=F32)


def _snake(i, j, nj):
    return jnp.where(i % 2 == 0, j, nj - 1 - j)


def _cast_specs(w, layer, rows, steps_per_row_block):
    _, r, c = w.shape
    n_slabs = pl.cdiv(r, rows)
    slab = lambda i, j: jnp.minimum(i * steps_per_row_block + j, n_slabs - 1)
    return (pl.BlockSpec((None, rows, c), lambda i, j: (layer, slab(i, j), 0)),
            pl.BlockSpec((rows, c), lambda i, j: (slab(i, j), 0)),
            jax.ShapeDtypeStruct((r, c), BF16))


def _slab_rows(w, steps):
    tile = 2 * SUBLANES
    return tile * pl.cdiv(w.shape[1], tile * steps)


def _ffn_up_body(*refs, n_cast, normed_input):
    n_in = 3 if normed_input else 4
    wg_ref, wu_ref = refs[n_in - 2:n_in]
    cast_in, o_ref = refs[n_in:n_in + n_cast], refs[n_in + n_cast]
    cast_out = refs[n_in + n_cast + 1:n_in + 2 * n_cast + 1]
    scratch = refs[n_in + 2 * n_cast + 1:]
    w_ref = scratch[0]
    if normed_input:
        h_ref = refs[0]
    else:
        x_ref, g_ref, h_ref = refs[0], refs[1], scratch[1]

        @pl.when(pl.program_id(1) == 0)
        def _():
            h_ref[...] = _rms(x_ref[...], g_ref[...]).astype(BF16)

    for src, dst in zip(cast_in, cast_out):
        dst[...] = src[...].astype(BF16)
    groups = BN // MXU_COLS
    for c in range(groups):
        w_ref[:, (2 * c) * MXU_COLS:(2 * c + 1) * MXU_COLS] = wg_ref[:, c * MXU_COLS:(c + 1) * MXU_COLS].astype(BF16)
        w_ref[:, (2 * c + 1) * MXU_COLS:(2 * c + 2) * MXU_COLS] = wu_ref[:, c * MXU_COLS:(c + 1) * MXU_COLS].astype(BF16)
    gu = _dot(h_ref[...], w_ref[...])
    for c in range(groups):
        g = gu[:, (2 * c) * MXU_COLS:(2 * c + 1) * MXU_COLS]
        u = gu[:, (2 * c + 1) * MXU_COLS:(2 * c + 2) * MXU_COLS]
        o_ref[:, c * MXU_COLS:(c + 1) * MXU_COLS] = (_silu(g) * u).astype(BF16)


def _ffn_up(xh, gain, w_gu, layer, cast_weights):
    m = xh.shape[0]
    normed_input = gain is None
    bm = 2 * BM if normed_input else BM
    nj = D_FF // BN
    steps = (m // bm) * nj
    if w_gu.ndim == 3:
        w_spec = lambda off: pl.BlockSpec((None, D_MODEL, BN), lambda i, j: (layer, 0, _snake(i, j, nj) + off))
    else:
        w_spec = lambda off: pl.BlockSpec((D_MODEL, BN), lambda i, j: (0, _snake(i, j, nj) + off))
    cast_specs = [_cast_specs(w, l, _slab_rows(w, steps), nj) for w, l in cast_weights]
    row_specs = [pl.BlockSpec((bm, D_MODEL), lambda i, j: (i, 0))]
    scratch = [pltpu.VMEM((D_MODEL, 2 * BN), BF16)]
    operands = [xh]
    if not normed_input:
        row_specs.append(pl.BlockSpec((1, D_MODEL), lambda i, j: (0, 0)))
        scratch.append(pltpu.VMEM((bm, D_MODEL), BF16))
        operands.append(gain)
    return pl.pallas_call(
        functools.partial(_ffn_up_body, n_cast=len(cast_weights), normed_input=normed_input),
        grid=(m // bm, nj),
        in_specs=row_specs + [w_spec(0), w_spec(nj)] + [s[0] for s in cast_specs],
        out_specs=[pl.BlockSpec((bm, BN), lambda i, j: (i, _snake(i, j, nj)))] + [s[1] for s in cast_specs],
        out_shape=[jax.ShapeDtypeStruct((m, D_FF), BF16)] + [s[2] for s in cast_specs],
        scratch_shapes=scratch,
        compiler_params=_params(("arbitrary", "arbitrary"), 56),
        name="ffn_up",
    )(*operands, w_gu, w_gu, *[w for w, _ in cast_weights])


def _mm_rows_body(a_ref, w_hbm, r_ref, *refs, scale, gain_use):
    w_ref, sem = refs[-2:]
    refs = refs[:-2]
    k = w_ref.shape[0]
    kq = k // W_PARTS

    def part_copy(p):
        rows = pl.ds(p * kq, kq)
        return pltpu.make_async_copy(w_hbm.at[rows, :], w_ref.at[rows, :], sem.at[p])

    o_ref = refs[1] if gain_use else refs[0]

    def finish(y):
        if gain_use == "final":
            y = _rms(y, refs[0][...])
        elif gain_use == "next":
            refs[2][...] = _rms(y, refs[0][...]).astype(BF16)
        o_ref[...] = y

    @pl.when(pl.program_id(0) == 0)
    def _():
        for p in range(W_PARTS):
            part_copy(p).start()
        o_ref[...] = r_ref[...]
        for p in range(W_PARTS):
            part_copy(p).wait()
            o_ref[...] += scale * _dot(a_ref[:, p * kq:(p + 1) * kq], w_ref[p * kq:(p + 1) * kq, :])
        if gain_use:
            finish(o_ref[...])

    @pl.when(pl.program_id(0) > 0)
    def _():
        finish(r_ref[...] + scale * _dot(a_ref[...], w_ref[...]))


ROWS_BM = 512
W_PARTS = 4


def _mm_rows(a, w, res, scale, name, gain=None, gain_use=None):
    m, k = a.shape
    n = w.shape[1]
    assert k % (W_PARTS * LANES) == 0
    row_out = pl.BlockSpec((ROWS_BM, n), lambda i: (i, 0))
    out_specs, out_shape = [row_out], [jax.ShapeDtypeStruct((m, n), F32)]
    if gain_use == "next":
        out_specs.append(row_out)
        out_shape.append(jax.ShapeDtypeStruct((m, n), BF16))
    gains = [] if gain is None else [gain]
    out = pl.pallas_call(
        functools.partial(_mm_rows_body, scale=scale, gain_use=gain_use),
        grid=(m // ROWS_BM,),
        in_specs=[
            pl.BlockSpec((ROWS_BM, k), lambda i: (i, 0)),
            pl.BlockSpec(memory_space=pl.ANY),
            pl.BlockSpec((ROWS_BM, n), lambda i: (i, 0)),
        ] + [pl.BlockSpec((1, n), lambda i: (0, 0))] * len(gains),
        out_specs=out_specs,
        out_shape=out_shape,
        scratch_shapes=[pltpu.VMEM((k, n), BF16), pltpu.SemaphoreType.DMA((W_PARTS,))],
        compiler_params=_params(("arbitrary",), 60),
        name=name,
    )(a, w, res, *gains)
    return out if gain_use == "next" else out[0]


def _norm_mm_body(x_ref, g_ref, w_ref, o_ref, h_ref):
    @pl.when(pl.program_id(1) == 0)
    def _():
        h_ref[...] = _rms(x_ref[...], g_ref[...]).astype(BF16)

    o_ref[...] = _dot(h_ref[...], w_ref[...].astype(BF16)).astype(o_ref.dtype)


def _norm_mm(x, gain, w, layer, bm, out_dtype, name):
    m, k = x.shape
    n = w.shape[2]
    return pl.pallas_call(
        _norm_mm_body,
        grid=(m // bm, n // BN),
        in_specs=[
            pl.BlockSpec((bm, k), lambda i, j: (i, 0)),
            pl.BlockSpec((1, k), lambda i, j: (0, 0)),
            pl.BlockSpec((None, k, BN), lambda i, j: (layer, 0, j)),
        ],
        out_specs=pl.BlockSpec((bm, BN), lambda i, j: (i, j)),
        out_shape=jax.ShapeDtypeStruct((m, n), out_dtype),
        scratch_shapes=[pltpu.VMEM((bm, k), BF16)],
        compiler_params=_params(("parallel", "arbitrary"), 40),
        name=name,
    )(x, gain, w)


EVEN_BN = 2 * BN


def _even_uv_body(h_ref, w_ref, o_ref):
    o_ref[...] = _gelu(_dot_nt(h_ref[...], w_ref[...].astype(BF16))).astype(BF16)


def _even_rest_body(h_ref, w_ref, wdt_ref, wo_ref, o_ref, dt_ref, wo_out_ref):
    @pl.when(pl.program_id(1) == 0)
    def _():
        row = lax.broadcasted_iota(jnp.int32, (LANES, 1), 0)
        dt_ref[...] = _dot_nt(h_ref[...], jnp.where(row < SSD_HEADS, wdt_ref[...], 0.0).astype(BF16))

    wo_out_ref[...] = wo_ref[...].astype(BF16)
    o_ref[...] = _dot_nt(h_ref[...], w_ref[...].astype(BF16)).astype(BF16)


def _even_in(h, w_in_t, w_out, layer):
    m = h.shape[0]
    bn = EVEN_BN
    bm = BM
    x_spec = pl.BlockSpec((bm, D_MODEL), lambda i, j: (i, 0))

    nj = 2 * D_MODEL // bn
    uv = pl.pallas_call(
        _even_uv_body,
        grid=(m // bm, nj),
        in_specs=[x_spec,
                  pl.BlockSpec((None, bn, D_MODEL), lambda i, j: (layer, _snake(i, j, nj), 0))],
        out_specs=pl.BlockSpec((bm, bn), lambda i, j: (i, _snake(i, j, nj))),
        out_shape=jax.ShapeDtypeStruct((m, 2 * D_MODEL), BF16),
        compiler_params=_params(("arbitrary", "arbitrary"), 56),
        name="even_in_uv",
    )(h, w_in_t)

    n_xbc, n_z = CONV_DIM // bn, D_MODEL // bn
    nr = n_xbc + n_z
    z_first, xbc_first = 2 * D_MODEL // bn, 3 * D_MODEL // bn

    def w_block(i, j):
        jc = _snake(i, j, nr)
        return jnp.where(jc < n_xbc, xbc_first + jc, z_first + jc - n_xbc)

    wo_in_spec, wo_out_spec, wo_shape = _cast_specs(w_out, layer, _slab_rows(w_out, (m // bm) * nr), nr)
    rest, dt_raw, w_out_b = pl.pallas_call(
        _even_rest_body,
        grid=(m // bm, nr),
        in_specs=[x_spec,
                  pl.BlockSpec((None, bn, D_MODEL), lambda i, j: (layer, w_block(i, j), 0)),
                  pl.BlockSpec((None, LANES, D_MODEL), lambda i, j: (layer, EVEN_MAIN // LANES, 0)),
                  wo_in_spec],
        out_specs=[pl.BlockSpec((bm, bn), lambda i, j: (i, _snake(i, j, nr))),
                   pl.BlockSpec((bm, LANES), lambda i, j: (i, 0)),
                   wo_out_spec],
        out_shape=[jax.ShapeDtypeStruct((m, CONV_DIM + D_MODEL), BF16),
                   jax.ShapeDtypeStruct((m, LANES), F32),
                   wo_shape],
        compiler_params=_params(("arbitrary", "arbitrary"), 56),
        name="even_in_rest",
    )(h, w_in_t, w_in_t, w_out)
    return uv, rest, dt_raw, w_out_b


def _rope_table_body(pos_ref, invf_ref, cos_ref, sin_ref):
    ang = pos_ref[...].astype(F32) * invf_ref[...]
    cos_ref[...] = jnp.cos(ang)
    sin_ref[...] = jnp.sin(ang)


def _rope_table(pos_row, invf_col):
    shape = jax.ShapeDtypeStruct((ROT_HALF, pos_row.shape[1]), F32)
    return pl.pallas_call(_rope_table_body, out_shape=[shape, shape], name="rope_table")(pos_row, invf_col)


def _qkv_body(h_ref, w_ref, b_ref, cost_ref, sint_ref, wo_ref, o_ref, wo_out_ref,
              cos_ref, sn_ref, sp_ref, *, nj):
    j = pl.program_id(1)

    @pl.when(j == 0)
    def _():
        reps = LANES // ROT_HALF
        cos = jnp.concatenate([cost_ref[...]] * reps, axis=0).T
        sin = jnp.concatenate([sint_ref[...]] * reps, axis=0).T
        lane = lax.broadcasted_iota(jnp.int32, (1, LANES), 1) % HEAD_DIM
        first = lane < ROT_HALF
        second = (lane >= ROT_HALF) & (lane < ROT_DIM)
        cos_ref[...] = jnp.where(first | second, cos, 1.0)
        sn_ref[...] = jnp.where(first, -sin, 0.0)
        sp_ref[...] = jnp.where(second, sin, 0.0)

    wo_out_ref[...] = wo_ref[...].astype(BF16)
    acc = _dot(h_ref[...], w_ref[...]) + b_ref[...]

    def rope(a):
        return (a * cos_ref[...] + pltpu.roll(a, LANES - ROT_HALF, 1) * sn_ref[...]
                + pltpu.roll(a, ROT_HALF, 1) * sp_ref[...])

    col_block = _snake(pl.program_id(0), j, nj)
    tiles = o_ref.shape[1] // LANES

    def kind(tile):
        return "q" if tile < ATT_HEADS * HEAD_DIM // LANES else "k" if tile < (ODD_IN - KV_WIDTH) // LANES else "v"

    for t in range(tiles):
        a = acc[:, t * LANES:(t + 1) * LANES]
        kinds = [kind(jb * tiles + t) for jb in range(nj)]
        roped = rope(a) if set(kinds) != {"v"} else None
        by_kind = {"q": lambda: roped * ATT_SCALE, "k": lambda: roped, "v": lambda: a}
        val = by_kind[kinds[-1]]()
        for jb in range(nj - 2, -1, -1):
            if kinds[jb] != kinds[jb + 1]:
                val = jnp.where(col_block <= jb, by_kind[kinds[jb]](), val)
        o_ref[:, t * LANES:(t + 1) * LANES] = val.astype(BF16)


def _qkv(h, w_b, w_o, layer, b, cos_t, sin_t):
    m = h.shape[0]
    nj = 2
    bn = ODD_IN // nj
    wo_in_spec, wo_out_spec, wo_shape = _cast_specs(w_o, layer, 128, nj)
    return pl.pallas_call(
        functools.partial(_qkv_body, nj=nj),
        grid=(m // BM, nj),
        in_specs=[
            pl.BlockSpec((BM, D_MODEL), lambda i, j: (i, 0)),
            pl.BlockSpec((D_MODEL, bn), lambda i, j: (0, _snake(i, j, nj))),
            pl.BlockSpec((1, bn), lambda i, j: (0, _snake(i, j, nj))),
            pl.BlockSpec((ROT_HALF, BM), lambda i, j: (0, i)),
            pl.BlockSpec((ROT_HALF, BM), lambda i, j: (0, i)),
            wo_in_spec,
        ],
        out_specs=[pl.BlockSpec((BM, bn), lambda i, j: (i, _snake(i, j, nj))), wo_out_spec],
        out_shape=[jax.ShapeDtypeStruct((m, ODD_IN), BF16), wo_shape],
        scratch_shapes=[
            pltpu.VMEM((BM, LANES), F32),
            pltpu.VMEM((BM, LANES), F32),
            pltpu.VMEM((BM, LANES), F32),
        ],
        compiler_params=_params(("arbitrary", "arbitrary"), 56),
        name="qkv_rope",
    )(h, w_b, b, cos_t, sin_t, w_o)


def _split3(x):
    hi = x.astype(BF16)
    r1 = x - hi.astype(F32)
    mid = r1.astype(BF16)
    lo = (r1 - mid.astype(F32)).astype(BF16)
    return hi, mid, lo


def _even_mix_body(u_ref, v_ref, z0_ref, z1_ref, xbc_ref, dt_ref, lng_ref, lnb_ref, ws_ref, bs_ref, cw_ref, cb_ref,
                   dtb_ref, alog_ref, dskip_ref, snorm_ref, e3_ref, shift_ref, o_ref, state_ref, xx_ref):
    c = pl.program_id(0)
    q = CHUNK

    @pl.when(c == 0)
    def _():
        state_ref[...] = jnp.zeros_like(state_ref)
        xx_ref[pl.ds(0, q), :] = jnp.zeros((q, CONV_DIM), BF16)

    row = lax.broadcasted_iota(jnp.int32, (q, q), 0)
    col = lax.broadcasted_iota(jnp.int32, (q, q), 1)
    causal = col <= row

    for g in range(GROUPS):
        seg = slice(g * GDIM, (g + 1) * GDIM)
        vg = v_ref[:, seg].astype(F32)
        mu = jnp.mean(vg, axis=-1, keepdims=True)
        d = vg - mu
        var = jnp.mean(d * d, axis=-1, keepdims=True)
        vn = d * lax.rsqrt(var + EPS) * lng_ref[:, seg] + lnb_ref[:, seg]
        w = jnp.where(causal, ws_ref[g], 0.0).astype(BF16)
        s = _dot(w, vn.astype(BF16)) + bs_ref[:, g:g + 1]
        o_ref[:, seg] = (u_ref[:, seg].astype(F32) * s).astype(BF16)

    x_cur = xbc_ref[...]
    xx_ref[pl.ds(q, q), :] = x_cur
    shifted = _dot(shift_ref[...], xx_ref[...])
    conv = cb_ref[...] + cw_ref[SSD_CONV - 1:SSD_CONV, :] * x_cur.astype(F32)
    for k in range(SSD_CONV - 1):
        conv = conv + cw_ref[k:k + 1, :] * shifted[k * q:(k + 1) * q]
    xx_ref[pl.ds(q - CONV_TAIL, CONV_TAIL), :] = x_cur[q - CONV_TAIL:, :]
    xbc = _silu(conv)
    xs = xbc[:, :D_MODEL]

    dt = jax.nn.softplus(dt_ref[...] + dtb_ref[...])
    a = dt * (-jnp.exp(alog_ref[...]))
    tri = jnp.where(causal, 1.0, 0.0).astype(BF16)
    a_hi, a_mid, a_lo = _split3(a)
    acs = _dot(tri, a_hi) + _dot(tri, a_mid) + _dot(tri, a_lo)
    acs_t = acs.T
    both = jnp.concatenate([dt, acs], axis=0)
    b_hi, b_mid, b_lo = _split3(both)
    both_e = _dot(jnp.concatenate([b_hi, b_mid, b_lo], axis=1), e3_ref[...])
    dt_e = both_e[:q]
    acs_e = both_e[q:]
    last_e = acs_e[q - 1:q, :]
    xdt = xs * dt_e
    xdec = (xdt * jnp.exp(last_e - acs_e)).astype(BF16)
    xdt_b = xdt.astype(BF16)
    grow_e = jnp.exp(acs_e)
    chunk_decay = jnp.exp(last_e)

    lane = lax.broadcasted_iota(jnp.int32, (q, LANES), 1)
    lo_half = lane < HEAD_DIM
    gate = _silu(jnp.concatenate([z0_ref[...], z1_ref[...]], axis=1).astype(F32))

    for g in range(GROUPS):
        seg = slice(g * GDIM, (g + 1) * GDIM)
        b_g = xbc[:, D_MODEL + g * SSD_STATE:D_MODEL + (g + 1) * SSD_STATE].astype(BF16)
        c_g = xbc[:, D_MODEL + GROUPS * SSD_STATE + g * SSD_STATE:
                  D_MODEL + GROUPS * SSD_STATE + (g + 1) * SSD_STATE].astype(BF16)
        cb_causal = jnp.where(causal, _dot_nt(c_g, b_g), 0.0)
        y_off = _dot(c_g, state_ref[g].astype(BF16)) * grow_e[:, seg]
        pieces = []
        for p in range(GDIM // LANES):
            mats = []
            for hh in range(2):
                h = g * (GDIM // HEAD_DIM) + 2 * p + hh
                seg_ij = jnp.minimum(acs[:, h:h + 1] - acs_t[h:h + 1, :], 0.0)
                mats.append((cb_causal * jnp.exp(seg_ij)).astype(BF16))
            x2 = xdt_b[:, g * GDIM + p * LANES:g * GDIM + (p + 1) * LANES]
            zero = jnp.zeros_like(x2)
            rhs = jnp.concatenate([jnp.where(lo_half, x2, zero), jnp.where(lo_half, zero, x2)], axis=0)
            pieces.append(_dot(jnp.concatenate(mats, axis=1), rhs))
        y_diag = jnp.concatenate(pieces, axis=1)
        new_states = _dot_tn(b_g, xdec[:, seg])
        state_ref[g] = state_ref[g] * chunk_decay[:, seg] + new_states
        y = y_diag + y_off + xs[:, seg] * dskip_ref[:, seg]
        y = y * gate[:, seg]
        y = y * lax.rsqrt(jnp.mean(y * y, axis=-1, keepdims=True) + EPS)
        o_ref[:, D_MODEL + g * GDIM:D_MODEL + (g + 1) * GDIM] = (y * snorm_ref[:, seg]).astype(BF16)


def _even_mix(uv, rest, dt_raw, ln_g, ln_b, ws, bs_t, conv_w, conv_b, dt_bias, a_log, d_skip_e, ssd_norm, e3,
              shift):
    m = uv.shape[0]
    full = lambda shape: pl.BlockSpec(shape, lambda c: (0,) * len(shape))
    z_block = CONV_DIM // EVEN_BN
    return pl.pallas_call(
        _even_mix_body,
        grid=(m // CHUNK,),
        in_specs=[
            pl.BlockSpec((CHUNK, D_MODEL), lambda c: (c, 0)),
            pl.BlockSpec((CHUNK, D_MODEL), lambda c: (c, 1)),
            pl.BlockSpec((CHUNK, EVEN_BN), lambda c: (c, z_block)),
            pl.BlockSpec((CHUNK, EVEN_BN), lambda c: (c, z_block + 1)),
            pl.BlockSpec((CHUNK, CONV_DIM), lambda c: (c, 0)),
            pl.BlockSpec((CHUNK, LANES), lambda c: (c, 0)),
            full((1, D_MODEL)), full((1, D_MODEL)),
            full((GROUPS, CHUNK, CHUNK)), full((CHUNK, GROUPS)),
            full((SSD_CONV, CONV_DIM)), full((1, CONV_DIM)),
            full((1, LANES)), full((1, LANES)),
            full((1, D_MODEL)), full((1, D_MODEL)),
            full((3 * LANES, D_MODEL)),
            full(((SSD_CONV - 1) * CHUNK, 2 * CHUNK)),
        ],
        out_specs=pl.BlockSpec((CHUNK, 2 * D_MODEL), lambda c: (c, 0)),
        out_shape=jax.ShapeDtypeStruct((m, 2 * D_MODEL), BF16),
        scratch_shapes=[
            pltpu.VMEM((GROUPS, SSD_STATE, GDIM), F32),
            pltpu.VMEM((2 * CHUNK, CONV_DIM), BF16),
        ],
        compiler_params=_params(("arbitrary",), 48),
        name="even_mix",
    )(uv, uv, rest, rest, rest, dt_raw, ln_g, ln_b, ws, bs_t, conv_w, conv_b, dt_bias, a_log, d_skip_e, ssd_norm,
      e3, shift)


def _swa_body(sink_ref, q_ref, kv_ref, kvp_ref, o_ref):
    n = pl.program_id(0)
    w = CHUNK
    row = lax.broadcasted_iota(jnp.int32, (w, w), 0)
    col = lax.broadcasted_iota(jnp.int32, (w, w), 1)
    own = col <= row
    lo_half = lax.broadcasted_iota(jnp.int32, (w, LANES), 1) < HEAD_DIM
    prev_bias = jnp.where(n > 0, 0.0, -jnp.inf)

    def head_tiles(ref, base, k):
        t = ref[:, base + (k // 2) * LANES:base + (k // 2 + 1) * LANES].astype(F32)
        r = pltpu.roll(t, HEAD_DIM, 1)
        return (t, r) if k % 2 == 0 else (r, t)

    for k in range(GROUPS):
        k_lo, k_hi = head_tiles(kv_ref, 0, k)
        kp_lo, kp_hi = head_tiles(kvp_ref, 0, k)
        v_lo, v_hi = head_tiles(kv_ref, KV_WIDTH, k)
        vp_lo, vp_hi = head_tiles(kvp_ref, KV_WIDTH, k)
        kk = jnp.where(lo_half, k_lo, k_hi).astype(BF16)
        kkp = jnp.where(lo_half, kp_lo, kp_hi).astype(BF16)
        out = []
        for parity in range(2):
            if parity == 0:
                vv = jnp.concatenate([jnp.where(lo_half, v_lo, 1.0), jnp.where(lo_half, vp_lo, 1.0)], axis=0)
            else:
                vv = jnp.concatenate([jnp.where(lo_half, 1.0, v_hi), jnp.where(lo_half, 1.0, vp_hi)], axis=0)
            lhs = []
            for p in range(ATT_REP // 2):
                q2 = q_ref[:, k * GDIM + p * LANES:k * GDIM + (p + 1) * LANES]
                zero = jnp.zeros_like(q2)
                lhs.append(jnp.where(lo_half, q2, zero) if parity == 0 else jnp.where(lo_half, zero, q2))
            lhs = jnp.concatenate(lhs, axis=0)
            s_own = _dot_nt(lhs, kk)
            s_prev = _dot_nt(lhs, kkp)
            probs, esink = [], []
            for p in range(ATT_REP // 2):
                sink = sink_ref[k * ATT_REP + 2 * p + parity]
                s = jnp.where(own, s_own[p * w:(p + 1) * w], s_prev[p * w:(p + 1) * w] + prev_bias)
                mx = jnp.maximum(jnp.max(s, axis=-1, keepdims=True), sink)
                e = jnp.exp(s - mx)
                probs.append(jnp.concatenate([jnp.where(own, e, 0.0).astype(BF16),
                                              jnp.where(own, 0.0, e).astype(BF16)], axis=1))
                esink.append(jnp.exp(sink - mx))
            o = _dot(jnp.concatenate(probs, axis=0), vv.astype(BF16))
            out.append((o, esink))
        for p in range(ATT_REP // 2):
            o_even = out[0][0][p * w:(p + 1) * w]
            o_odd = out[1][0][p * w:(p + 1) * w]
            num = jnp.where(lo_half, o_even, o_odd)
            den = pltpu.roll(jnp.where(lo_half, o_odd, o_even), HEAD_DIM, 1)
            den = den + jnp.where(lo_half, out[0][1][p], out[1][1][p])
            o_ref[:, k * GDIM + p * LANES:k * GDIM + (p + 1) * LANES] = (num / den).astype(BF16)


def _swa(qkv, sinks):
    m = qkv.shape[0]
    kv_block = ATT_HEADS * HEAD_DIM // (2 * KV_WIDTH)
    return pl.pallas_call(
        _swa_body,
        grid=(m // CHUNK,),
        in_specs=[
            pl.BlockSpec(memory_space=pltpu.SMEM),
            pl.BlockSpec((CHUNK, D_MODEL), lambda n: (n, 0)),
            pl.BlockSpec((CHUNK, 2 * KV_WIDTH), lambda n: (n, kv_block)),
            pl.BlockSpec((CHUNK, 2 * KV_WIDTH), lambda n: (jnp.maximum(n - 1, 0), kv_block)),
        ],
        out_specs=pl.BlockSpec((CHUNK, D_MODEL), lambda n: (n, 0)),
        out_shape=jax.ShapeDtypeStruct((m, D_MODEL), BF16),
        compiler_params=_params(("parallel",), 32),
        name="swa",
    )(sinks, qkv, qkv, qkv)


def _xattn_body(x_ref, g_ref, wq_ref, kv_ref, wo_ref, gn_ref, o_ref, hn_ref, wq_b, wo_b):
    @pl.when(pl.program_id(0) == 0)
    def _():
        wq_b[...] = wq_ref[...].astype(BF16)
        wo_b[...] = wo_ref[...].astype(BF16)

    x = x_ref[...]
    h = _rms(x, g_ref[...]).astype(BF16)
    q = _dot(h, wq_b[...]).astype(BF16)
    outs = []
    for hd in range(X_HEADS):
        seg = slice(hd * X_HEAD_DIM, (hd + 1) * X_HEAD_DIM)
        k = kv_ref[:, seg]
        v = kv_ref[:, X_WIDTH + hd * X_HEAD_DIM:X_WIDTH + (hd + 1) * X_HEAD_DIM]
        s = _dot_nt(q[:, seg], k) * X_SCALE
        e = jnp.exp(s - jnp.max(s, axis=-1, keepdims=True))
        o = _dot(e.astype(BF16), v) * (1.0 / jnp.sum(e, axis=-1, keepdims=True))
        outs.append(o.astype(BF16))
    y = x + _dot(jnp.concatenate(outs, axis=1), wo_b[...])
    o_ref[...] = y
    hn_ref[...] = _rms(y, gn_ref[...]).astype(BF16)


def _xattn(x, gain, w_q, kv, w_o, layer, next_gain):
    m = x.shape[0]
    bm = 512
    row = pl.BlockSpec((bm, D_MODEL), lambda i: (i, 0))
    vec = pl.BlockSpec((1, D_MODEL), lambda i: (0, 0))
    return pl.pallas_call(
        _xattn_body,
        grid=(m // bm,),
        in_specs=[
            row, vec,
            pl.BlockSpec((None, D_MODEL, X_WIDTH), lambda i: (layer, 0, 0)),
            pl.BlockSpec((N_MEM, 2 * X_WIDTH), lambda i: (0, 0)),
            pl.BlockSpec((None, X_WIDTH, D_MODEL), lambda i: (layer, 0, 0)),
            vec,
        ],
        out_specs=[row, row],
        out_shape=[jax.ShapeDtypeStruct((m, D_MODEL), F32), jax.ShapeDtypeStruct((m, D_MODEL), BF16)],
        scratch_shapes=[pltpu.VMEM((D_MODEL, X_WIDTH), BF16), pltpu.VMEM((X_WIDTH, D_MODEL), BF16)],
        compiler_params=_params(("arbitrary",), 48),
        name="xattn",
    )(x, gain, w_q, kv, w_o, next_gain)


def _ffn(x, h, gain, w_gu, w_down, layer, name, more_casts, out_gain, gain_use):
    casts = [(w_down, layer)] + list(more_casts)
    if h is None:
        act, w_down_b, *copies = _ffn_up(x, gain.reshape(1, -1), w_gu, layer, casts)
    else:
        act, w_down_b, *copies = _ffn_up(h, None, w_gu, layer, casts)
    out = _mm_rows(act, w_down_b, x, 0.5, name, out_gain.reshape(1, -1), gain_use)
    y, h_next = out if gain_use == "next" else (out, None)
    return y, h_next, copies


def _pad_lanes(v):
    return jnp.pad(v.reshape(1, -1), ((0, 0), (0, LANES - v.shape[-1])))


def kernel(x, mem, positions, norm_ffn1, w_ffn1_gu, w_ffn1_down, norm_mix, w_in_even, gm_ln_g, gm_ln_b, gm_ws, gm_bs, conv_w, conv_b, dt_bias, a_log, d_skip, ssd_norm, w_out_even, w_qkv, b_qkv, sinks, w_o_odd, norm_xq, norm_mem, w_xq, w_xkv, w_xo, norm_ffn2, w_ffn2_gu, w_ffn2_down, final_norm):
    bsz, seq, d = x.shape
    assert (bsz, seq, d) == (1, SEQ, D_MODEL)
    xr = x.reshape(seq, d)
    memr = mem.reshape(N_MEM, d)
    inv_freq = ROPE_THETA ** (-jnp.arange(0, ROT_DIM, 2, dtype=F32) / ROT_DIM)
    cos_t, sin_t = _rope_table(positions.reshape(1, seq), inv_freq.reshape(ROT_HALF, 1))
    head_of_lane = jnp.arange(D_MODEL, dtype=jnp.int32) // HEAD_DIM
    e1 = (jnp.arange(LANES, dtype=jnp.int32)[:, None] == head_of_lane[None, :]).astype(BF16)
    e3 = jnp.concatenate([e1, e1, e1], axis=0)
    sel_row = jnp.arange((SSD_CONV - 1) * CHUNK, dtype=jnp.int32)[:, None]
    sel_col = jnp.arange(2 * CHUNK, dtype=jnp.int32)[None, :]
    shift = (sel_col == CHUNK + sel_row % CHUNK - (SSD_CONV - 1) + sel_row // CHUNK).astype(BF16)

    w_in_t = jnp.swapaxes(w_in_even, 1, 2)

    w_gu1 = w_ffn1_gu
    h = None
    for i in range(DEPTH):
        j = i // 2
        xr, h, (w_gu2,) = _ffn(xr, h, norm_ffn1[i], w_gu1, w_ffn1_down, i, "ffn1_down",
                               [(w_ffn2_gu, i)], norm_mix[i], "next")
        if i % 2 == 0:
            uv, rest, dt_raw, w_out_b = _even_in(h, w_in_t, w_out_even, j)
            mix = _even_mix(
                uv, rest, dt_raw, gm_ln_g[j].reshape(1, -1), gm_ln_b[j].reshape(1, -1), gm_ws[j], gm_bs[j].T,
                conv_w[j], conv_b[j].reshape(1, -1), _pad_lanes(dt_bias[j]), _pad_lanes(a_log[j]),
                jnp.repeat(d_skip[j], HEAD_DIM).reshape(1, -1), ssd_norm[j].reshape(1, -1), e3, shift)
            xr = _mm_rows(mix, w_out_b, xr, 1.0, "even_out")
        else:
            qkv, w_o_b = _qkv(h, w_qkv_b, w_o_odd, j, b_qkv[j].reshape(1, -1), cos_t, sin_t)
            att = _swa(qkv, sinks[j])
            xr = _mm_rows(att, w_o_b, xr, 1.0, "odd_out")
        kv = _norm_mm(memr, norm_mem[i].reshape(1, -1), w_xkv, i, N_MEM, BF16, "mem_kv")
        xr, h = _xattn(xr, norm_xq[i].reshape(1, -1), w_xq, kv, w_xo, i, norm_ffn2[i].reshape(1, -1))
        last = i + 1 == DEPTH
        ahead = []
        if not last:
            ahead.append((w_ffn1_gu, i + 1))
            if (i + 1) % 2 == 1:
                ahead.append((w_qkv, (i + 1) // 2))
        xr, h, copies = _ffn(xr, h, norm_ffn2[i], w_gu2, w_ffn2_down, i, "ffn2_down", ahead,
                             final_norm if last else norm_ffn1[i + 1], "final" if last else "next")
        if copies:
            w_gu1 = copies[0]
            w_qkv_b = copies[1] if len(copies) > 1 else None
    return xr.reshape(bsz, seq, d)
```

```python
import functools

import jax
import jax.numpy as jnp
from jax import lax
from jax.experimental import pallas as pl
from jax.experimental.pallas import tpu as pltpu

F32 = jnp.float32
BF16 = jnp.bfloat16

D_MODEL = 2048
SEQ = 8192
DEPTH = 2
EPS = 1e-5
N_MEM = 256
D_FF = 5632
CHUNK = 128
GROUPS = 4
GDIM = D_MODEL // GROUPS
HEAD_DIM = 64
SSD_HEADS = 32
SSD_STATE = 128
SSD_CONV = 4
CONV_DIM = D_MODEL + 2 * GROUPS * SSD_STATE
EVEN_MAIN = 2 * D_MODEL + D_MODEL + CONV_DIM
ATT_HEADS = 32
ATT_REP = ATT_HEADS // GROUPS
ATT_SCALE = HEAD_DIM ** -0.5
ROT_DIM = HEAD_DIM // 4
ROT_HALF = ROT_DIM // 2
ROPE_THETA = 500000.0
KV_WIDTH = GROUPS * HEAD_DIM
ODD_IN = (ATT_HEADS + 2 * GROUPS) * HEAD_DIM
X_HEADS = 4
X_HEAD_DIM = 128
X_WIDTH = X_HEADS * X_HEAD_DIM
X_SCALE = X_HEAD_DIM ** -0.5

LANES = 128
SUBLANES = 8
MXU_COLS = 256
FFN_SLAB = 1024
CONV_TAIL = 16
BM = 1024
BN = 512
MIB = 1024 * 1024


def _params(semantics, vmem_mib):
    return pltpu.CompilerParams(dimension_semantics=semantics, vmem_limit_bytes=vmem_mib * MIB)


def _rms(x, g):
    ms = jnp.mean(x * x, axis=-1, keepdims=True)
    return x * lax.rsqrt(ms + EPS) * g


def _silu(x):
    return x * jax.nn.sigmoid(x)


def _gelu(x):
    return 0.5 * x * (1.0 + lax.erf(x * (2.0 ** -0.5)))


def _dot(a, b):
    return jnp.dot(a, b, preferred_element_type=F32)


def _dot_nt(a, b):
    return lax.dot_general(a, b, (((1,), (1,)), ((), ())), preferred_element_type=F32)


def _dot_tn(a, b):
    return lax.dot_general(a, b, (((0,), (0,)), ((), ())), preferred_element_type=F32)


def _snake(i, j, nj):
    return jnp.where(i % 2 == 0, j, nj - 1 - j)


def _cast_specs(w, layer, rows, steps_per_row_block):
    _, r, c = w.shape
    n_slabs = pl.cdiv(r, rows)
    slab = lambda i, j: jnp.minimum(i * steps_per_row_block + j, n_slabs - 1)
    return (pl.BlockSpec((None, rows, c), lambda i, j: (layer, slab(i, j), 0)),
            pl.BlockSpec((rows, c), lambda i, j: (slab(i, j), 0)),
            jax.ShapeDtypeStruct((r, c), BF16))


def _slab_rows(w, steps):
    tile = 2 * SUBLANES
    return tile * pl.cdiv(w.shape[1], tile * steps)


def _ffn_up_body(*refs, n_cast, normed_input):
    n_in = 3 if normed_input else 4
    wg_ref, wu_ref = refs[n_in - 2:n_in]
    cast_in, o_ref = refs[n_in:n_in + n_cast], refs[n_in + n_cast]
    cast_out = refs[n_in + n_cast + 1:n_in + 2 * n_cast + 1]
    scratch = refs[n_in + 2 * n_cast + 1:]
    w_ref = scratch[0]
    if normed_input:
        h_ref = refs[0]
    else:
        x_ref, g_ref, h_ref = refs[0], refs[1], scratch[1]

        @pl.when(pl.program_id(1) == 0)
        def _():
            h_ref[...] = _rms(x_ref[...], g_ref[...]).astype(BF16)

    for src, dst in zip(cast_in, cast_out):
        dst[...] = src[...].astype(BF16)
    groups = BN // MXU_COLS
    for c in range(groups):
        w_ref[:, (2 * c) * MXU_COLS:(2 * c + 1) * MXU_COLS] = wg_ref[:, c * MXU_COLS:(c + 1) * MXU_COLS].astype(BF16)
        w_ref[:, (2 * c + 1) * MXU_COLS:(2 * c + 2) * MXU_COLS] = wu_ref[:, c * MXU_COLS:(c + 1) * MXU_COLS].astype(BF16)
    rows = o_ref.shape[0]
    for r0 in range(0, rows, FFN_SLAB):
        gu = _dot(h_ref[r0:r0 + FFN_SLAB, :], w_ref[...])
        for c in range(groups):
            g = gu[:, (2 * c) * MXU_COLS:(2 * c + 1) * MXU_COLS]
            u = gu[:, (2 * c + 1) * MXU_COLS:(2 * c + 2) * MXU_COLS]
            o_ref[r0:r0 + FFN_SLAB, c * MXU_COLS:(c + 1) * MXU_COLS] = (_silu(g) * u).astype(BF16)


def _ffn_up(xh, gain, w_gu, layer, cast_weights):
    m = xh.shape[0]
    normed_input = gain is None
    bm = 2 * BM if normed_input else BM
    nj = D_FF // BN
    steps = (m // bm) * nj
    if w_gu.ndim == 3:
        w_spec = lambda off: pl.BlockSpec((None, D_MODEL, BN), lambda i, j: (layer, 0, _snake(i, j, nj) + off))
    else:
        w_spec = lambda off: pl.BlockSpec((D_MODEL, BN), lambda i, j: (0, _snake(i, j, nj) + off))
    cast_specs = [_cast_specs(w, l, _slab_rows(w, steps), nj) for w, l in cast_weights]
    row_specs = [pl.BlockSpec((bm, D_MODEL), lambda i, j: (i, 0))]
    scratch = [pltpu.VMEM((D_MODEL, 2 * BN), BF16)]
    operands = [xh]
    if not normed_input:
        row_specs.append(pl.BlockSpec((1, D_MODEL), lambda i, j: (0, 0)))
        scratch.append(pltpu.VMEM((bm, D_MODEL), BF16))
        operands.append(gain)
    return pl.pallas_call(
        functools.partial(_ffn_up_body, n_cast=len(cast_weights), normed_input=normed_input),
        grid=(m // bm, nj),
        in_specs=row_specs + [w_spec(0), w_spec(nj)] + [s[0] for s in cast_specs],
        out_specs=[pl.BlockSpec((bm, BN), lambda i, j: (i, _snake(i, j, nj)))] + [s[1] for s in cast_specs],
        out_shape=[jax.ShapeDtypeStruct((m, D_FF), BF16)] + [s[2] for s in cast_specs],
        scratch_shapes=scratch,
        compiler_params=_params(("arbitrary", "arbitrary"), 56),
        name="ffn_up",
    )(*operands, w_gu, w_gu, *[w for w, _ in cast_weights])


def _mm_rows_body(a_ref, w_ref, r_ref, *refs, scale, gain_use):
    o_ref = refs[1] if gain_use else refs[0]
    for rows in _row_slabs(o_ref, ROWS_SLAB):
        y = r_ref[rows, :] + scale * _dot(a_ref[rows, :], w_ref[...])
        if gain_use == "final":
            y = _rms(y, refs[0][...])
        elif gain_use == "next":
            refs[2][rows, :] = _rms(y, refs[0][...]).astype(BF16)
        o_ref[rows, :] = y


ROWS_BM = 512


def _mm_rows(a, w, res, scale, name, gain=None, gain_use=None):
    m, k = a.shape
    n = w.shape[1]
    row_out = pl.BlockSpec((ROWS_BM, n), lambda i: (i, 0))
    out_specs, out_shape = [row_out], [jax.ShapeDtypeStruct((m, n), F32)]
    if gain_use == "next":
        out_specs.append(row_out)
        out_shape.append(jax.ShapeDtypeStruct((m, n), BF16))
    gains = [] if gain is None else [gain]
    out = pl.pallas_call(
        functools.partial(_mm_rows_body, scale=scale, gain_use=gain_use),
        grid=(m // ROWS_BM,),
        in_specs=[
            pl.BlockSpec((ROWS_BM, k), lambda i: (i, 0)),
            pl.BlockSpec((k, n), lambda i: (0, 0), pipeline_mode=pl.Buffered(1)),
            pl.BlockSpec((ROWS_BM, n), lambda i: (i, 0)),
        ] + [pl.BlockSpec((1, n), lambda i: (0, 0))] * len(gains),
        out_specs=out_specs,
        out_shape=out_shape,
        compiler_params=_params(("arbitrary",), 60),
        name=name,
    )(a, w, res, *gains)
    return out if gain_use == "next" else out[0]


def _norm_mm_body(x_ref, g_ref, w_ref, o_ref, h_ref):
    @pl.when(pl.program_id(1) == 0)
    def _():
        h_ref[...] = _rms(x_ref[...], g_ref[...]).astype(BF16)

    o_ref[...] = _dot(h_ref[...], w_ref[...].astype(BF16)).astype(o_ref.dtype)


def _norm_mm(x, gain, w, layer, bm, out_dtype, name):
    m, k = x.shape
    n = w.shape[2]
    return pl.pallas_call(
        _norm_mm_body,
        grid=(m // bm, n // BN),
        in_specs=[
            pl.BlockSpec((bm, k), lambda i, j: (i, 0)),
            pl.BlockSpec((1, k), lambda i, j: (0, 0)),
            pl.BlockSpec((None, k, BN), lambda i, j: (layer, 0, j)),
        ],
        out_specs=pl.BlockSpec((bm, BN), lambda i, j: (i, j)),
        out_shape=jax.ShapeDtypeStruct((m, n), out_dtype),
        scratch_shapes=[pltpu.VMEM((bm, k), BF16)],
        compiler_params=_params(("parallel", "arbitrary"), 40),
        name=name,
    )(x, gain, w)


EVEN_BN = 2 * BN
EVEN_SLAB = 512
QKV_SLAB = 256
ROWS_SLAB = 256


def _row_slabs(ref, slab):
    return [slice(r0, r0 + slab) for r0 in range(0, ref.shape[0], slab)]


def _even_uv_body(h_ref, w_ref, o_ref):
    w = w_ref[...].astype(BF16)
    for rows in _row_slabs(o_ref, EVEN_SLAB):
        o_ref[rows, :] = _gelu(_dot_nt(h_ref[rows, :], w)).astype(BF16)


def _even_rest_body(h_ref, w_ref, wdt_ref, wo_ref, o_ref, dt_ref, wo_out_ref):
    @pl.when(pl.program_id(1) == 0)
    def _():
        row = lax.broadcasted_iota(jnp.int32, (LANES, 1), 0)
        dt_ref[...] = _dot_nt(h_ref[...], jnp.where(row < SSD_HEADS, wdt_ref[...], 0.0).astype(BF16))

    wo_out_ref[...] = wo_ref[...].astype(BF16)
    w = w_ref[...].astype(BF16)
    for rows in _row_slabs(o_ref, EVEN_SLAB):
        o_ref[rows, :] = _dot_nt(h_ref[rows, :], w).astype(BF16)


def _even_in(h, w_in_t, w_out, layer):
    m = h.shape[0]
    bn = EVEN_BN
    x_spec = pl.BlockSpec((BM, D_MODEL), lambda i, j: (i, 0))

    nj = 2 * D_MODEL // bn
    uv = pl.pallas_call(
        _even_uv_body,
        grid=(m // BM, nj),
        in_specs=[x_spec,
                  pl.BlockSpec((None, bn, D_MODEL), lambda i, j: (layer, _snake(i, j, nj), 0))],
        out_specs=pl.BlockSpec((BM, bn), lambda i, j: (i, _snake(i, j, nj))),
        out_shape=jax.ShapeDtypeStruct((m, 2 * D_MODEL), BF16),
        compiler_params=_params(("arbitrary", "arbitrary"), 56),
        name="even_in_uv",
    )(h, w_in_t)

    n_xbc, n_z = CONV_DIM // bn, D_MODEL // bn
    nr = n_xbc + n_z
    z_first, xbc_first = 2 * D_MODEL // bn, 3 * D_MODEL // bn

    def w_block(i, j):
        jc = _snake(i, j, nr)
        return jnp.where(jc < n_xbc, xbc_first + jc, z_first + jc - n_xbc)

    wo_in_spec, wo_out_spec, wo_shape = _cast_specs(w_out, layer, 128, nr)
    rest, dt_raw, w_out_b = pl.pallas_call(
        _even_rest_body,
        grid=(m // BM, nr),
        in_specs=[x_spec,
                  pl.BlockSpec((None, bn, D_MODEL), lambda i, j: (layer, w_block(i, j), 0)),
                  pl.BlockSpec((None, LANES, D_MODEL), lambda i, j: (layer, EVEN_MAIN // LANES, 0)),
                  wo_in_spec],
        out_specs=[pl.BlockSpec((BM, bn), lambda i, j: (i, _snake(i, j, nr))),
                   pl.BlockSpec((BM, LANES), lambda i, j: (i, 0)),
                   wo_out_spec],
        out_shape=[jax.ShapeDtypeStruct((m, CONV_DIM + D_MODEL), BF16),
                   jax.ShapeDtypeStruct((m, LANES), F32),
                   wo_shape],
        compiler_params=_params(("arbitrary", "arbitrary"), 56),
        name="even_in_rest",
    )(h, w_in_t, w_in_t, w_out)
    return uv, rest, dt_raw, w_out_b


def _rope_table_body(pos_ref, invf_ref, cos_ref, sin_ref):
    ang = pos_ref[...].astype(F32) * invf_ref[...]
    cos_ref[...] = jnp.cos(ang)
    sin_ref[...] = jnp.sin(ang)


def _rope_table(pos_row, invf_col):
    shape = jax.ShapeDtypeStruct((ROT_HALF, pos_row.shape[1]), F32)
    return pl.pallas_call(_rope_table_body, out_shape=[shape, shape], name="rope_table")(pos_row, invf_col)


def _qkv_body(h_ref, w_ref, b_ref, cost_ref, sint_ref, wo_ref, o_ref, wo_out_ref,
              cos_ref, sn_ref, sp_ref, *, nj):
    j = pl.program_id(1)

    @pl.when(j == 0)
    def _():
        reps = LANES // ROT_HALF
        cos = jnp.concatenate([cost_ref[...]] * reps, axis=0).T
        sin = jnp.concatenate([sint_ref[...]] * reps, axis=0).T
        lane = lax.broadcasted_iota(jnp.int32, (1, LANES), 1) % HEAD_DIM
        first = lane < ROT_HALF
        second = (lane >= ROT_HALF) & (lane < ROT_DIM)
        cos_ref[...] = jnp.where(first | second, cos, 1.0)
        sn_ref[...] = jnp.where(first, -sin, 0.0)
        sp_ref[...] = jnp.where(second, sin, 0.0)

    wo_out_ref[...] = wo_ref[...].astype(BF16)

    col_block = _snake(pl.program_id(0), j, nj)
    tiles = o_ref.shape[1] // LANES

    def kind(tile):
        return "q" if tile < ATT_HEADS * HEAD_DIM // LANES else "k" if tile < (ODD_IN - KV_WIDTH) // LANES else "v"

    for rows in _row_slabs(o_ref, QKV_SLAB):
        acc = _dot(h_ref[rows, :], w_ref[...]) + b_ref[...]

        def rope(a):
            return (a * cos_ref[rows, :] + pltpu.roll(a, LANES - ROT_HALF, 1) * sn_ref[rows, :]
                    + pltpu.roll(a, ROT_HALF, 1) * sp_ref[rows, :])

        for t in range(tiles):
            a = acc[:, t * LANES:(t + 1) * LANES]
            kinds = [kind(jb * tiles + t) for jb in range(nj)]
            roped = rope(a) if set(kinds) != {"v"} else None
            by_kind = {"q": lambda: roped * ATT_SCALE, "k": lambda: roped, "v": lambda: a}
            val = by_kind[kinds[-1]]()
            for jb in range(nj - 2, -1, -1):
                if kinds[jb] != kinds[jb + 1]:
                    val = jnp.where(col_block <= jb, by_kind[kinds[jb]](), val)
            o_ref[rows, t * LANES:(t + 1) * LANES] = val.astype(BF16)


def _qkv(h, w_b, w_o, layer, b, cos_t, sin_t):
    m = h.shape[0]
    nj = 2
    bn = ODD_IN // nj
    wo_in_spec, wo_out_spec, wo_shape = _cast_specs(w_o, layer, 128, nj)
    return pl.pallas_call(
        functools.partial(_qkv_body, nj=nj),
        grid=(m // BM, nj),
        in_specs=[
            pl.BlockSpec((BM, D_MODEL), lambda i, j: (i, 0)),
            pl.BlockSpec((D_MODEL, bn), lambda i, j: (0, _snake(i, j, nj))),
            pl.BlockSpec((1, bn), lambda i, j: (0, _snake(i, j, nj))),
            pl.BlockSpec((ROT_HALF, BM), lambda i, j: (0, i)),
            pl.BlockSpec((ROT_HALF, BM), lambda i, j: (0, i)),
            wo_in_spec,
        ],
        out_specs=[pl.BlockSpec((BM, bn), lambda i, j: (i, _snake(i, j, nj))), wo_out_spec],
        out_shape=[jax.ShapeDtypeStruct((m, ODD_IN), BF16), wo_shape],
        scratch_shapes=[
            pltpu.VMEM((BM, LANES), F32),
            pltpu.VMEM((BM, LANES), F32),
            pltpu.VMEM((BM, LANES), F32),
        ],
        compiler_params=_params(("arbitrary", "arbitrary"), 56),
        name="qkv_rope",
    )(h, w_b, b, cos_t, sin_t, w_o)


def _split3(x):
    hi = x.astype(BF16)
    r1 = x - hi.astype(F32)
    mid = r1.astype(BF16)
    lo = (r1 - mid.astype(F32)).astype(BF16)
    return hi, mid, lo


def _even_mix_body(u_ref, v_ref, z0_ref, z1_ref, xbc_ref, dt_ref, lng_ref, lnb_ref, ws_ref, bs_ref, cw_ref, cb_ref,
                   dtb_ref, alog_ref, dskip_ref, snorm_ref, e3_ref, shift_ref, o_ref, state_ref, xx_ref):
    c = pl.program_id(0)
    q = CHUNK

    @pl.when(c == 0)
    def _():
        state_ref[...] = jnp.zeros_like(state_ref)
        xx_ref[pl.ds(0, q), :] = jnp.zeros((q, CONV_DIM), BF16)

    row = lax.broadcasted_iota(jnp.int32, (q, q), 0)
    col = lax.broadcasted_iota(jnp.int32, (q, q), 1)
    causal = col <= row

    for g in range(GROUPS):
        seg = slice(g * GDIM, (g + 1) * GDIM)
        vg = v_ref[:, seg].astype(F32)
        mu = jnp.mean(vg, axis=-1, keepdims=True)
        d = vg - mu
        var = jnp.mean(d * d, axis=-1, keepdims=True)
        vn = d * lax.rsqrt(var + EPS) * lng_ref[:, seg] + lnb_ref[:, seg]
        w = jnp.where(causal, ws_ref[g], 0.0).astype(BF16)
        s = _dot(w, vn.astype(BF16)) + bs_ref[:, g:g + 1]
        o_ref[:, seg] = (u_ref[:, seg].astype(F32) * s).astype(BF16)

    x_cur = xbc_ref[...]
    xx_ref[pl.ds(q, q), :] = x_cur
    shifted = _dot(shift_ref[...], xx_ref[...])
    conv = cb_ref[...] + cw_ref[SSD_CONV - 1:SSD_CONV, :] * x_cur.astype(F32)
    for k in range(SSD_CONV - 1):
        conv = conv + cw_ref[k:k + 1, :] * shifted[k * q:(k + 1) * q]
    xx_ref[pl.ds(q - CONV_TAIL, CONV_TAIL), :] = x_cur[q - CONV_TAIL:, :]
    xbc = _silu(conv)
    xs = xbc[:, :D_MODEL]

    dt = jax.nn.softplus(dt_ref[...] + dtb_ref[...])
    a = dt * (-jnp.exp(alog_ref[...]))
    tri = jnp.where(causal, 1.0, 0.0).astype(BF16)
    a_hi, a_mid, a_lo = _split3(a)
    acs = _dot(tri, a_hi) + _dot(tri, a_mid) + _dot(tri, a_lo)
    acs_t = acs.T
    both = jnp.concatenate([dt, acs], axis=0)
    b_hi, b_mid, b_lo = _split3(both)
    both_e = _dot(jnp.concatenate([b_hi, b_mid, b_lo], axis=1), e3_ref[...])
    dt_e = both_e[:q]
    acs_e = both_e[q:]
    last_e = acs_e[q - 1:q, :]
    xdt = xs * dt_e
    xdec = (xdt * jnp.exp(last_e - acs_e)).astype(BF16)
    xdt_b = xdt.astype(BF16)
    grow_e = jnp.exp(acs_e)
    chunk_decay = jnp.exp(last_e)

    lane = lax.broadcasted_iota(jnp.int32, (q, LANES), 1)
    lo_half = lane < HEAD_DIM
    gate = _silu(jnp.concatenate([z0_ref[...], z1_ref[...]], axis=1).astype(F32))

    for g in range(GROUPS):
        seg = slice(g * GDIM, (g + 1) * GDIM)
        b_g = xbc[:, D_MODEL + g * SSD_STATE:D_MODEL + (g + 1) * SSD_STATE].astype(BF16)
        c_g = xbc[:, D_MODEL + GROUPS * SSD_STATE + g * SSD_STATE:
                  D_MODEL + GROUPS * SSD_STATE + (g + 1) * SSD_STATE].astype(BF16)
        cb_causal = jnp.where(causal, _dot_nt(c_g, b_g), 0.0)
        y_off = _dot(c_g, state_ref[g].astype(BF16)) * grow_e[:, seg]
        pieces = []
        for p in range(GDIM // LANES):
            mats = []
            for hh in range(2):
                h = g * (GDIM // HEAD_DIM) + 2 * p + hh
                seg_ij = jnp.minimum(acs[:, h:h + 1] - acs_t[h:h + 1, :], 0.0)
                mats.append((cb_causal * jnp.exp(seg_ij)).astype(BF16))
            x2 = xdt_b[:, g * GDIM + p * LANES:g * GDIM + (p + 1) * LANES]
            zero = jnp.zeros_like(x2)
            rhs = jnp.concatenate([jnp.where(lo_half, x2, zero), jnp.where(lo_half, zero, x2)], axis=0)
            pieces.append(_dot(jnp.concatenate(mats, axis=1), rhs))
        y_diag = jnp.concatenate(pieces, axis=1)
        new_states = _dot_tn(b_g, xdec[:, seg])
        state_ref[g] = state_ref[g] * chunk_decay[:, seg] + new_states
        y = y_diag + y_off + xs[:, seg] * dskip_ref[:, seg]
        y = y * gate[:, seg]
        y = y * lax.rsqrt(jnp.mean(y * y, axis=-1, keepdims=True) + EPS)
        o_ref[:, D_MODEL + g * GDIM:D_MODEL + (g + 1) * GDIM] = (y * snorm_ref[:, seg]).astype(BF16)


def _even_mix(uv, rest, dt_raw, ln_g, ln_b, ws, bs_t, conv_w, conv_b, dt_bias, a_log, d_skip_e, ssd_norm, e3,
              shift):
    m = uv.shape[0]
    full = lambda shape: pl.BlockSpec(shape, lambda c: (0,) * len(shape))
    z_block = CONV_DIM // EVEN_BN
    return pl.pallas_call(
        _even_mix_body,
        grid=(m // CHUNK,),
        in_specs=[
            pl.BlockSpec((CHUNK, D_MODEL), lambda c: (c, 0)),
            pl.BlockSpec((CHUNK, D_MODEL), lambda c: (c, 1)),
            pl.BlockSpec((CHUNK, EVEN_BN), lambda c: (c, z_block)),
            pl.BlockSpec((CHUNK, EVEN_BN), lambda c: (c, z_block + 1)),
            pl.BlockSpec((CHUNK, CONV_DIM), lambda c: (c, 0)),
            pl.BlockSpec((CHUNK, LANES), lambda c: (c, 0)),
            full((1, D_MODEL)), full((1, D_MODEL)),
            full((GROUPS, CHUNK, CHUNK)), full((CHUNK, GROUPS)),
            full((SSD_CONV, CONV_DIM)), full((1, CONV_DIM)),
            full((1, LANES)), full((1, LANES)),
            full((1, D_MODEL)), full((1, D_MODEL)),
            full((3 * LANES, D_MODEL)),
            full(((SSD_CONV - 1) * CHUNK, 2 * CHUNK)),
        ],
        out_specs=pl.BlockSpec((CHUNK, 2 * D_MODEL), lambda c: (c, 0)),
        out_shape=jax.ShapeDtypeStruct((m, 2 * D_MODEL), BF16),
        scratch_shapes=[
            pltpu.VMEM((GROUPS, SSD_STATE, GDIM), F32),
            pltpu.VMEM((2 * CHUNK, CONV_DIM), BF16),
        ],
        compiler_params=_params(("arbitrary",), 48),
        name="even_mix",
    )(uv, uv, rest, rest, rest, dt_raw, ln_g, ln_b, ws, bs_t, conv_w, conv_b, dt_bias, a_log, d_skip_e, ssd_norm,
      e3, shift)


def _swa_body(sink_ref, q_ref, kv_ref, kvp_ref, o_ref):
    n = pl.program_id(0)
    w = CHUNK
    row = lax.broadcasted_iota(jnp.int32, (w, w), 0)
    col = lax.broadcasted_iota(jnp.int32, (w, w), 1)
    own = col <= row
    lo_half = lax.broadcasted_iota(jnp.int32, (w, LANES), 1) < HEAD_DIM
    prev_bias = jnp.where(n > 0, 0.0, -jnp.inf)

    def head_tiles(ref, base, k):
        t = ref[:, base + (k // 2) * LANES:base + (k // 2 + 1) * LANES].astype(F32)
        r = pltpu.roll(t, HEAD_DIM, 1)
        return (t, r) if k % 2 == 0 else (r, t)

    for k in range(GROUPS):
        k_lo, k_hi = head_tiles(kv_ref, 0, k)
        kp_lo, kp_hi = head_tiles(kvp_ref, 0, k)
        v_lo, v_hi = head_tiles(kv_ref, KV_WIDTH, k)
        vp_lo, vp_hi = head_tiles(kvp_ref, KV_WIDTH, k)
        kk = jnp.where(lo_half, k_lo, k_hi).astype(BF16)
        kkp = jnp.where(lo_half, kp_lo, kp_hi).astype(BF16)
        out = []
        for parity in range(2):
            if parity == 0:
                vv = jnp.concatenate([jnp.where(lo_half, v_lo, 1.0), jnp.where(lo_half, vp_lo, 1.0)], axis=0)
            else:
                vv = jnp.concatenate([jnp.where(lo_half, 1.0, v_hi), jnp.where(lo_half, 1.0, vp_hi)], axis=0)
            lhs = []
            for p in range(ATT_REP // 2):
                q2 = q_ref[:, k * GDIM + p * LANES:k * GDIM + (p + 1) * LANES]
                zero = jnp.zeros_like(q2)
                lhs.append(jnp.where(lo_half, q2, zero) if parity == 0 else jnp.where(lo_half, zero, q2))
            lhs = jnp.concatenate(lhs, axis=0)
            s_own = _dot_nt(lhs, kk)
            s_prev = _dot_nt(lhs, kkp)
            probs, esink = [], []
            for p in range(ATT_REP // 2):
                sink = sink_ref[k * ATT_REP + 2 * p + parity]
                s = jnp.where(own, s_own[p * w:(p + 1) * w], s_prev[p * w:(p + 1) * w] + prev_bias)
                mx = jnp.maximum(jnp.max(s, axis=-1, keepdims=True), sink)
                e = jnp.exp(s - mx)
                probs.append(jnp.concatenate([jnp.where(own, e, 0.0).astype(BF16),
                                              jnp.where(own, 0.0, e).astype(BF16)], axis=1))
                esink.append(jnp.exp(sink - mx))
            o = _dot(jnp.concatenate(probs, axis=0), vv.astype(BF16))
            out.append((o, esink))
        for p in range(ATT_REP // 2):
            o_even = out[0][0][p * w:(p + 1) * w]
            o_odd = out[1][0][p * w:(p + 1) * w]
            num = jnp.where(lo_half, o_even, o_odd)
            den = pltpu.roll(jnp.where(lo_half, o_odd, o_even), HEAD_DIM, 1)
            den = den + jnp.where(lo_half, out[0][1][p], out[1][1][p])
            o_ref[:, k * GDIM + p * LANES:k * GDIM + (p + 1) * LANES] = (num / den).astype(BF16)


def _swa(qkv, sinks):
    m = qkv.shape[0]
    kv_block = ATT_HEADS * HEAD_DIM // (2 * KV_WIDTH)
    return pl.pallas_call(
        _swa_body,
        grid=(m // CHUNK,),
        in_specs=[
            pl.BlockSpec(memory_space=pltpu.SMEM),
            pl.BlockSpec((CHUNK, D_MODEL), lambda n: (n, 0)),
            pl.BlockSpec((CHUNK, 2 * KV_WIDTH), lambda n: (n, kv_block)),
            pl.BlockSpec((CHUNK, 2 * KV_WIDTH), lambda n: (jnp.maximum(n - 1, 0), kv_block)),
        ],
        out_specs=pl.BlockSpec((CHUNK, D_MODEL), lambda n: (n, 0)),
        out_shape=jax.ShapeDtypeStruct((m, D_MODEL), BF16),
        compiler_params=_params(("parallel",), 32),
        name="swa",
    )(sinks, qkv, qkv, qkv)


def _xattn_body(x_ref, g_ref, wq_ref, kv_ref, wo_ref, gn_ref, o_ref, hn_ref, wq_b, wo_b):
    @pl.when(pl.program_id(0) == 0)
    def _():
        wq_b[...] = wq_ref[...].astype(BF16)
        wo_b[...] = wo_ref[...].astype(BF16)

    x = x_ref[...]
    h = _rms(x, g_ref[...]).astype(BF16)
    q = _dot(h, wq_b[...]).astype(BF16)
    outs = []
    for hd in range(X_HEADS):
        seg = slice(hd * X_HEAD_DIM, (hd + 1) * X_HEAD_DIM)
        k = kv_ref[:, seg]
        v = kv_ref[:, X_WIDTH + hd * X_HEAD_DIM:X_WIDTH + (hd + 1) * X_HEAD_DIM]
        s = _dot_nt(q[:, seg], k) * X_SCALE
        e = jnp.exp(s - jnp.max(s, axis=-1, keepdims=True))
        o = _dot(e.astype(BF16), v) * (1.0 / jnp.sum(e, axis=-1, keepdims=True))
        outs.append(o.astype(BF16))
    y = x + _dot(jnp.concatenate(outs, axis=1), wo_b[...])
    o_ref[...] = y
    hn_ref[...] = _rms(y, gn_ref[...]).astype(BF16)


def _xattn(x, gain, w_q, kv, w_o, layer, next_gain):
    m = x.shape[0]
    bm = 512
    row = pl.BlockSpec((bm, D_MODEL), lambda i: (i, 0))
    vec = pl.BlockSpec((1, D_MODEL), lambda i: (0, 0))
    return pl.pallas_call(
        _xattn_body,
        grid=(m // bm,),
        in_specs=[
            row, vec,
            pl.BlockSpec((None, D_MODEL, X_WIDTH), lambda i: (layer, 0, 0)),
            pl.BlockSpec((N_MEM, 2 * X_WIDTH), lambda i: (0, 0)),
            pl.BlockSpec((None, X_WIDTH, D_MODEL), lambda i: (layer, 0, 0)),
            vec,
        ],
        out_specs=[row, row],
        out_shape=[jax.ShapeDtypeStruct((m, D_MODEL), F32), jax.ShapeDtypeStruct((m, D_MODEL), BF16)],
        scratch_shapes=[pltpu.VMEM((D_MODEL, X_WIDTH), BF16), pltpu.VMEM((X_WIDTH, D_MODEL), BF16)],
        compiler_params=_params(("arbitrary",), 48),
        name="xattn",
    )(x, gain, w_q, kv, w_o, next_gain)


def _ffn(x, h, gain, w_gu, w_down, layer, name, more_casts, out_gain, gain_use):
    casts = [(w_down, layer)] + list(more_casts)
    if h is None:
        act, w_down_b, *copies = _ffn_up(x, gain.reshape(1, -1), w_gu, layer, casts)
    else:
        act, w_down_b, *copies = _ffn_up(h, None, w_gu, layer, casts)
    out = _mm_rows(act, w_down_b, x, 0.5, name, out_gain.reshape(1, -1), gain_use)
    y, h_next = out if gain_use == "next" else (out, None)
    return y, h_next, copies


def _pad_lanes(v):
    return jnp.pad(v.reshape(1, -1), ((0, 0), (0, LANES - v.shape[-1])))


def kernel(x, mem, positions, norm_ffn1, w_ffn1_gu, w_ffn1_down, norm_mix, w_in_even, gm_ln_g, gm_ln_b, gm_ws, gm_bs, conv_w, conv_b, dt_bias, a_log, d_skip, ssd_norm, w_out_even, w_qkv, b_qkv, sinks, w_o_odd, norm_xq, norm_mem, w_xq, w_xkv, w_xo, norm_ffn2, w_ffn2_gu, w_ffn2_down, final_norm):
    bsz, seq, d = x.shape
    assert (bsz, seq, d) == (1, SEQ, D_MODEL)
    xr = x.reshape(seq, d)
    memr = mem.reshape(N_MEM, d)
    inv_freq = ROPE_THETA ** (-jnp.arange(0, ROT_DIM, 2, dtype=F32) / ROT_DIM)
    cos_t, sin_t = _rope_table(positions.reshape(1, seq), inv_freq.reshape(ROT_HALF, 1))
    head_of_lane = jnp.arange(D_MODEL, dtype=jnp.int32) // HEAD_DIM
    e1 = (jnp.arange(LANES, dtype=jnp.int32)[:, None] == head_of_lane[None, :]).astype(BF16)
    e3 = jnp.concatenate([e1, e1, e1], axis=0)
    sel_row = jnp.arange((SSD_CONV - 1) * CHUNK, dtype=jnp.int32)[:, None]
    sel_col = jnp.arange(2 * CHUNK, dtype=jnp.int32)[None, :]
    shift = (sel_col == CHUNK + sel_row % CHUNK - (SSD_CONV - 1) + sel_row // CHUNK).astype(BF16)

    w_in_t = jnp.swapaxes(w_in_even, 1, 2)

    w_gu1 = w_ffn1_gu
    h = None
    for i in range(DEPTH):
        j = i // 2
        xr, h, (w_gu2,) = _ffn(xr, h, norm_ffn1[i], w_gu1, w_ffn1_down, i, "ffn1_down",
                               [(w_ffn2_gu, i)], norm_mix[i], "next")
        if i % 2 == 0:
            uv, rest, dt_raw, w_out_b = _even_in(h, w_in_t, w_out_even, j)
            mix = _even_mix(
                uv, rest, dt_raw, gm_ln_g[j].reshape(1, -1), gm_ln_b[j].reshape(1, -1), gm_ws[j], gm_bs[j].T,
                conv_w[j], conv_b[j].reshape(1, -1), _pad_lanes(dt_bias[j]), _pad_lanes(a_log[j]),
                jnp.repeat(d_skip[j], HEAD_DIM).reshape(1, -1), ssd_norm[j].reshape(1, -1), e3, shift)
            xr = _mm_rows(mix, w_out_b, xr, 1.0, "even_out")
        else:
            qkv, w_o_b = _qkv(h, w_qkv_b, w_o_odd, j, b_qkv[j].reshape(1, -1), cos_t, sin_t)
            att = _swa(qkv, sinks[j])
            xr = _mm_rows(att, w_o_b, xr, 1.0, "odd_out")
        kv = _norm_mm(memr, norm_mem[i].reshape(1, -1), w_xkv, i, N_MEM, BF16, "mem_kv")
        xr, h = _xattn(xr, norm_xq[i].reshape(1, -1), w_xq, kv, w_xo, i, norm_ffn2[i].reshape(1, -1))
        last = i + 1 == DEPTH
        ahead = []
        if not last:
            ahead.append((w_ffn1_gu, i + 1))
            if (i + 1) % 2 == 1:
                ahead.append((w_qkv, (i + 1) // 2))
        xr, h, copies = _ffn(xr, h, norm_ffn2[i], w_gu2, w_ffn2_down, i, "ffn2_down", ahead,
                             final_norm if last else norm_ffn1[i + 1], "final" if last else "next")
        if copies:
            w_gu1 = copies[0]
            w_qkv_b = copies[1] if len(copies) > 1 else None
    return xr.reshape(bsz, seq, d)
```

```python
import functools

import jax
import jax.numpy as jnp
from jax import lax
from jax.experimental import pallas as pl
from jax.experimental.pallas import tpu as pltpu

F32 = jnp.float32
BF16 = jnp.bfloat16

D_MODEL = 2048
SEQ = 8192
DEPTH = 2
EPS = 1e-5
N_MEM = 256
D_FF = 5632
CHUNK = 128
GROUPS = 4
GDIM = D_MODEL // GROUPS
HEAD_DIM = 64
SSD_HEADS = 32
SSD_STATE = 128
SSD_CONV = 4
CONV_DIM = D_MODEL + 2 * GROUPS * SSD_STATE
EVEN_MAIN = 2 * D_MODEL + D_MODEL + CONV_DIM
ATT_HEADS = 32
ATT_REP = ATT_HEADS // GROUPS
ATT_SCALE = HEAD_DIM ** -0.5
ROT_DIM = HEAD_DIM // 4
ROT_HALF = ROT_DIM // 2
ROPE_THETA = 500000.0
KV_WIDTH = GROUPS * HEAD_DIM
ODD_IN = (ATT_HEADS + 2 * GROUPS) * HEAD_DIM
X_HEADS = 4
X_HEAD_DIM = 128
X_WIDTH = X_HEADS * X_HEAD_DIM
X_SCALE = X_HEAD_DIM ** -0.5

LANES = 128
SUBLANES = 8
MXU_COLS = 256
FFN_SLAB = 1024
CONV_TAIL = 16
BM = 1024
BN = 512
MIB = 1024 * 1024


def _params(semantics, vmem_mib):
    return pltpu.CompilerParams(dimension_semantics=semantics, vmem_limit_bytes=vmem_mib * MIB)


def _rms(x, g):
    ms = jnp.mean(x * x, axis=-1, keepdims=True)
    return x * lax.rsqrt(ms + EPS) * g


def _silu(x):
    return x * jax.nn.sigmoid(x)


def _gelu(x):
    return 0.5 * x * (1.0 + lax.erf(x * (2.0 ** -0.5)))


def _dot(a, b):
    return jnp.dot(a, b, preferred_element_type=F32)


def _dot_nt(a, b):
    return lax.dot_general(a, b, (((1,), (1,)), ((), ())), preferred_element_type=F32)


def _dot_tn(a, b):
    return lax.dot_general(a, b, (((0,), (0,)), ((), ())), preferred_element_type=F32)


def _snake(i, j, nj):
    return jnp.where(i % 2 == 0, j, nj - 1 - j)


def _cast_specs(w, layer, rows, steps_per_row_block):
    _, r, c = w.shape
    n_slabs = pl.cdiv(r, rows)
    slab = lambda i, j: jnp.minimum(i * steps_per_row_block + j, n_slabs - 1)
    return (pl.BlockSpec((None, rows, c), lambda i, j: (layer, slab(i, j), 0)),
            pl.BlockSpec((rows, c), lambda i, j: (slab(i, j), 0)),
            jax.ShapeDtypeStruct((r, c), BF16))


def _slab_rows(w, steps):
    tile = 2 * SUBLANES
    return tile * pl.cdiv(w.shape[1], tile * steps)


def _ffn_up_body(*refs, n_cast, normed_input):
    n_in = 3 if normed_input else 4
    wg_ref, wu_ref = refs[n_in - 2:n_in]
    cast_in, o_ref = refs[n_in:n_in + n_cast], refs[n_in + n_cast]
    cast_out = refs[n_in + n_cast + 1:n_in + 2 * n_cast + 1]
    scratch = refs[n_in + 2 * n_cast + 1:]
    w_ref = scratch[0]
    if normed_input:
        h_ref = refs[0]
    else:
        x_ref, g_ref, h_ref = refs[0], refs[1], scratch[1]

        @pl.when(pl.program_id(1) == 0)
        def _():
            h_ref[...] = _rms(x_ref[...], g_ref[...]).astype(BF16)

    for src, dst in zip(cast_in, cast_out):
        dst[...] = src[...].astype(BF16)
    groups = BN // MXU_COLS
    for c in range(groups):
        w_ref[:, (2 * c) * MXU_COLS:(2 * c + 1) * MXU_COLS] = wg_ref[:, c * MXU_COLS:(c + 1) * MXU_COLS].astype(BF16)
        w_ref[:, (2 * c + 1) * MXU_COLS:(2 * c + 2) * MXU_COLS] = wu_ref[:, c * MXU_COLS:(c + 1) * MXU_COLS].astype(BF16)
    rows = o_ref.shape[0]
    for r0 in range(0, rows, FFN_SLAB):
        gu = _dot(h_ref[r0:r0 + FFN_SLAB, :], w_ref[...])
        for c in range(groups):
            g = gu[:, (2 * c) * MXU_COLS:(2 * c + 1) * MXU_COLS]
            u = gu[:, (2 * c + 1) * MXU_COLS:(2 * c + 2) * MXU_COLS]
            o_ref[r0:r0 + FFN_SLAB, c * MXU_COLS:(c + 1) * MXU_COLS] = (_silu(g) * u).astype(BF16)


def _ffn_up(xh, gain, w_gu, layer, cast_weights):
    m = xh.shape[0]
    normed_input = gain is None
    bm = 2 * BM if normed_input else BM
    nj = D_FF // BN
    steps = (m // bm) * nj
    if w_gu.ndim == 3:
        w_spec = lambda off: pl.BlockSpec((None, D_MODEL, BN), lambda i, j: (layer, 0, _snake(i, j, nj) + off))
    else:
        w_spec = lambda off: pl.BlockSpec((D_MODEL, BN), lambda i, j: (0, _snake(i, j, nj) + off))
    cast_specs = [_cast_specs(w, l, _slab_rows(w, steps), nj) for w, l in cast_weights]
    row_specs = [pl.BlockSpec((bm, D_MODEL), lambda i, j: (i, 0))]
    scratch = [pltpu.VMEM((D_MODEL, 2 * BN), BF16)]
    operands = [xh]
    if not normed_input:
        row_specs.append(pl.BlockSpec((1, D_MODEL), lambda i, j: (0, 0)))
        scratch.append(pltpu.VMEM((bm, D_MODEL), BF16))
        operands.append(gain)
    return pl.pallas_call(
        functools.partial(_ffn_up_body, n_cast=len(cast_weights), normed_input=normed_input),
        grid=(m // bm, nj),
        in_specs=row_specs + [w_spec(0), w_spec(nj)] + [s[0] for s in cast_specs],
        out_specs=[pl.BlockSpec((bm, BN), lambda i, j: (i, _snake(i, j, nj)))] + [s[1] for s in cast_specs],
        out_shape=[jax.ShapeDtypeStruct((m, D_FF), BF16)] + [s[2] for s in cast_specs],
        scratch_shapes=scratch,
        compiler_params=_params(("arbitrary", "arbitrary"), 56),
        name="ffn_up",
    )(*operands, w_gu, w_gu, *[w for w, _ in cast_weights])


def _mm_rows_body(a_ref, w_ref, r_ref, *refs, scale, gain_use):
    y = r_ref[...] + scale * _dot(a_ref[...], w_ref[...])
    o_ref = refs[1] if gain_use else refs[0]
    if gain_use == "final":
        y = _rms(y, refs[0][...])
    elif gain_use == "next":
        refs[2][...] = _rms(y, refs[0][...]).astype(BF16)
    o_ref[...] = y


ROWS_BM = 512


def _mm_rows(a, w, res, scale, name, gain=None, gain_use=None):
    m, k = a.shape
    n = w.shape[1]
    row_out = pl.BlockSpec((ROWS_BM, n), lambda i: (i, 0))
    out_specs, out_shape = [row_out], [jax.ShapeDtypeStruct((m, n), F32)]
    if gain_use == "next":
        out_specs.append(row_out)
        out_shape.append(jax.ShapeDtypeStruct((m, n), BF16))
    gains = [] if gain is None else [gain]
    out = pl.pallas_call(
        functools.partial(_mm_rows_body, scale=scale, gain_use=gain_use),
        grid=(m // ROWS_BM,),
        in_specs=[
            pl.BlockSpec((ROWS_BM, k), lambda i: (i, 0)),
            pl.BlockSpec((k, n), lambda i: (0, 0), pipeline_mode=pl.Buffered(1)),
            pl.BlockSpec((ROWS_BM, n), lambda i: (i, 0)),
        ] + [pl.BlockSpec((1, n), lambda i: (0, 0))] * len(gains),
        out_specs=out_specs,
        out_shape=out_shape,
        compiler_params=_params(("arbitrary",), 60),
        name=name,
    )(a, w, res, *gains)
    return out if gain_use == "next" else out[0]


def _norm_mm_body(x_ref, g_ref, w_ref, o_ref, h_ref):
    @pl.when(pl.program_id(1) == 0)
    def _():
        h_ref[...] = _rms(x_ref[...], g_ref[...]).astype(BF16)

    o_ref[...] = _dot(h_ref[...], w_ref[...].astype(BF16)).astype(o_ref.dtype)


def _norm_mm(x, gain, w, layer, bm, out_dtype, name):
    m, k = x.shape
    n = w.shape[2]
    return pl.pallas_call(
        _norm_mm_body,
        grid=(m // bm, n // BN),
        in_specs=[
            pl.BlockSpec((bm, k), lambda i, j: (i, 0)),
            pl.BlockSpec((1, k), lambda i, j: (0, 0)),
            pl.BlockSpec((None, k, BN), lambda i, j: (layer, 0, j)),
        ],
        out_specs=pl.BlockSpec((bm, BN), lambda i, j: (i, j)),
        out_shape=jax.ShapeDtypeStruct((m, n), out_dtype),
        scratch_shapes=[pltpu.VMEM((bm, k), BF16)],
        compiler_params=_params(("parallel", "arbitrary"), 40),
        name=name,
    )(x, gain, w)


EVEN_BN = 2 * BN
QKV_SLAB = 256


def _row_slabs(ref, slab):
    return [slice(r0, r0 + slab) for r0 in range(0, ref.shape[0], slab)]


def _even_uv_body(h_ref, w_ref, o_ref):
    o_ref[...] = _gelu(_dot_nt(h_ref[...], w_ref[...].astype(BF16))).astype(BF16)


def _even_rest_body(h_ref, w_ref, wdt_ref, wo_ref, o_ref, dt_ref, wo_out_ref):
    @pl.when(pl.program_id(1) == 0)
    def _():
        row = lax.broadcasted_iota(jnp.int32, (LANES, 1), 0)
        dt_ref[...] = _dot_nt(h_ref[...], jnp.where(row < SSD_HEADS, wdt_ref[...], 0.0).astype(BF16))

    wo_out_ref[...] = wo_ref[...].astype(BF16)
    o_ref[...] = _dot_nt(h_ref[...], w_ref[...].astype(BF16)).astype(BF16)


def _even_in(h, w_in_t, w_out, layer):
    m = h.shape[0]
    bn = EVEN_BN
    x_spec = pl.BlockSpec((BM, D_MODEL), lambda i, j: (i, 0))

    nj = 2 * D_MODEL // bn
    uv = pl.pallas_call(
        _even_uv_body,
        grid=(m // BM, nj),
        in_specs=[x_spec,
                  pl.BlockSpec((None, bn, D_MODEL), lambda i, j: (layer, _snake(i, j, nj), 0))],
        out_specs=pl.BlockSpec((BM, bn), lambda i, j: (i, _snake(i, j, nj))),
        out_shape=jax.ShapeDtypeStruct((m, 2 * D_MODEL), BF16),
        compiler_params=_params(("arbitrary", "arbitrary"), 56),
        name="even_in_uv",
    )(h, w_in_t)

    n_xbc, n_z = CONV_DIM // bn, D_MODEL // bn
    nr = n_xbc + n_z
    z_first, xbc_first = 2 * D_MODEL // bn, 3 * D_MODEL // bn

    def w_block(i, j):
        jc = _snake(i, j, nr)
        return jnp.where(jc < n_xbc, xbc_first + jc, z_first + jc - n_xbc)

    wo_in_spec, wo_out_spec, wo_shape = _cast_specs(w_out, layer, 128, nr)
    rest, dt_raw, w_out_b = pl.pallas_call(
        _even_rest_body,
        grid=(m // BM, nr),
        in_specs=[x_spec,
                  pl.BlockSpec((None, bn, D_MODEL), lambda i, j: (layer, w_block(i, j), 0)),
                  pl.BlockSpec((None, LANES, D_MODEL), lambda i, j: (layer, EVEN_MAIN // LANES, 0)),
                  wo_in_spec],
        out_specs=[pl.BlockSpec((BM, bn), lambda i, j: (i, _snake(i, j, nr))),
                   pl.BlockSpec((BM, LANES), lambda i, j: (i, 0)),
                   wo_out_spec],
        out_shape=[jax.ShapeDtypeStruct((m, CONV_DIM + D_MODEL), BF16),
                   jax.ShapeDtypeStruct((m, LANES), F32),
                   wo_shape],
        compiler_params=_params(("arbitrary", "arbitrary"), 56),
        name="even_in_rest",
    )(h, w_in_t, w_in_t, w_out)
    return uv, rest, dt_raw, w_out_b


def _rope_table_body(pos_ref, invf_ref, cos_ref, sin_ref):
    ang = pos_ref[...].astype(F32) * invf_ref[...]
    cos_ref[...] = jnp.cos(ang)
    sin_ref[...] = jnp.sin(ang)


def _rope_table(pos_row, invf_col):
    shape = jax.ShapeDtypeStruct((ROT_HALF, pos_row.shape[1]), F32)
    return pl.pallas_call(_rope_table_body, out_shape=[shape, shape], name="rope_table")(pos_row, invf_col)


def _qkv_body(h_ref, w_ref, b_ref, cost_ref, sint_ref, wo_ref, o_ref, wo_out_ref,
              cos_ref, sn_ref, sp_ref, *, nj):
    j = pl.program_id(1)

    @pl.when(j == 0)
    def _():
        reps = LANES // ROT_HALF
        cos = jnp.concatenate([cost_ref[...]] * reps, axis=0).T
        sin = jnp.concatenate([sint_ref[...]] * reps, axis=0).T
        lane = lax.broadcasted_iota(jnp.int32, (1, LANES), 1) % HEAD_DIM
        first = lane < ROT_HALF
        second = (lane >= ROT_HALF) & (lane < ROT_DIM)
        cos_ref[...] = jnp.where(first | second, cos, 1.0)
        sn_ref[...] = jnp.where(first, -sin, 0.0)
        sp_ref[...] = jnp.where(second, sin, 0.0)

    wo_out_ref[...] = wo_ref[...].astype(BF16)

    col_block = _snake(pl.program_id(0), j, nj)
    tiles = o_ref.shape[1] // LANES

    def kind(tile):
        return "q" if tile < ATT_HEADS * HEAD_DIM // LANES else "k" if tile < (ODD_IN - KV_WIDTH) // LANES else "v"

    for rows in _row_slabs(o_ref, QKV_SLAB):
        acc = _dot(h_ref[rows, :], w_ref[...]) + b_ref[...]

        def rope(a):
            return (a * cos_ref[rows, :] + pltpu.roll(a, LANES - ROT_HALF, 1) * sn_ref[rows, :]
                    + pltpu.roll(a, ROT_HALF, 1) * sp_ref[rows, :])

        for t in range(tiles):
            a = acc[:, t * LANES:(t + 1) * LANES]
            kinds = [kind(jb * tiles + t) for jb in range(nj)]
            roped = rope(a) if set(kinds) != {"v"} else None
            by_kind = {"q": lambda: roped * ATT_SCALE, "k": lambda: roped, "v": lambda: a}
            val = by_kind[kinds[-1]]()
            for jb in range(nj - 2, -1, -1):
                if kinds[jb] != kinds[jb + 1]:
                    val = jnp.where(col_block <= jb, by_kind[kinds[jb]](), val)
            o_ref[rows, t * LANES:(t + 1) * LANES] = val.astype(BF16)


def _qkv(h, w_b, w_o, layer, b, cos_t, sin_t):
    m = h.shape[0]
    nj = 2
    bn = ODD_IN // nj
    wo_in_spec, wo_out_spec, wo_shape = _cast_specs(w_o, layer, 128, nj)
    return pl.pallas_call(
        functools.partial(_qkv_body, nj=nj),
        grid=(m // BM, nj),
        in_specs=[
            pl.BlockSpec((BM, D_MODEL), lambda i, j: (i, 0)),
            pl.BlockSpec((D_MODEL, bn), lambda i, j: (0, _snake(i, j, nj))),
            pl.BlockSpec((1, bn), lambda i, j: (0, _snake(i, j, nj))),
            pl.BlockSpec((ROT_HALF, BM), lambda i, j: (0, i)),
            pl.BlockSpec((ROT_HALF, BM), lambda i, j: (0, i)),
            wo_in_spec,
        ],
        out_specs=[pl.BlockSpec((BM, bn), lambda i, j: (i, _snake(i, j, nj))), wo_out_spec],
        out_shape=[jax.ShapeDtypeStruct((m, ODD_IN), BF16), wo_shape],
        scratch_shapes=[
            pltpu.VMEM((BM, LANES), F32),
            pltpu.VMEM((BM, LANES), F32),
            pltpu.VMEM((BM, LANES), F32),
        ],
        compiler_params=_params(("arbitrary", "arbitrary"), 56),
        name="qkv_rope",
    )(h, w_b, b, cos_t, sin_t, w_o)


def _split3(x):
    hi = x.astype(BF16)
    r1 = x - hi.astype(F32)
    mid = r1.astype(BF16)
    lo = (r1 - mid.astype(F32)).astype(BF16)
    return hi, mid, lo


def _even_mix_body(u_ref, v_ref, z0_ref, z1_ref, xbc_ref, dt_ref, lng_ref, lnb_ref, ws_ref, bs_ref, cw_ref, cb_ref,
                   dtb_ref, alog_ref, dskip_ref, snorm_ref, e3_ref, shift_ref, o_ref, state_ref, xx_ref):
    c = pl.program_id(0)
    q = CHUNK

    @pl.when(c == 0)
    def _():
        state_ref[...] = jnp.zeros_like(state_ref)
        xx_ref[pl.ds(0, q), :] = jnp.zeros((q, CONV_DIM), BF16)

    row = lax.broadcasted_iota(jnp.int32, (q, q), 0)
    col = lax.broadcasted_iota(jnp.int32, (q, q), 1)
    causal = col <= row

    for g in range(GROUPS):
        seg = slice(g * GDIM, (g + 1) * GDIM)
        vg = v_ref[:, seg].astype(F32)
        mu = jnp.mean(vg, axis=-1, keepdims=True)
        d = vg - mu
        var = jnp.mean(d * d, axis=-1, keepdims=True)
        vn = d * lax.rsqrt(var + EPS) * lng_ref[:, seg] + lnb_ref[:, seg]
        w = jnp.where(causal, ws_ref[g], 0.0).astype(BF16)
        s = _dot(w, vn.astype(BF16)) + bs_ref[:, g:g + 1]
        o_ref[:, seg] = (u_ref[:, seg].astype(F32) * s).astype(BF16)

    x_cur = xbc_ref[...]
    xx_ref[pl.ds(q, q), :] = x_cur
    shifted = _dot(shift_ref[...], xx_ref[...])
    conv = cb_ref[...] + cw_ref[SSD_CONV - 1:SSD_CONV, :] * x_cur.astype(F32)
    for k in range(SSD_CONV - 1):
        conv = conv + cw_ref[k:k + 1, :] * shifted[k * q:(k + 1) * q]
    xx_ref[pl.ds(q - CONV_TAIL, CONV_TAIL), :] = x_cur[q - CONV_TAIL:, :]
    xbc = _silu(conv)
    xs = xbc[:, :D_MODEL]

    dt = jax.nn.softplus(dt_ref[...] + dtb_ref[...])
    a = dt * (-jnp.exp(alog_ref[...]))
    tri = jnp.where(causal, 1.0, 0.0).astype(BF16)
    a_hi, a_mid, a_lo = _split3(a)
    acs = _dot(tri, a_hi) + _dot(tri, a_mid) + _dot(tri, a_lo)
    acs_t = acs.T
    both = jnp.concatenate([dt, acs], axis=0)
    b_hi, b_mid, b_lo = _split3(both)
    both_e = _dot(jnp.concatenate([b_hi, b_mid, b_lo], axis=1), e3_ref[...])
    dt_e = both_e[:q]
    acs_e = both_e[q:]
    last_e = acs_e[q - 1:q, :]
    xdt = xs * dt_e
    xdec = (xdt * jnp.exp(last_e - acs_e)).astype(BF16)
    xdt_b = xdt.astype(BF16)
    grow_e = jnp.exp(acs_e)
    chunk_decay = jnp.exp(last_e)

    lane = lax.broadcasted_iota(jnp.int32, (q, LANES), 1)
    lo_half = lane < HEAD_DIM
    gate = _silu(jnp.concatenate([z0_ref[...], z1_ref[...]], axis=1).astype(F32))

    for g in range(GROUPS):
        seg = slice(g * GDIM, (g + 1) * GDIM)
        b_g = xbc[:, D_MODEL + g * SSD_STATE:D_MODEL + (g + 1) * SSD_STATE].astype(BF16)
        c_g = xbc[:, D_MODEL + GROUPS * SSD_STATE + g * SSD_STATE:
                  D_MODEL + GROUPS * SSD_STATE + (g + 1) * SSD_STATE].astype(BF16)
        cb_causal = jnp.where(causal, _dot_nt(c_g, b_g), 0.0)
        y_off = _dot(c_g, state_ref[g].astype(BF16)) * grow_e[:, seg]
        pieces = []
        for p in range(GDIM // LANES):
            mats = []
            for hh in range(2):
                h = g * (GDIM // HEAD_DIM) + 2 * p + hh
                seg_ij = jnp.minimum(acs[:, h:h + 1] - acs_t[h:h + 1, :], 0.0)
                mats.append((cb_causal * jnp.exp(seg_ij)).astype(BF16))
            x2 = xdt_b[:, g * GDIM + p * LANES:g * GDIM + (p + 1) * LANES]
            zero = jnp.zeros_like(x2)
            rhs = jnp.concatenate([jnp.where(lo_half, x2, zero), jnp.where(lo_half, zero, x2)], axis=0)
            pieces.append(_dot(jnp.concatenate(mats, axis=1), rhs))
        y_diag = jnp.concatenate(pieces, axis=1)
        new_states = _dot_tn(b_g, xdec[:, seg])
        state_ref[g] = state_ref[g] * chunk_decay[:, seg] + new_states
        y = y_diag + y_off + xs[:, seg] * dskip_ref[:, seg]
        y = y * gate[:, seg]
        y = y * lax.rsqrt(jnp.mean(y * y, axis=-1, keepdims=True) + EPS)
        o_ref[:, D_MODEL + g * GDIM:D_MODEL + (g + 1) * GDIM] = (y * snorm_ref[:, seg]).astype(BF16)


def _even_mix(uv, rest, dt_raw, ln_g, ln_b, ws, bs_t, conv_w, conv_b, dt_bias, a_log, d_skip_e, ssd_norm, e3,
              shift):
    m = uv.shape[0]
    full = lambda shape: pl.BlockSpec(shape, lambda c: (0,) * len(shape))
    z_block = CONV_DIM // EVEN_BN
    return pl.pallas_call(
        _even_mix_body,
        grid=(m // CHUNK,),
        in_specs=[
            pl.BlockSpec((CHUNK, D_MODEL), lambda c: (c, 0)),
            pl.BlockSpec((CHUNK, D_MODEL), lambda c: (c, 1)),
            pl.BlockSpec((CHUNK, EVEN_BN), lambda c: (c, z_block)),
            pl.BlockSpec((CHUNK, EVEN_BN), lambda c: (c, z_block + 1)),
            pl.BlockSpec((CHUNK, CONV_DIM), lambda c: (c, 0)),
            pl.BlockSpec((CHUNK, LANES), lambda c: (c, 0)),
            full((1, D_MODEL)), full((1, D_MODEL)),
            full((GROUPS, CHUNK, CHUNK)), full((CHUNK, GROUPS)),
            full((SSD_CONV, CONV_DIM)), full((1, CONV_DIM)),
            full((1, LANES)), full((1, LANES)),
            full((1, D_MODEL)), full((1, D_MODEL)),
            full((3 * LANES, D_MODEL)),
            full(((SSD_CONV - 1) * CHUNK, 2 * CHUNK)),
        ],
        out_specs=pl.BlockSpec((CHUNK, 2 * D_MODEL), lambda c: (c, 0)),
        out_shape=jax.ShapeDtypeStruct((m, 2 * D_MODEL), BF16),
        scratch_shapes=[
            pltpu.VMEM((GROUPS, SSD_STATE, GDIM), F32),
            pltpu.VMEM((2 * CHUNK, CONV_DIM), BF16),
        ],
        compiler_params=_params(("arbitrary",), 48),
        name="even_mix",
    )(uv, uv, rest, rest, rest, dt_raw, ln_g, ln_b, ws, bs_t, conv_w, conv_b, dt_bias, a_log, d_skip_e, ssd_norm,
      e3, shift)


def _swa_body(sink_ref, q_ref, kv_ref, kvp_ref, o_ref):
    n = pl.program_id(0)
    w = CHUNK
    row = lax.broadcasted_iota(jnp.int32, (w, w), 0)
    col = lax.broadcasted_iota(jnp.int32, (w, w), 1)
    own = col <= row
    lo_half = lax.broadcasted_iota(jnp.int32, (w, LANES), 1) < HEAD_DIM
    prev_bias = jnp.where(n > 0, 0.0, -jnp.inf)

    def head_tiles(ref, base, k):
        t = ref[:, base + (k // 2) * LANES:base + (k // 2 + 1) * LANES].astype(F32)
        r = pltpu.roll(t, HEAD_DIM, 1)
        return (t, r) if k % 2 == 0 else (r, t)

    for k in range(GROUPS):
        k_lo, k_hi = head_tiles(kv_ref, 0, k)
        kp_lo, kp_hi = head_tiles(kvp_ref, 0, k)
        v_lo, v_hi = head_tiles(kv_ref, KV_WIDTH, k)
        vp_lo, vp_hi = head_tiles(kvp_ref, KV_WIDTH, k)
        kk = jnp.where(lo_half, k_lo, k_hi).astype(BF16)
        kkp = jnp.where(lo_half, kp_lo, kp_hi).astype(BF16)
        out = []
        for parity in range(2):
            if parity == 0:
                vv = jnp.concatenate([jnp.where(lo_half, v_lo, 1.0), jnp.where(lo_half, vp_lo, 1.0)], axis=0)
            else:
                vv = jnp.concatenate([jnp.where(lo_half, 1.0, v_hi), jnp.where(lo_half, 1.0, vp_hi)], axis=0)
            lhs = []
            for p in range(ATT_REP // 2):
                q2 = q_ref[:, k * GDIM + p * LANES:k * GDIM + (p + 1) * LANES]
                zero = jnp.zeros_like(q2)
                lhs.append(jnp.where(lo_half, q2, zero) if parity == 0 else jnp.where(lo_half, zero, q2))
            lhs = jnp.concatenate(lhs, axis=0)
            s_own = _dot_nt(lhs, kk)
            s_prev = _dot_nt(lhs, kkp)
            probs, esink = [], []
            for p in range(ATT_REP // 2):
                sink = sink_ref[k * ATT_REP + 2 * p + parity]
                s = jnp.where(own, s_own[p * w:(p + 1) * w], s_prev[p * w:(p + 1) * w] + prev_bias)
                mx = jnp.maximum(jnp.max(s, axis=-1, keepdims=True), sink)
                e = jnp.exp(s - mx)
                probs.append(jnp.concatenate([jnp.where(own, e, 0.0).astype(BF16),
                                              jnp.where(own, 0.0, e).astype(BF16)], axis=1))
                esink.append(jnp.exp(sink - mx))
            o = _dot(jnp.concatenate(probs, axis=0), vv.astype(BF16))
            out.append((o, esink))
        for p in range(ATT_REP // 2):
            o_even = out[0][0][p * w:(p + 1) * w]
            o_odd = out[1][0][p * w:(p + 1) * w]
            num = jnp.where(lo_half, o_even, o_odd)
            den = pltpu.roll(jnp.where(lo_half, o_odd, o_even), HEAD_DIM, 1)
            den = den + jnp.where(lo_half, out[0][1][p], out[1][1][p])
            o_ref[:, k * GDIM + p * LANES:k * GDIM + (p + 1) * LANES] = (num / den).astype(BF16)


def _swa(qkv, sinks):
    m = qkv.shape[0]
    kv_block = ATT_HEADS * HEAD_DIM // (2 * KV_WIDTH)
    return pl.pallas_call(
        _swa_body,
        grid=(m // CHUNK,),
        in_specs=[
            pl.BlockSpec(memory_space=pltpu.SMEM),
            pl.BlockSpec((CHUNK, D_MODEL), lambda n: (n, 0)),
            pl.BlockSpec((CHUNK, 2 * KV_WIDTH), lambda n: (n, kv_block)),
            pl.BlockSpec((CHUNK, 2 * KV_WIDTH), lambda n: (jnp.maximum(n - 1, 0), kv_block)),
        ],
        out_specs=pl.BlockSpec((CHUNK, D_MODEL), lambda n: (n, 0)),
        out_shape=jax.ShapeDtypeStruct((m, D_MODEL), BF16),
        compiler_params=_params(("parallel",), 32),
        name="swa",
    )(sinks, qkv, qkv, qkv)


def _xattn_body(x_ref, g_ref, wq_ref, kv_ref, wo_ref, gn_ref, o_ref, hn_ref, wq_b, wo_b):
    @pl.when(pl.program_id(0) == 0)
    def _():
        wq_b[...] = wq_ref[...].astype(BF16)
        wo_b[...] = wo_ref[...].astype(BF16)

    x = x_ref[...]
    h = _rms(x, g_ref[...]).astype(BF16)
    q = _dot(h, wq_b[...]).astype(BF16)
    outs = []
    for hd in range(X_HEADS):
        seg = slice(hd * X_HEAD_DIM, (hd + 1) * X_HEAD_DIM)
        k = kv_ref[:, seg]
        v = kv_ref[:, X_WIDTH + hd * X_HEAD_DIM:X_WIDTH + (hd + 1) * X_HEAD_DIM]
        s = _dot_nt(q[:, seg], k) * X_SCALE
        e = jnp.exp(s - jnp.max(s, axis=-1, keepdims=True))
        o = _dot(e.astype(BF16), v) * (1.0 / jnp.sum(e, axis=-1, keepdims=True))
        outs.append(o.astype(BF16))
    y = x + _dot(jnp.concatenate(outs, axis=1), wo_b[...])
    o_ref[...] = y
    hn_ref[...] = _rms(y, gn_ref[...]).astype(BF16)


def _xattn(x, gain, w_q, kv, w_o, layer, next_gain):
    m = x.shape[0]
    bm = 512
    row = pl.BlockSpec((bm, D_MODEL), lambda i: (i, 0))
    vec = pl.BlockSpec((1, D_MODEL), lambda i: (0, 0))
    return pl.pallas_call(
        _xattn_body,
        grid=(m // bm,),
        in_specs=[
            row, vec,
            pl.BlockSpec((None, D_MODEL, X_WIDTH), lambda i: (layer, 0, 0)),
            pl.BlockSpec((N_MEM, 2 * X_WIDTH), lambda i: (0, 0)),
            pl.BlockSpec((None, X_WIDTH, D_MODEL), lambda i: (layer, 0, 0)),
            vec,
        ],
        out_specs=[row, row],
        out_shape=[jax.ShapeDtypeStruct((m, D_MODEL), F32), jax.ShapeDtypeStruct((m, D_MODEL), BF16)],
        scratch_shapes=[pltpu.VMEM((D_MODEL, X_WIDTH), BF16), pltpu.VMEM((X_WIDTH, D_MODEL), BF16)],
        compiler_params=_params(("arbitrary",), 48),
        name="xattn",
    )(x, gain, w_q, kv, w_o, next_gain)


def _ffn(x, h, gain, w_gu, w_down, layer, name, more_casts, out_gain, gain_use):
    casts = [(w_down, layer)] + list(more_casts)
    if h is None:
        act, w_down_b, *copies = _ffn_up(x, gain.reshape(1, -1), w_gu, layer, casts)
    else:
        act, w_down_b, *copies = _ffn_up(h, None, w_gu, layer, casts)
    out = _mm_rows(act, w_down_b, x, 0.5, name, out_gain.reshape(1, -1), gain_use)
    y, h_next = out if gain_use == "next" else (out, None)
    return y, h_next, copies


def _pad_lanes(v):
    return jnp.pad(v.reshape(1, -1), ((0, 0), (0, LANES - v.shape[-1])))


def kernel(x, mem, positions, norm_ffn1, w_ffn1_gu, w_ffn1_down, norm_mix, w_in_even, gm_ln_g, gm_ln_b, gm_ws, gm_bs, conv_w, conv_b, dt_bias, a_log, d_skip, ssd_norm, w_out_even, w_qkv, b_qkv, sinks, w_o_odd, norm_xq, norm_mem, w_xq, w_xkv, w_xo, norm_ffn2, w_ffn2_gu, w_ffn2_down, final_norm):
    bsz, seq, d = x.shape
    assert (bsz, seq, d) == (1, SEQ, D_MODEL)
    xr = x.reshape(seq, d)
    memr = mem.reshape(N_MEM, d)
    inv_freq = ROPE_THETA ** (-jnp.arange(0, ROT_DIM, 2, dtype=F32) / ROT_DIM)
    cos_t, sin_t = _rope_table(positions.reshape(1, seq), inv_freq.reshape(ROT_HALF, 1))
    head_of_lane = jnp.arange(D_MODEL, dtype=jnp.int32) // HEAD_DIM
    e1 = (jnp.arange(LANES, dtype=jnp.int32)[:, None] == head_of_lane[None, :]).astype(BF16)
    e3 = jnp.concatenate([e1, e1, e1], axis=0)
    sel_row = jnp.arange((SSD_CONV - 1) * CHUNK, dtype=jnp.int32)[:, None]
    sel_col = jnp.arange(2 * CHUNK, dtype=jnp.int32)[None, :]
    shift = (sel_col == CHUNK + sel_row % CHUNK - (SSD_CONV - 1) + sel_row // CHUNK).astype(BF16)

    w_in_t = jnp.swapaxes(w_in_even, 1, 2)

    w_gu1 = w_ffn1_gu
    h = None
    for i in range(DEPTH):
        j = i // 2
        xr, h, (w_gu2,) = _ffn(xr, h, norm_ffn1[i], w_gu1, w_ffn1_down, i, "ffn1_down",
                               [(w_ffn2_gu, i)], norm_mix[i], "next")
        if i % 2 == 0:
            uv, rest, dt_raw, w_out_b = _even_in(h, w_in_t, w_out_even, j)
            mix = _even_mix(
                uv, rest, dt_raw, gm_ln_g[j].reshape(1, -1), gm_ln_b[j].reshape(1, -1), gm_ws[j], gm_bs[j].T,
                conv_w[j], conv_b[j].reshape(1, -1), _pad_lanes(dt_bias[j]), _pad_lanes(a_log[j]),
                jnp.repeat(d_skip[j], HEAD_DIM).reshape(1, -1), ssd_norm[j].reshape(1, -1), e3, shift)
            xr = _mm_rows(mix, w_out_b, xr, 1.0, "even_out")
        else:
            qkv, w_o_b = _qkv(h, w_qkv_b, w_o_odd, j, b_qkv[j].reshape(1, -1), cos_t, sin_t)
            att = _swa(qkv, sinks[j])
            xr = _mm_rows(att, w_o_b, xr, 1.0, "odd_out")
        kv = _norm_mm(memr, norm_mem[i].reshape(1, -1), w_xkv, i, N_MEM, BF16, "mem_kv")
        xr, h = _xattn(xr, norm_xq[i].reshape(1, -1), w_xq, kv, w_xo, i, norm_ffn2[i].reshape(1, -1))
        last = i + 1 == DEPTH
        ahead = []
        if not last:
            ahead.append((w_ffn1_gu, i + 1))
            if (i + 1) % 2 == 1:
                ahead.append((w_qkv, (i + 1) // 2))
        xr, h, copies = _ffn(xr, h, norm_ffn2[i], w_gu2, w_ffn2_down, i, "ffn2_down", ahead,
                             final_norm if last else norm_ffn1[i + 1], "final" if last else "next")
        if copies:
            w_gu1 = copies[0]
            w_qkv_b = copies[1] if len(copies) > 1 else None
    return xr.reshape(bsz, seq, d)
```

```python
import functools

import jax
import jax.numpy as jnp
from jax import lax
from jax.experimental import pallas as pl
from jax.experimental.pallas import tpu as pltpu

F32 = jnp.float32
BF16 = jnp.bfloat16

D_MODEL = 2048
SEQ = 8192
DEPTH = 2
EPS = 1e-5
N_MEM = 256
D_FF = 5632
CHUNK = 128
GROUPS = 4
GDIM = D_MODEL // GROUPS
HEAD_DIM = 64
SSD_HEADS = 32
SSD_STATE = 128
SSD_CONV = 4
CONV_DIM = D_MODEL + 2 * GROUPS * SSD_STATE
EVEN_MAIN = 2 * D_MODEL + D_MODEL + CONV_DIM
ATT_HEADS = 32
ATT_REP = ATT_HEADS // GROUPS
ATT_SCALE = HEAD_DIM ** -0.5
ROT_DIM = HEAD_DIM // 4
ROT_HALF = ROT_DIM // 2
ROPE_THETA = 500000.0
KV_WIDTH = GROUPS * HEAD_DIM
ODD_IN = (ATT_HEADS + 2 * GROUPS) * HEAD_DIM
X_HEADS = 4
X_HEAD_DIM = 128
X_WIDTH = X_HEADS * X_HEAD_DIM
X_SCALE = X_HEAD_DIM ** -0.5

LANES = 128
SUBLANES = 8
MXU_COLS = 256
FFN_SLAB = 1024
CONV_TAIL = 16
BM = 1024
BN = 512
MIB = 1024 * 1024


def _params(semantics, vmem_mib):
    return pltpu.CompilerParams(dimension_semantics=semantics, vmem_limit_bytes=vmem_mib * MIB)


def _rms(x, g):
    ms = jnp.mean(x * x, axis=-1, keepdims=True)
    return x * lax.rsqrt(ms + EPS) * g


def _silu(x):
    return x * jax.nn.sigmoid(x)


def _gelu(x):
    return 0.5 * x * (1.0 + lax.erf(x * (2.0 ** -0.5)))


def _dot(a, b):
    return jnp.dot(a, b, preferred_element_type=F32)


def _dot_nt(a, b):
    return lax.dot_general(a, b, (((1,), (1,)), ((), ())), preferred_element_type=F32)


def _dot_tn(a, b):
    return lax.dot_general(a, b, (((0,), (0,)), ((), ())), preferred_element_type=F32)


def _snake(i, j, nj):
    return jnp.where(i % 2 == 0, j, nj - 1 - j)


def _cast_specs(w, layer, rows, steps_per_row_block):
    _, r, c = w.shape
    n_slabs = pl.cdiv(r, rows)
    slab = lambda i, j: jnp.minimum(i * steps_per_row_block + j, n_slabs - 1)
    return (pl.BlockSpec((None, rows, c), lambda i, j: (layer, slab(i, j), 0)),
            pl.BlockSpec((rows, c), lambda i, j: (slab(i, j), 0)),
            jax.ShapeDtypeStruct((r, c), BF16))


def _slab_rows(w, steps):
    tile = 2 * SUBLANES
    return tile * pl.cdiv(w.shape[1], tile * steps)


def _ffn_up_body(*refs, n_cast, normed_input):
    n_in = 3 if normed_input else 4
    wg_ref, wu_ref = refs[n_in - 2:n_in]
    cast_in, o_ref = refs[n_in:n_in + n_cast], refs[n_in + n_cast]
    cast_out = refs[n_in + n_cast + 1:n_in + 2 * n_cast + 1]
    scratch = refs[n_in + 2 * n_cast + 1:]
    w_ref = scratch[0]
    if normed_input:
        h_ref = refs[0]
    else:
        x_ref, g_ref, h_ref = refs[0], refs[1], scratch[1]

        @pl.when(pl.program_id(1) == 0)
        def _():
            h_ref[...] = _rms(x_ref[...], g_ref[...]).astype(BF16)

    for src, dst in zip(cast_in, cast_out):
        dst[...] = src[...].astype(BF16)
    groups = BN // MXU_COLS
    for c in range(groups):
        w_ref[:, (2 * c) * MXU_COLS:(2 * c + 1) * MXU_COLS] = wg_ref[:, c * MXU_COLS:(c + 1) * MXU_COLS].astype(BF16)
        w_ref[:, (2 * c + 1) * MXU_COLS:(2 * c + 2) * MXU_COLS] = wu_ref[:, c * MXU_COLS:(c + 1) * MXU_COLS].astype(BF16)
    rows = o_ref.shape[0]
    for r0 in range(0, rows, FFN_SLAB):
        gu = _dot(h_ref[r0:r0 + FFN_SLAB, :], w_ref[...])
        for c in range(groups):
            g = gu[:, (2 * c) * MXU_COLS:(2 * c + 1) * MXU_COLS]
            u = gu[:, (2 * c + 1) * MXU_COLS:(2 * c + 2) * MXU_COLS]
            o_ref[r0:r0 + FFN_SLAB, c * MXU_COLS:(c + 1) * MXU_COLS] = (_silu(g) * u).astype(BF16)


def _ffn_up(xh, gain, w_gu, layer, cast_weights):
    m = xh.shape[0]
    normed_input = gain is None
    bm = 2 * BM if normed_input else BM
    nj = D_FF // BN
    steps = (m // bm) * nj
    if w_gu.ndim == 3:
        w_spec = lambda off: pl.BlockSpec((None, D_MODEL, BN), lambda i, j: (layer, 0, _snake(i, j, nj) + off))
    else:
        w_spec = lambda off: pl.BlockSpec((D_MODEL, BN), lambda i, j: (0, _snake(i, j, nj) + off))
    cast_specs = [_cast_specs(w, l, _slab_rows(w, steps), nj) for w, l in cast_weights]
    row_specs = [pl.BlockSpec((bm, D_MODEL), lambda i, j: (i, 0))]
    scratch = [pltpu.VMEM((D_MODEL, 2 * BN), BF16)]
    operands = [xh]
    if not normed_input:
        row_specs.append(pl.BlockSpec((1, D_MODEL), lambda i, j: (0, 0)))
        scratch.append(pltpu.VMEM((bm, D_MODEL), BF16))
        operands.append(gain)
    return pl.pallas_call(
        functools.partial(_ffn_up_body, n_cast=len(cast_weights), normed_input=normed_input),
        grid=(m // bm, nj),
        in_specs=row_specs + [w_spec(0), w_spec(nj)] + [s[0] for s in cast_specs],
        out_specs=[pl.BlockSpec((bm, BN), lambda i, j: (i, _snake(i, j, nj)))] + [s[1] for s in cast_specs],
        out_shape=[jax.ShapeDtypeStruct((m, D_FF), BF16)] + [s[2] for s in cast_specs],
        scratch_shapes=scratch,
        compiler_params=_params(("arbitrary", "arbitrary"), 56),
        name="ffn_up",
    )(*operands, w_gu, w_gu, *[w for w, _ in cast_weights])


def _mm_rows_body(a_ref, w_ref, r_ref, *refs, scale, gain_use):
    y = r_ref[...] + scale * _dot(a_ref[...], w_ref[...])
    o_ref = refs[1] if gain_use else refs[0]
    if gain_use == "final":
        y = _rms(y, refs[0][...])
    elif gain_use == "next":
        refs[2][...] = _rms(y, refs[0][...]).astype(BF16)
    o_ref[...] = y


ROWS_BM = 512


def _mm_rows(a, w, res, scale, name, gain=None, gain_use=None):
    m, k = a.shape
    n = w.shape[1]
    row_out = pl.BlockSpec((ROWS_BM, n), lambda i: (i, 0))
    out_specs, out_shape = [row_out], [jax.ShapeDtypeStruct((m, n), F32)]
    if gain_use == "next":
        out_specs.append(row_out)
        out_shape.append(jax.ShapeDtypeStruct((m, n), BF16))
    gains = [] if gain is None else [gain]
    out = pl.pallas_call(
        functools.partial(_mm_rows_body, scale=scale, gain_use=gain_use),
        grid=(m // ROWS_BM,),
        in_specs=[
            pl.BlockSpec((ROWS_BM, k), lambda i: (i, 0)),
            pl.BlockSpec((k, n), lambda i: (0, 0), pipeline_mode=pl.Buffered(1)),
            pl.BlockSpec((ROWS_BM, n), lambda i: (i, 0)),
        ] + [pl.BlockSpec((1, n), lambda i: (0, 0))] * len(gains),
        out_specs=out_specs,
        out_shape=out_shape,
        compiler_params=_params(("arbitrary",), 60),
        name=name,
    )(a, w, res, *gains)
    return out if gain_use == "next" else out[0]


def _norm_mm_body(x_ref, g_ref, w_ref, o_ref, h_ref):
    @pl.when(pl.program_id(1) == 0)
    def _():
        h_ref[...] = _rms(x_ref[...], g_ref[...]).astype(BF16)

    o_ref[...] = _dot(h_ref[...], w_ref[...].astype(BF16)).astype(o_ref.dtype)


def _norm_mm(x, gain, w, layer, bm, out_dtype, name):
    m, k = x.shape
    n = w.shape[2]
    return pl.pallas_call(
        _norm_mm_body,
        grid=(m // bm, n // BN),
        in_specs=[
            pl.BlockSpec((bm, k), lambda i, j: (i, 0)),
            pl.BlockSpec((1, k), lambda i, j: (0, 0)),
            pl.BlockSpec((None, k, BN), lambda i, j: (layer, 0, j)),
        ],
        out_specs=pl.BlockSpec((bm, BN), lambda i, j: (i, j)),
        out_shape=jax.ShapeDtypeStruct((m, n), out_dtype),
        scratch_shapes=[pltpu.VMEM((bm, k), BF16)],
        compiler_params=_params(("parallel", "arbitrary"), 40),
        name=name,
    )(x, gain, w)


EVEN_BN = 2 * BN
QKV_SLAB = 256


def _row_slabs(ref, slab):
    return [slice(r0, r0 + slab) for r0 in range(0, ref.shape[0], slab)]


W_RING = 3


def _ring_weight_block(w_hbm, ring_ref, sem, layer, nj, block_of_col):
    rows = ring_ref.shape[1]
    n_steps = pl.num_programs(0) * nj
    step = pl.program_id(0) * nj + pl.program_id(1)

    def copy(s):
        col = block_of_col(_snake(s // nj, s % nj, nj))
        src = w_hbm.at[layer, pl.ds(pl.multiple_of(col * rows, rows), rows), :]
        return pltpu.make_async_copy(src, ring_ref.at[s % W_RING], sem.at[s % W_RING])

    @pl.when(step == 0)
    def _():
        for s0 in range(W_RING - 1):
            copy(jnp.int32(s0)).start()

    @pl.when(step + (W_RING - 1) < n_steps)
    def _():
        copy(step + (W_RING - 1)).start()

    copy(step).wait()
    return ring_ref.at[step % W_RING]


def _even_uv_body(h_ref, w_hbm, o_ref, ring_ref, sem, *, layer, nj):
    w_ref = _ring_weight_block(w_hbm, ring_ref, sem, layer, nj, lambda jc: jc)
    o_ref[...] = _gelu(_dot_nt(h_ref[...], w_ref[...].astype(BF16))).astype(BF16)


def _even_rest_body(h_ref, w_hbm, wdt_ref, wo_ref, o_ref, dt_ref, wo_out_ref, ring_ref, sem, *, layer, nj, block_of_col):
    w_ref = _ring_weight_block(w_hbm, ring_ref, sem, layer, nj, block_of_col)

    @pl.when(pl.program_id(1) == 0)
    def _():
        row = lax.broadcasted_iota(jnp.int32, (LANES, 1), 0)
        dt_ref[...] = _dot_nt(h_ref[...], jnp.where(row < SSD_HEADS, wdt_ref[...], 0.0).astype(BF16))

    wo_out_ref[...] = wo_ref[...].astype(BF16)
    o_ref[...] = _dot_nt(h_ref[...], w_ref[...].astype(BF16)).astype(BF16)


def _even_in(h, w_in_t, w_out, layer):
    m = h.shape[0]
    bn = EVEN_BN
    x_spec = pl.BlockSpec((BM, D_MODEL), lambda i, j: (i, 0))

    ring = [pltpu.VMEM((W_RING, bn, D_MODEL), F32), pltpu.SemaphoreType.DMA((W_RING,))]
    nj = 2 * D_MODEL // bn
    uv = pl.pallas_call(
        functools.partial(_even_uv_body, layer=layer, nj=nj),
        grid=(m // BM, nj),
        in_specs=[x_spec, pl.BlockSpec(memory_space=pl.ANY)],
        out_specs=pl.BlockSpec((BM, bn), lambda i, j: (i, _snake(i, j, nj))),
        out_shape=jax.ShapeDtypeStruct((m, 2 * D_MODEL), BF16),
        scratch_shapes=ring,
        compiler_params=_params(("arbitrary", "arbitrary"), 56),
        name="even_in_uv",
    )(h, w_in_t)

    n_xbc, n_z = CONV_DIM // bn, D_MODEL // bn
    nr = n_xbc + n_z
    z_first, xbc_first = 2 * D_MODEL // bn, 3 * D_MODEL // bn

    def block_of_col(jc):
        return jnp.where(jc < n_xbc, xbc_first + jc, z_first + jc - n_xbc)

    wo_in_spec, wo_out_spec, wo_shape = _cast_specs(w_out, layer, 128, nr)
    rest, dt_raw, w_out_b = pl.pallas_call(
        functools.partial(_even_rest_body, layer=layer, nj=nr, block_of_col=block_of_col),
        grid=(m // BM, nr),
        in_specs=[x_spec,
                  pl.BlockSpec(memory_space=pl.ANY),
                  pl.BlockSpec((None, LANES, D_MODEL), lambda i, j: (layer, EVEN_MAIN // LANES, 0)),
                  wo_in_spec],
        out_specs=[pl.BlockSpec((BM, bn), lambda i, j: (i, _snake(i, j, nr))),
                   pl.BlockSpec((BM, LANES), lambda i, j: (i, 0)),
                   wo_out_spec],
        out_shape=[jax.ShapeDtypeStruct((m, CONV_DIM + D_MODEL), BF16),
                   jax.ShapeDtypeStruct((m, LANES), F32),
                   wo_shape],
        scratch_shapes=ring,
        compiler_params=_params(("arbitrary", "arbitrary"), 56),
        name="even_in_rest",
    )(h, w_in_t, w_in_t, w_out)
    return uv, rest, dt_raw, w_out_b


def _rope_table_body(pos_ref, invf_ref, cos_ref, sin_ref):
    ang = pos_ref[...].astype(F32) * invf_ref[...]
    cos_ref[...] = jnp.cos(ang)
    sin_ref[...] = jnp.sin(ang)


def _rope_table(pos_row, invf_col):
    shape = jax.ShapeDtypeStruct((ROT_HALF, pos_row.shape[1]), F32)
    return pl.pallas_call(_rope_table_body, out_shape=[shape, shape], name="rope_table")(pos_row, invf_col)


def _qkv_body(h_ref, w_ref, b_ref, cost_ref, sint_ref, wo_ref, o_ref, wo_out_ref,
              cos_ref, sn_ref, sp_ref, *, nj):
    j = pl.program_id(1)

    @pl.when(j == 0)
    def _():
        reps = LANES // ROT_HALF
        cos = jnp.concatenate([cost_ref[...]] * reps, axis=0).T
        sin = jnp.concatenate([sint_ref[...]] * reps, axis=0).T
        lane = lax.broadcasted_iota(jnp.int32, (1, LANES), 1) % HEAD_DIM
        first = lane < ROT_HALF
        second = (lane >= ROT_HALF) & (lane < ROT_DIM)
        cos_ref[...] = jnp.where(first | second, cos, 1.0)
        sn_ref[...] = jnp.where(first, -sin, 0.0)
        sp_ref[...] = jnp.where(second, sin, 0.0)

    wo_out_ref[...] = wo_ref[...].astype(BF16)

    col_block = _snake(pl.program_id(0), j, nj)
    tiles = o_ref.shape[1] // LANES

    def kind(tile):
        return "q" if tile < ATT_HEADS * HEAD_DIM // LANES else "k" if tile < (ODD_IN - KV_WIDTH) // LANES else "v"

    for rows in _row_slabs(o_ref, QKV_SLAB):
        acc = _dot(h_ref[rows, :], w_ref[...]) + b_ref[...]

        def rope(a):
            return (a * cos_ref[rows, :] + pltpu.roll(a, LANES - ROT_HALF, 1) * sn_ref[rows, :]
                    + pltpu.roll(a, ROT_HALF, 1) * sp_ref[rows, :])

        for t in range(tiles):
            a = acc[:, t * LANES:(t + 1) * LANES]
            kinds = [kind(jb * tiles + t) for jb in range(nj)]
            roped = rope(a) if set(kinds) != {"v"} else None
            by_kind = {"q": lambda: roped * ATT_SCALE, "k": lambda: roped, "v": lambda: a}
            val = by_kind[kinds[-1]]()
            for jb in range(nj - 2, -1, -1):
                if kinds[jb] != kinds[jb + 1]:
                    val = jnp.where(col_block <= jb, by_kind[kinds[jb]](), val)
            o_ref[rows, t * LANES:(t + 1) * LANES] = val.astype(BF16)


def _qkv(h, w_b, w_o, layer, b, cos_t, sin_t):
    m = h.shape[0]
    nj = 2
    bn = ODD_IN // nj
    wo_in_spec, wo_out_spec, wo_shape = _cast_specs(w_o, layer, 128, nj)
    return pl.pallas_call(
        functools.partial(_qkv_body, nj=nj),
        grid=(m // BM, nj),
        in_specs=[
            pl.BlockSpec((BM, D_MODEL), lambda i, j: (i, 0)),
            pl.BlockSpec((D_MODEL, bn), lambda i, j: (0, _snake(i, j, nj))),
            pl.BlockSpec((1, bn), lambda i, j: (0, _snake(i, j, nj))),
            pl.BlockSpec((ROT_HALF, BM), lambda i, j: (0, i)),
            pl.BlockSpec((ROT_HALF, BM), lambda i, j: (0, i)),
            wo_in_spec,
        ],
        out_specs=[pl.BlockSpec((BM, bn), lambda i, j: (i, _snake(i, j, nj))), wo_out_spec],
        out_shape=[jax.ShapeDtypeStruct((m, ODD_IN), BF16), wo_shape],
        scratch_shapes=[
            pltpu.VMEM((BM, LANES), F32),
            pltpu.VMEM((BM, LANES), F32),
            pltpu.VMEM((BM, LANES), F32),
        ],
        compiler_params=_params(("arbitrary", "arbitrary"), 56),
        name="qkv_rope",
    )(h, w_b, b, cos_t, sin_t, w_o)


def _split3(x):
    hi = x.astype(BF16)
    r1 = x - hi.astype(F32)
    mid = r1.astype(BF16)
    lo = (r1 - mid.astype(F32)).astype(BF16)
    return hi, mid, lo


def _even_mix_body(u_ref, v_ref, z0_ref, z1_ref, xbc_ref, dt_ref, lng_ref, lnb_ref, ws_ref, bs_ref, cw_ref, cb_ref,
                   dtb_ref, alog_ref, dskip_ref, snorm_ref, e3_ref, shift_ref, o_ref, state_ref, xx_ref):
    c = pl.program_id(0)
    q = CHUNK

    @pl.when(c == 0)
    def _():
        state_ref[...] = jnp.zeros_like(state_ref)
        xx_ref[pl.ds(0, q), :] = jnp.zeros((q, CONV_DIM), BF16)

    row = lax.broadcasted_iota(jnp.int32, (q, q), 0)
    col = lax.broadcasted_iota(jnp.int32, (q, q), 1)
    causal = col <= row

    for g in range(GROUPS):
        seg = slice(g * GDIM, (g + 1) * GDIM)
        vg = v_ref[:, seg].astype(F32)
        mu = jnp.mean(vg, axis=-1, keepdims=True)
        d = vg - mu
        var = jnp.mean(d * d, axis=-1, keepdims=True)
        vn = d * lax.rsqrt(var + EPS) * lng_ref[:, seg] + lnb_ref[:, seg]
        w = jnp.where(causal, ws_ref[g], 0.0).astype(BF16)
        s = _dot(w, vn.astype(BF16)) + bs_ref[:, g:g + 1]
        o_ref[:, seg] = (u_ref[:, seg].astype(F32) * s).astype(BF16)

    x_cur = xbc_ref[...]
    xx_ref[pl.ds(q, q), :] = x_cur
    shifted = _dot(shift_ref[...], xx_ref[...])
    conv = cb_ref[...] + cw_ref[SSD_CONV - 1:SSD_CONV, :] * x_cur.astype(F32)
    for k in range(SSD_CONV - 1):
        conv = conv + cw_ref[k:k + 1, :] * shifted[k * q:(k + 1) * q]
    xx_ref[pl.ds(q - CONV_TAIL, CONV_TAIL), :] = x_cur[q - CONV_TAIL:, :]
    xbc = _silu(conv)
    xs = xbc[:, :D_MODEL]

    dt = jax.nn.softplus(dt_ref[...] + dtb_ref[...])
    a = dt * (-jnp.exp(alog_ref[...]))
    tri = jnp.where(causal, 1.0, 0.0).astype(BF16)
    a_hi, a_mid, a_lo = _split3(a)
    acs = _dot(tri, a_hi) + _dot(tri, a_mid) + _dot(tri, a_lo)
    acs_t = acs.T
    both = jnp.concatenate([dt, acs], axis=0)
    b_hi, b_mid, b_lo = _split3(both)
    both_e = _dot(jnp.concatenate([b_hi, b_mid, b_lo], axis=1), e3_ref[...])
    dt_e = both_e[:q]
    acs_e = both_e[q:]
    last_e = acs_e[q - 1:q, :]
    xdt = xs * dt_e
    xdec = (xdt * jnp.exp(last_e - acs_e)).astype(BF16)
    xdt_b = xdt.astype(BF16)
    grow_e = jnp.exp(acs_e)
    chunk_decay = jnp.exp(last_e)

    lane = lax.broadcasted_iota(jnp.int32, (q, LANES), 1)
    lo_half = lane < HEAD_DIM
    gate = _silu(jnp.concatenate([z0_ref[...], z1_ref[...]], axis=1).astype(F32))

    for g in range(GROUPS):
        seg = slice(g * GDIM, (g + 1) * GDIM)
        b_g = xbc[:, D_MODEL + g * SSD_STATE:D_MODEL + (g + 1) * SSD_STATE].astype(BF16)
        c_g = xbc[:, D_MODEL + GROUPS * SSD_STATE + g * SSD_STATE:
                  D_MODEL + GROUPS * SSD_STATE + (g + 1) * SSD_STATE].astype(BF16)
        cb_causal = jnp.where(causal, _dot_nt(c_g, b_g), 0.0)
        y_off = _dot(c_g, state_ref[g].astype(BF16)) * grow_e[:, seg]
        pieces = []
        for p in range(GDIM // LANES):
            mats = []
            for hh in range(2):
                h = g * (GDIM // HEAD_DIM) + 2 * p + hh
                seg_ij = jnp.minimum(acs[:, h:h + 1] - acs_t[h:h + 1, :], 0.0)
                mats.append((cb_causal * jnp.exp(seg_ij)).astype(BF16))
            x2 = xdt_b[:, g * GDIM + p * LANES:g * GDIM + (p + 1) * LANES]
            zero = jnp.zeros_like(x2)
            rhs = jnp.concatenate([jnp.where(lo_half, x2, zero), jnp.where(lo_half, zero, x2)], axis=0)
            pieces.append(_dot(jnp.concatenate(mats, axis=1), rhs))
        y_diag = jnp.concatenate(pieces, axis=1)
        new_states = _dot_tn(b_g, xdec[:, seg])
        state_ref[g] = state_ref[g] * chunk_decay[:, seg] + new_states
        y = y_diag + y_off + xs[:, seg] * dskip_ref[:, seg]
        y = y * gate[:, seg]
        y = y * lax.rsqrt(jnp.mean(y * y, axis=-1, keepdims=True) + EPS)
        o_ref[:, D_MODEL + g * GDIM:D_MODEL + (g + 1) * GDIM] = (y * snorm_ref[:, seg]).astype(BF16)


def _even_mix(uv, rest, dt_raw, ln_g, ln_b, ws, bs_t, conv_w, conv_b, dt_bias, a_log, d_skip_e, ssd_norm, e3,
              shift):
    m = uv.shape[0]
    full = lambda shape: pl.BlockSpec(shape, lambda c: (0,) * len(shape))
    z_block = CONV_DIM // EVEN_BN
    return pl.pallas_call(
        _even_mix_body,
        grid=(m // CHUNK,),
        in_specs=[
            pl.BlockSpec((CHUNK, D_MODEL), lambda c: (c, 0)),
            pl.BlockSpec((CHUNK, D_MODEL), lambda c: (c, 1)),
            pl.BlockSpec((CHUNK, EVEN_BN), lambda c: (c, z_block)),
            pl.BlockSpec((CHUNK, EVEN_BN), lambda c: (c, z_block + 1)),
            pl.BlockSpec((CHUNK, CONV_DIM), lambda c: (c, 0)),
            pl.BlockSpec((CHUNK, LANES), lambda c: (c, 0)),
            full((1, D_MODEL)), full((1, D_MODEL)),
            full((GROUPS, CHUNK, CHUNK)), full((CHUNK, GROUPS)),
            full((SSD_CONV, CONV_DIM)), full((1, CONV_DIM)),
            full((1, LANES)), full((1, LANES)),
            full((1, D_MODEL)), full((1, D_MODEL)),
            full((3 * LANES, D_MODEL)),
            full(((SSD_CONV - 1) * CHUNK, 2 * CHUNK)),
        ],
        out_specs=pl.BlockSpec((CHUNK, 2 * D_MODEL), lambda c: (c, 0)),
        out_shape=jax.ShapeDtypeStruct((m, 2 * D_MODEL), BF16),
        scratch_shapes=[
            pltpu.VMEM((GROUPS, SSD_STATE, GDIM), F32),
            pltpu.VMEM((2 * CHUNK, CONV_DIM), BF16),
        ],
        compiler_params=_params(("arbitrary",), 48),
        name="even_mix",
    )(uv, uv, rest, rest, rest, dt_raw, ln_g, ln_b, ws, bs_t, conv_w, conv_b, dt_bias, a_log, d_skip_e, ssd_norm,
      e3, shift)


def _swa_body(sink_ref, q_ref, kv_ref, kvp_ref, o_ref):
    n = pl.program_id(0)
    w = CHUNK
    row = lax.broadcasted_iota(jnp.int32, (w, w), 0)
    col = lax.broadcasted_iota(jnp.int32, (w, w), 1)
    own = col <= row
    lo_half = lax.broadcasted_iota(jnp.int32, (w, LANES), 1) < HEAD_DIM
    prev_bias = jnp.where(n > 0, 0.0, -jnp.inf)

    def head_tiles(ref, base, k):
        t = ref[:, base + (k // 2) * LANES:base + (k // 2 + 1) * LANES].astype(F32)
        r = pltpu.roll(t, HEAD_DIM, 1)
        return (t, r) if k % 2 == 0 else (r, t)

    for k in range(GROUPS):
        k_lo, k_hi = head_tiles(kv_ref, 0, k)
        kp_lo, kp_hi = head_tiles(kvp_ref, 0, k)
        v_lo, v_hi = head_tiles(kv_ref, KV_WIDTH, k)
        vp_lo, vp_hi = head_tiles(kvp_ref, KV_WIDTH, k)
        kk = jnp.where(lo_half, k_lo, k_hi).astype(BF16)
        kkp = jnp.where(lo_half, kp_lo, kp_hi).astype(BF16)
        out = []
        for parity in range(2):
            if parity == 0:
                vv = jnp.concatenate([jnp.where(lo_half, v_lo, 1.0), jnp.where(lo_half, vp_lo, 1.0)], axis=0)
            else:
                vv = jnp.concatenate([jnp.where(lo_half, 1.0, v_hi), jnp.where(lo_half, 1.0, vp_hi)], axis=0)
            lhs = []
            for p in range(ATT_REP // 2):
                q2 = q_ref[:, k * GDIM + p * LANES:k * GDIM + (p + 1) * LANES]
                zero = jnp.zeros_like(q2)
                lhs.append(jnp.where(lo_half, q2, zero) if parity == 0 else jnp.where(lo_half, zero, q2))
            lhs = jnp.concatenate(lhs, axis=0)
            s_own = _dot_nt(lhs, kk)
            s_prev = _dot_nt(lhs, kkp)
            probs, esink = [], []
            for p in range(ATT_REP // 2):
                sink = sink_ref[k * ATT_REP + 2 * p + parity]
                s = jnp.where(own, s_own[p * w:(p + 1) * w], s_prev[p * w:(p + 1) * w] + prev_bias)
                mx = jnp.maximum(jnp.max(s, axis=-1, keepdims=True), sink)
                e = jnp.exp(s - mx)
                probs.append(jnp.concatenate([jnp.where(own, e, 0.0).astype(BF16),
                                              jnp.where(own, 0.0, e).astype(BF16)], axis=1))
                esink.append(jnp.exp(sink - mx))
            o = _dot(jnp.concatenate(probs, axis=0), vv.astype(BF16))
            out.append((o, esink))
        for p in range(ATT_REP // 2):
            o_even = out[0][0][p * w:(p + 1) * w]
            o_odd = out[1][0][p * w:(p + 1) * w]
            num = jnp.where(lo_half, o_even, o_odd)
            den = pltpu.roll(jnp.where(lo_half, o_odd, o_even), HEAD_DIM, 1)
            den = den + jnp.where(lo_half, out[0][1][p], out[1][1][p])
            o_ref[:, k * GDIM + p * LANES:k * GDIM + (p + 1) * LANES] = (num / den).astype(BF16)


def _swa(qkv, sinks):
    m = qkv.shape[0]
    kv_block = ATT_HEADS * HEAD_DIM // (2 * KV_WIDTH)
    return pl.pallas_call(
        _swa_body,
        grid=(m // CHUNK,),
        in_specs=[
            pl.BlockSpec(memory_space=pltpu.SMEM),
            pl.BlockSpec((CHUNK, D_MODEL), lambda n: (n, 0)),
            pl.BlockSpec((CHUNK, 2 * KV_WIDTH), lambda n: (n, kv_block)),
            pl.BlockSpec((CHUNK, 2 * KV_WIDTH), lambda n: (jnp.maximum(n - 1, 0), kv_block)),
        ],
        out_specs=pl.BlockSpec((CHUNK, D_MODEL), lambda n: (n, 0)),
        out_shape=jax.ShapeDtypeStruct((m, D_MODEL), BF16),
        compiler_params=_params(("parallel",), 32),
        name="swa",
    )(sinks, qkv, qkv, qkv)


def _xattn_body(x_ref, g_ref, wq_ref, kv_ref, wo_ref, gn_ref, o_ref, hn_ref, wq_b, wo_b):
    @pl.when(pl.program_id(0) == 0)
    def _():
        wq_b[...] = wq_ref[...].astype(BF16)
        wo_b[...] = wo_ref[...].astype(BF16)

    x = x_ref[...]
    h = _rms(x, g_ref[...]).astype(BF16)
    q = _dot(h, wq_b[...]).astype(BF16)
    outs = []
    for hd in range(X_HEADS):
        seg = slice(hd * X_HEAD_DIM, (hd + 1) * X_HEAD_DIM)
        k = kv_ref[:, seg]
        v = kv_ref[:, X_WIDTH + hd * X_HEAD_DIM:X_WIDTH + (hd + 1) * X_HEAD_DIM]
        s = _dot_nt(q[:, seg], k) * X_SCALE
        e = jnp.exp(s - jnp.max(s, axis=-1, keepdims=True))
        o = _dot(e.astype(BF16), v) * (1.0 / jnp.sum(e, axis=-1, keepdims=True))
        outs.append(o.astype(BF16))
    y = x + _dot(jnp.concatenate(outs, axis=1), wo_b[...])
    o_ref[...] = y
    hn_ref[...] = _rms(y, gn_ref[...]).astype(BF16)


def _xattn(x, gain, w_q, kv, w_o, layer, next_gain):
    m = x.shape[0]
    bm = 512
    row = pl.BlockSpec((bm, D_MODEL), lambda i: (i, 0))
    vec = pl.BlockSpec((1, D_MODEL), lambda i: (0, 0))
    return pl.pallas_call(
        _xattn_body,
        grid=(m // bm,),
        in_specs=[
            row, vec,
            pl.BlockSpec((None, D_MODEL, X_WIDTH), lambda i: (layer, 0, 0)),
            pl.BlockSpec((N_MEM, 2 * X_WIDTH), lambda i: (0, 0)),
            pl.BlockSpec((None, X_WIDTH, D_MODEL), lambda i: (layer, 0, 0)),
            vec,
        ],
        out_specs=[row, row],
        out_shape=[jax.ShapeDtypeStruct((m, D_MODEL), F32), jax.ShapeDtypeStruct((m, D_MODEL), BF16)],
        scratch_shapes=[pltpu.VMEM((D_MODEL, X_WIDTH), BF16), pltpu.VMEM((X_WIDTH, D_MODEL), BF16)],
        compiler_params=_params(("arbitrary",), 48),
        name="xattn",
    )(x, gain, w_q, kv, w_o, next_gain)


def _ffn(x, h, gain, w_gu, w_down, layer, name, more_casts, out_gain, gain_use):
    casts = [(w_down, layer)] + list(more_casts)
    if h is None:
        act, w_down_b, *copies = _ffn_up(x, gain.reshape(1, -1), w_gu, layer, casts)
    else:
        act, w_down_b, *copies = _ffn_up(h, None, w_gu, layer, casts)
    out = _mm_rows(act, w_down_b, x, 0.5, name, out_gain.reshape(1, -1), gain_use)
    y, h_next = out if gain_use == "next" else (out, None)
    return y, h_next, copies


def _pad_lanes(v):
    return jnp.pad(v.reshape(1, -1), ((0, 0), (0, LANES - v.shape[-1])))


def kernel(x, mem, positions, norm_ffn1, w_ffn1_gu, w_ffn1_down, norm_mix, w_in_even, gm_ln_g, gm_ln_b, gm_ws, gm_bs, conv_w, conv_b, dt_bias, a_log, d_skip, ssd_norm, w_out_even, w_qkv, b_qkv, sinks, w_o_odd, norm_xq, norm_mem, w_xq, w_xkv, w_xo, norm_ffn2, w_ffn2_gu, w_ffn2_down, final_norm):
    bsz, seq, d = x.shape
    assert (bsz, seq, d) == (1, SEQ, D_MODEL)
    xr = x.reshape(seq, d)
    memr = mem.reshape(N_MEM, d)
    inv_freq = ROPE_THETA ** (-jnp.arange(0, ROT_DIM, 2, dtype=F32) / ROT_DIM)
    cos_t, sin_t = _rope_table(positions.reshape(1, seq), inv_freq.reshape(ROT_HALF, 1))
    head_of_lane = jnp.arange(D_MODEL, dtype=jnp.int32) // HEAD_DIM
    e1 = (jnp.arange(LANES, dtype=jnp.int32)[:, None] == head_of_lane[None, :]).astype(BF16)
    e3 = jnp.concatenate([e1, e1, e1], axis=0)
    sel_row = jnp.arange((SSD_CONV - 1) * CHUNK, dtype=jnp.int32)[:, None]
    sel_col = jnp.arange(2 * CHUNK, dtype=jnp.int32)[None, :]
    shift = (sel_col == CHUNK + sel_row % CHUNK - (SSD_CONV - 1) + sel_row // CHUNK).astype(BF16)

    w_in_t = jnp.swapaxes(w_in_even, 1, 2)

    w_gu1 = w_ffn1_gu
    h = None
    for i in range(DEPTH):
        j = i // 2
        xr, h, (w_gu2,) = _ffn(xr, h, norm_ffn1[i], w_gu1, w_ffn1_down, i, "ffn1_down",
                               [(w_ffn2_gu, i)], norm_mix[i], "next")
        if i % 2 == 0:
            uv, rest, dt_raw, w_out_b = _even_in(h, w_in_t, w_out_even, j)
            mix = _even_mix(
                uv, rest, dt_raw, gm_ln_g[j].reshape(1, -1), gm_ln_b[j].reshape(1, -1), gm_ws[j], gm_bs[j].T,
                conv_w[j], conv_b[j].reshape(1, -1), _pad_lanes(dt_bias[j]), _pad_lanes(a_log[j]),
                jnp.repeat(d_skip[j], HEAD_DIM).reshape(1, -1), ssd_norm[j].reshape(1, -1), e3, shift)
            xr = _mm_rows(mix, w_out_b, xr, 1.0, "even_out")
        else:
            qkv, w_o_b = _qkv(h, w_qkv_b, w_o_odd, j, b_qkv[j].reshape(1, -1), cos_t, sin_t)
            att = _swa(qkv, sinks[j])
            xr = _mm_rows(att, w_o_b, xr, 1.0, "odd_out")
        kv = _norm_mm(memr, norm_mem[i].reshape(1, -1), w_xkv, i, N_MEM, BF16, "mem_kv")
        xr, h = _xattn(xr, norm_xq[i].reshape(1, -1), w_xq, kv, w_xo, i, norm_ffn2[i].reshape(1, -1))
        last = i + 1 == DEPTH
        ahead = []
        if not last:
            ahead.append((w_ffn1_gu, i + 1))
            if (i + 1) % 2 == 1:
                ahead.append((w_qkv, (i + 1) // 2))
        xr, h, copies = _ffn(xr, h, norm_ffn2[i], w_gu2, w_ffn2_down, i, "ffn2_down", ahead,
                             final_norm if last else norm_ffn1[i + 1], "final" if last else "next")
        if copies:
            w_gu1 = copies[0]
            w_qkv_b = copies[1] if len(copies) > 1 else None
    return xr.reshape(bsz, seq, d)
```

```python
import functools

import jax
import jax.numpy as jnp
from jax import lax
from jax.experimental import pallas as pl
from jax.experimental.pallas import tpu as pltpu

F32 = jnp.float32
BF16 = jnp.bfloat16

D_MODEL = 2048
SEQ = 8192
DEPTH = 2
EPS = 1e-5
N_MEM = 256
D_FF = 5632
CHUNK = 128
GROUPS = 4
GDIM = D_MODEL // GROUPS
HEAD_DIM = 64
SSD_HEADS = 32
SSD_STATE = 128
SSD_CONV = 4
CONV_DIM = D_MODEL + 2 * GROUPS * SSD_STATE
EVEN_MAIN = 2 * D_MODEL + D_MODEL + CONV_DIM
ATT_HEADS = 32
ATT_REP = ATT_HEADS // GROUPS
ATT_SCALE = HEAD_DIM ** -0.5
ROT_DIM = HEAD_DIM // 4
ROT_HALF = ROT_DIM // 2
ROPE_THETA = 500000.0
KV_WIDTH = GROUPS * HEAD_DIM
ODD_IN = (ATT_HEADS + 2 * GROUPS) * HEAD_DIM
X_HEADS = 4
X_HEAD_DIM = 128
X_WIDTH = X_HEADS * X_HEAD_DIM
X_SCALE = X_HEAD_DIM ** -0.5

LANES = 128
SUBLANES = 8
MXU_COLS = 256
FFN_SLAB = 1024
CONV_TAIL = 16
BM = 1024
BN = 512
MIB = 1024 * 1024


def _params(semantics, vmem_mib):
    return pltpu.CompilerParams(dimension_semantics=semantics, vmem_limit_bytes=vmem_mib * MIB)


def _rms(x, g):
    ms = jnp.mean(x * x, axis=-1, keepdims=True)
    return x * lax.rsqrt(ms + EPS) * g


def _silu(x):
    return x * jax.nn.sigmoid(x)


def _gelu(x):
    return 0.5 * x * (1.0 + lax.erf(x * (2.0 ** -0.5)))


def _dot(a, b):
    return jnp.dot(a, b, preferred_element_type=F32)


def _dot_nt(a, b):
    return lax.dot_general(a, b, (((1,), (1,)), ((), ())), preferred_element_type=F32)


def _dot_tn(a, b):
    return lax.dot_general(a, b, (((0,), (0,)), ((), ())), preferred_element_type=F32)


def _snake(i, j, nj):
    return jnp.where(i % 2 == 0, j, nj - 1 - j)


def _cast_specs(w, layer, rows, steps_per_row_block):
    _, r, c = w.shape
    n_slabs = pl.cdiv(r, rows)
    slab = lambda i, j: jnp.minimum(i * steps_per_row_block + j, n_slabs - 1)
    return (pl.BlockSpec((None, rows, c), lambda i, j: (layer, slab(i, j), 0)),
            pl.BlockSpec((rows, c), lambda i, j: (slab(i, j), 0)),
            jax.ShapeDtypeStruct((r, c), BF16))


def _slab_rows(w, steps):
    tile = 2 * SUBLANES
    return tile * pl.cdiv(w.shape[1], tile * steps)


def _ffn_up_body(*refs, n_cast, normed_input):
    n_in = 3 if normed_input else 4
    wg_ref, wu_ref = refs[n_in - 2:n_in]
    cast_in, o_ref = refs[n_in:n_in + n_cast], refs[n_in + n_cast]
    cast_out = refs[n_in + n_cast + 1:n_in + 2 * n_cast + 1]
    scratch = refs[n_in + 2 * n_cast + 1:]
    w_ref = scratch[0]
    if normed_input:
        h_ref = refs[0]
    else:
        x_ref, g_ref, h_ref = refs[0], refs[1], scratch[1]

        @pl.when(pl.program_id(1) == 0)
        def _():
            h_ref[...] = _rms(x_ref[...], g_ref[...]).astype(BF16)

    for src, dst in zip(cast_in, cast_out):
        dst[...] = src[...].astype(BF16)
    groups = BN // MXU_COLS
    for c in range(groups):
        w_ref[:, (2 * c) * MXU_COLS:(2 * c + 1) * MXU_COLS] = wg_ref[:, c * MXU_COLS:(c + 1) * MXU_COLS].astype(BF16)
        w_ref[:, (2 * c + 1) * MXU_COLS:(2 * c + 2) * MXU_COLS] = wu_ref[:, c * MXU_COLS:(c + 1) * MXU_COLS].astype(BF16)
    rows = o_ref.shape[0]
    for r0 in range(0, rows, FFN_SLAB):
        gu = _dot(h_ref[r0:r0 + FFN_SLAB, :], w_ref[...])
        for c in range(groups):
            g = gu[:, (2 * c) * MXU_COLS:(2 * c + 1) * MXU_COLS]
            u = gu[:, (2 * c + 1) * MXU_COLS:(2 * c + 2) * MXU_COLS]
            o_ref[r0:r0 + FFN_SLAB, c * MXU_COLS:(c + 1) * MXU_COLS] = (_silu(g) * u).astype(BF16)


def _ffn_up(xh, gain, w_gu, layer, cast_weights):
    m = xh.shape[0]
    normed_input = gain is None
    bm = 2 * BM if normed_input else BM
    nj = D_FF // BN
    steps = (m // bm) * nj
    if w_gu.ndim == 3:
        w_spec = lambda off: pl.BlockSpec((None, D_MODEL, BN), lambda i, j: (layer, 0, _snake(i, j, nj) + off))
    else:
        w_spec = lambda off: pl.BlockSpec((D_MODEL, BN), lambda i, j: (0, _snake(i, j, nj) + off))
    cast_specs = [_cast_specs(w, l, _slab_rows(w, steps), nj) for w, l in cast_weights]
    row_specs = [pl.BlockSpec((bm, D_MODEL), lambda i, j: (i, 0))]
    scratch = [pltpu.VMEM((D_MODEL, 2 * BN), BF16)]
    operands = [xh]
    if not normed_input:
        row_specs.append(pl.BlockSpec((1, D_MODEL), lambda i, j: (0, 0)))
        scratch.append(pltpu.VMEM((bm, D_MODEL), BF16))
        operands.append(gain)
    return pl.pallas_call(
        functools.partial(_ffn_up_body, n_cast=len(cast_weights), normed_input=normed_input),
        grid=(m // bm, nj),
        in_specs=row_specs + [w_spec(0), w_spec(nj)] + [s[0] for s in cast_specs],
        out_specs=[pl.BlockSpec((bm, BN), lambda i, j: (i, _snake(i, j, nj)))] + [s[1] for s in cast_specs],
        out_shape=[jax.ShapeDtypeStruct((m, D_FF), BF16)] + [s[2] for s in cast_specs],
        scratch_shapes=scratch,
        compiler_params=_params(("arbitrary", "arbitrary"), 56),
        name="ffn_up",
    )(*operands, w_gu, w_gu, *[w for w, _ in cast_weights])


def _mm_rows_body(a_ref, w_ref, r_ref, *refs, scale, gain_use):
    y = r_ref[...] + scale * _dot(a_ref[...], w_ref[...])
    o_ref = refs[1] if gain_use else refs[0]
    if gain_use == "final":
        y = _rms(y, refs[0][...])
    elif gain_use == "next":
        refs[2][...] = _rms(y, refs[0][...]).astype(BF16)
    o_ref[...] = y


ROWS_BM = 512


def _mm_rows(a, w, res, scale, name, gain=None, gain_use=None):
    m, k = a.shape
    n = w.shape[1]
    row_out = pl.BlockSpec((ROWS_BM, n), lambda i: (i, 0))
    out_specs, out_shape = [row_out], [jax.ShapeDtypeStruct((m, n), F32)]
    if gain_use == "next":
        out_specs.append(row_out)
        out_shape.append(jax.ShapeDtypeStruct((m, n), BF16))
    gains = [] if gain is None else [gain]
    out = pl.pallas_call(
        functools.partial(_mm_rows_body, scale=scale, gain_use=gain_use),
        grid=(m // ROWS_BM,),
        in_specs=[
            pl.BlockSpec((ROWS_BM, k), lambda i: (i, 0)),
            pl.BlockSpec((k, n), lambda i: (0, 0), pipeline_mode=pl.Buffered(1)),
            pl.BlockSpec((ROWS_BM, n), lambda i: (i, 0)),
        ] + [pl.BlockSpec((1, n), lambda i: (0, 0))] * len(gains),
        out_specs=out_specs,
        out_shape=out_shape,
        compiler_params=_params(("arbitrary",), 60),
        name=name,
    )(a, w, res, *gains)
    return out if gain_use == "next" else out[0]


def _norm_mm_body(x_ref, g_ref, w_ref, o_ref, h_ref):
    @pl.when(pl.program_id(1) == 0)
    def _():
        h_ref[...] = _rms(x_ref[...], g_ref[...]).astype(BF16)

    o_ref[...] = _dot(h_ref[...], w_ref[...].astype(BF16)).astype(o_ref.dtype)


def _norm_mm(x, gain, w, layer, bm, out_dtype, name):
    m, k = x.shape
    n = w.shape[2]
    return pl.pallas_call(
        _norm_mm_body,
        grid=(m // bm, n // BN),
        in_specs=[
            pl.BlockSpec((bm, k), lambda i, j: (i, 0)),
            pl.BlockSpec((1, k), lambda i, j: (0, 0)),
            pl.BlockSpec((None, k, BN), lambda i, j: (layer, 0, j)),
        ],
        out_specs=pl.BlockSpec((bm, BN), lambda i, j: (i, j)),
        out_shape=jax.ShapeDtypeStruct((m, n), out_dtype),
        scratch_shapes=[pltpu.VMEM((bm, k), BF16)],
        compiler_params=_params(("parallel", "arbitrary"), 40),
        name=name,
    )(x, gain, w)


EVEN_BN = 2 * BN
QKV_SLAB = 256


def _row_slabs(ref, slab):
    return [slice(r0, r0 + slab) for r0 in range(0, ref.shape[0], slab)]


def _even_in_body(h_ref, w_ref, wdt_ref, wo_ref, uv_ref, rest_ref, dt_ref, wo_out_ref, *, nj, n_uv):
    @pl.when(pl.program_id(1) == 0)
    def _():
        row = lax.broadcasted_iota(jnp.int32, (LANES, 1), 0)
        dt_ref[...] = _dot_nt(h_ref[...], jnp.where(row < SSD_HEADS, wdt_ref[...], 0.0).astype(BF16))

    wo_out_ref[...] = wo_ref[...].astype(BF16)
    col = _snake(pl.program_id(0), pl.program_id(1), nj)

    @pl.when(col < n_uv)
    def _():
        uv_ref[...] = _gelu(_dot_nt(h_ref[...], w_ref[...].astype(BF16))).astype(BF16)

    @pl.when(col >= n_uv)
    def _():
        rest_ref[...] = _dot_nt(h_ref[...], w_ref[...].astype(BF16)).astype(BF16)


def _even_in(h, w_in_t, w_out, layer):
    m = h.shape[0]
    bn = EVEN_BN
    n_uv, n_xbc, n_z = 2 * D_MODEL // bn, CONV_DIM // bn, D_MODEL // bn
    nj = n_uv + n_xbc + n_z
    z_first, xbc_first = n_uv, n_uv + n_z

    def w_block(i, j):
        c = _snake(i, j, nj)
        return jnp.where(c < n_uv, c, jnp.where(c < n_uv + n_xbc, xbc_first + c - n_uv, z_first + c - n_uv - n_xbc))

    uv_col = lambda i, j: jnp.minimum(_snake(i, j, nj), n_uv - 1)
    rest_col = lambda i, j: jnp.maximum(_snake(i, j, nj) - n_uv, 0)
    wo_in_spec, wo_out_spec, wo_shape = _cast_specs(w_out, layer, _slab_rows(w_out, (m // BM) * nj), nj)
    return pl.pallas_call(
        functools.partial(_even_in_body, nj=nj, n_uv=n_uv),
        grid=(m // BM, nj),
        in_specs=[pl.BlockSpec((BM, D_MODEL), lambda i, j: (i, 0)),
                  pl.BlockSpec((None, bn, D_MODEL), lambda i, j: (layer, w_block(i, j), 0)),
                  pl.BlockSpec((None, LANES, D_MODEL), lambda i, j: (layer, EVEN_MAIN // LANES, 0)),
                  wo_in_spec],
        out_specs=[pl.BlockSpec((BM, bn), lambda i, j: (i, uv_col(i, j))),
                   pl.BlockSpec((BM, bn), lambda i, j: (i, rest_col(i, j))),
                   pl.BlockSpec((BM, LANES), lambda i, j: (i, 0)),
                   wo_out_spec],
        out_shape=[jax.ShapeDtypeStruct((m, 2 * D_MODEL), BF16),
                   jax.ShapeDtypeStruct((m, CONV_DIM + D_MODEL), BF16),
                   jax.ShapeDtypeStruct((m, LANES), F32),
                   wo_shape],
        compiler_params=_params(("arbitrary", "arbitrary"), 56),
        name="even_in",
    )(h, w_in_t, w_in_t, w_out)


def _rope_table_body(pos_ref, invf_ref, cos_ref, sin_ref):
    ang = pos_ref[...].astype(F32) * invf_ref[...]
    cos_ref[...] = jnp.cos(ang)
    sin_ref[...] = jnp.sin(ang)


def _rope_table(pos_row, invf_col):
    shape = jax.ShapeDtypeStruct((ROT_HALF, pos_row.shape[1]), F32)
    return pl.pallas_call(_rope_table_body, out_shape=[shape, shape], name="rope_table")(pos_row, invf_col)


def _qkv_body(h_ref, w_ref, b_ref, cost_ref, sint_ref, wo_ref, o_ref, wo_out_ref,
              cos_ref, sn_ref, sp_ref, *, nj):
    j = pl.program_id(1)

    @pl.when(j == 0)
    def _():
        reps = LANES // ROT_HALF
        cos = jnp.concatenate([cost_ref[...]] * reps, axis=0).T
        sin = jnp.concatenate([sint_ref[...]] * reps, axis=0).T
        lane = lax.broadcasted_iota(jnp.int32, (1, LANES), 1) % HEAD_DIM
        first = lane < ROT_HALF
        second = (lane >= ROT_HALF) & (lane < ROT_DIM)
        cos_ref[...] = jnp.where(first | second, cos, 1.0)
        sn_ref[...] = jnp.where(first, -sin, 0.0)
        sp_ref[...] = jnp.where(second, sin, 0.0)

    wo_out_ref[...] = wo_ref[...].astype(BF16)

    col_block = _snake(pl.program_id(0), j, nj)
    tiles = o_ref.shape[1] // LANES

    def kind(tile):
        return "q" if tile < ATT_HEADS * HEAD_DIM // LANES else "k" if tile < (ODD_IN - KV_WIDTH) // LANES else "v"

    for rows in _row_slabs(o_ref, QKV_SLAB):
        acc = _dot(h_ref[rows, :], w_ref[...]) + b_ref[...]

        def rope(a):
            return (a * cos_ref[rows, :] + pltpu.roll(a, LANES - ROT_HALF, 1) * sn_ref[rows, :]
                    + pltpu.roll(a, ROT_HALF, 1) * sp_ref[rows, :])

        for t in range(tiles):
            a = acc[:, t * LANES:(t + 1) * LANES]
            kinds = [kind(jb * tiles + t) for jb in range(nj)]
            roped = rope(a) if set(kinds) != {"v"} else None
            by_kind = {"q": lambda: roped * ATT_SCALE, "k": lambda: roped, "v": lambda: a}
            val = by_kind[kinds[-1]]()
            for jb in range(nj - 2, -1, -1):
                if kinds[jb] != kinds[jb + 1]:
                    val = jnp.where(col_block <= jb, by_kind[kinds[jb]](), val)
            o_ref[rows, t * LANES:(t + 1) * LANES] = val.astype(BF16)


def _qkv(h, w_b, w_o, layer, b, cos_t, sin_t):
    m = h.shape[0]
    nj = 2
    bn = ODD_IN // nj
    wo_in_spec, wo_out_spec, wo_shape = _cast_specs(w_o, layer, 128, nj)
    return pl.pallas_call(
        functools.partial(_qkv_body, nj=nj),
        grid=(m // BM, nj),
        in_specs=[
            pl.BlockSpec((BM, D_MODEL), lambda i, j: (i, 0)),
            pl.BlockSpec((D_MODEL, bn), lambda i, j: (0, _snake(i, j, nj))),
            pl.BlockSpec((1, bn), lambda i, j: (0, _snake(i, j, nj))),
            pl.BlockSpec((ROT_HALF, BM), lambda i, j: (0, i)),
            pl.BlockSpec((ROT_HALF, BM), lambda i, j: (0, i)),
            wo_in_spec,
        ],
        out_specs=[pl.BlockSpec((BM, bn), lambda i, j: (i, _snake(i, j, nj))), wo_out_spec],
        out_shape=[jax.ShapeDtypeStruct((m, ODD_IN), BF16), wo_shape],
        scratch_shapes=[
            pltpu.VMEM((BM, LANES), F32),
            pltpu.VMEM((BM, LANES), F32),
            pltpu.VMEM((BM, LANES), F32),
        ],
        compiler_params=_params(("arbitrary", "arbitrary"), 56),
        name="qkv_rope",
    )(h, w_b, b, cos_t, sin_t, w_o)


def _split3(x):
    hi = x.astype(BF16)
    r1 = x - hi.astype(F32)
    mid = r1.astype(BF16)
    lo = (r1 - mid.astype(F32)).astype(BF16)
    return hi, mid, lo


def _even_mix_body(u_ref, v_ref, z0_ref, z1_ref, xbc_ref, dt_ref, lng_ref, lnb_ref, ws_ref, bs_ref, cw_ref, cb_ref,
                   dtb_ref, alog_ref, dskip_ref, snorm_ref, e3_ref, shift_ref, o_ref, state_ref, xx_ref):
    c = pl.program_id(0)
    q = CHUNK

    @pl.when(c == 0)
    def _():
        state_ref[...] = jnp.zeros_like(state_ref)
        xx_ref[pl.ds(0, q), :] = jnp.zeros((q, CONV_DIM), BF16)

    row = lax.broadcasted_iota(jnp.int32, (q, q), 0)
    col = lax.broadcasted_iota(jnp.int32, (q, q), 1)
    causal = col <= row

    for g in range(GROUPS):
        seg = slice(g * GDIM, (g + 1) * GDIM)
        vg = v_ref[:, seg].astype(F32)
        mu = jnp.mean(vg, axis=-1, keepdims=True)
        d = vg - mu
        var = jnp.mean(d * d, axis=-1, keepdims=True)
        vn = d * lax.rsqrt(var + EPS) * lng_ref[:, seg] + lnb_ref[:, seg]
        w = jnp.where(causal, ws_ref[g], 0.0).astype(BF16)
        s = _dot(w, vn.astype(BF16)) + bs_ref[:, g:g + 1]
        o_ref[:, seg] = (u_ref[:, seg].astype(F32) * s).astype(BF16)

    x_cur = xbc_ref[...]
    xx_ref[pl.ds(q, q), :] = x_cur
    shifted = _dot(shift_ref[...], xx_ref[...])
    conv = cb_ref[...] + cw_ref[SSD_CONV - 1:SSD_CONV, :] * x_cur.astype(F32)
    for k in range(SSD_CONV - 1):
        conv = conv + cw_ref[k:k + 1, :] * shifted[k * q:(k + 1) * q]
    xx_ref[pl.ds(q - CONV_TAIL, CONV_TAIL), :] = x_cur[q - CONV_TAIL:, :]
    xbc = _silu(conv)
    xs = xbc[:, :D_MODEL]

    dt = jax.nn.softplus(dt_ref[...] + dtb_ref[...])
    a = dt * (-jnp.exp(alog_ref[...]))
    tri = jnp.where(causal, 1.0, 0.0).astype(BF16)
    a_hi, a_mid, a_lo = _split3(a)
    acs = _dot(tri, a_hi) + _dot(tri, a_mid) + _dot(tri, a_lo)
    acs_t = acs.T
    both = jnp.concatenate([dt, acs], axis=0)
    b_hi, b_mid, b_lo = _split3(both)
    both_e = _dot(jnp.concatenate([b_hi, b_mid, b_lo], axis=1), e3_ref[...])
    dt_e = both_e[:q]
    acs_e = both_e[q:]
    last_e = acs_e[q - 1:q, :]
    xdt = xs * dt_e
    xdec = (xdt * jnp.exp(last_e - acs_e)).astype(BF16)
    xdt_b = xdt.astype(BF16)
    grow_e = jnp.exp(acs_e)
    chunk_decay = jnp.exp(last_e)

    lane = lax.broadcasted_iota(jnp.int32, (q, LANES), 1)
    lo_half = lane < HEAD_DIM
    gate = _silu(jnp.concatenate([z0_ref[...], z1_ref[...]], axis=1).astype(F32))

    for g in range(GROUPS):
        seg = slice(g * GDIM, (g + 1) * GDIM)
        b_g = xbc[:, D_MODEL + g * SSD_STATE:D_MODEL + (g + 1) * SSD_STATE].astype(BF16)
        c_g = xbc[:, D_MODEL + GROUPS * SSD_STATE + g * SSD_STATE:
                  D_MODEL + GROUPS * SSD_STATE + (g + 1) * SSD_STATE].astype(BF16)
        cb_causal = jnp.where(causal, _dot_nt(c_g, b_g), 0.0)
        y_off = _dot(c_g, state_ref[g].astype(BF16)) * grow_e[:, seg]
        pieces = []
        for p in range(GDIM // LANES):
            mats = []
            for hh in range(2):
                h = g * (GDIM // HEAD_DIM) + 2 * p + hh
                seg_ij = jnp.minimum(acs[:, h:h + 1] - acs_t[h:h + 1, :], 0.0)
                mats.append((cb_causal * jnp.exp(seg_ij)).astype(BF16))
            x2 = xdt_b[:, g * GDIM + p * LANES:g * GDIM + (p + 1) * LANES]
            zero = jnp.zeros_like(x2)
            rhs = jnp.concatenate([jnp.where(lo_half, x2, zero), jnp.where(lo_half, zero, x2)], axis=0)
            pieces.append(_dot(jnp.concatenate(mats, axis=1), rhs))
        y_diag = jnp.concatenate(pieces, axis=1)
        new_states = _dot_tn(b_g, xdec[:, seg])
        state_ref[g] = state_ref[g] * chunk_decay[:, seg] + new_states
        y = y_diag + y_off + xs[:, seg] * dskip_ref[:, seg]
        y = y * gate[:, seg]
        y = y * lax.rsqrt(jnp.mean(y * y, axis=-1, keepdims=True) + EPS)
        o_ref[:, D_MODEL + g * GDIM:D_MODEL + (g + 1) * GDIM] = (y * snorm_ref[:, seg]).astype(BF16)


def _even_mix(uv, rest, dt_raw, ln_g, ln_b, ws, bs_t, conv_w, conv_b, dt_bias, a_log, d_skip_e, ssd_norm, e3,
              shift):
    m = uv.shape[0]
    full = lambda shape: pl.BlockSpec(shape, lambda c: (0,) * len(shape))
    z_block = CONV_DIM // EVEN_BN
    return pl.pallas_call(
        _even_mix_body,
        grid=(m // CHUNK,),
        in_specs=[
            pl.BlockSpec((CHUNK, D_MODEL), lambda c: (c, 0)),
            pl.BlockSpec((CHUNK, D_MODEL), lambda c: (c, 1)),
            pl.BlockSpec((CHUNK, EVEN_BN), lambda c: (c, z_block)),
            pl.BlockSpec((CHUNK, EVEN_BN), lambda c: (c, z_block + 1)),
            pl.BlockSpec((CHUNK, CONV_DIM), lambda c: (c, 0)),
            pl.BlockSpec((CHUNK, LANES), lambda c: (c, 0)),
            full((1, D_MODEL)), full((1, D_MODEL)),
            full((GROUPS, CHUNK, CHUNK)), full((CHUNK, GROUPS)),
            full((SSD_CONV, CONV_DIM)), full((1, CONV_DIM)),
            full((1, LANES)), full((1, LANES)),
            full((1, D_MODEL)), full((1, D_MODEL)),
            full((3 * LANES, D_MODEL)),
            full(((SSD_CONV - 1) * CHUNK, 2 * CHUNK)),
        ],
        out_specs=pl.BlockSpec((CHUNK, 2 * D_MODEL), lambda c: (c, 0)),
        out_shape=jax.ShapeDtypeStruct((m, 2 * D_MODEL), BF16),
        scratch_shapes=[
            pltpu.VMEM((GROUPS, SSD_STATE, GDIM), F32),
            pltpu.VMEM((2 * CHUNK, CONV_DIM), BF16),
        ],
        compiler_params=_params(("arbitrary",), 48),
        name="even_mix",
    )(uv, uv, rest, rest, rest, dt_raw, ln_g, ln_b, ws, bs_t, conv_w, conv_b, dt_bias, a_log, d_skip_e, ssd_norm,
      e3, shift)


def _swa_body(sink_ref, q_ref, kv_ref, kvp_ref, o_ref):
    n = pl.program_id(0)
    w = CHUNK
    row = lax.broadcasted_iota(jnp.int32, (w, w), 0)
    col = lax.broadcasted_iota(jnp.int32, (w, w), 1)
    own = col <= row
    lo_half = lax.broadcasted_iota(jnp.int32, (w, LANES), 1) < HEAD_DIM
    prev_bias = jnp.where(n > 0, 0.0, -jnp.inf)

    def head_tiles(ref, base, k):
        t = ref[:, base + (k // 2) * LANES:base + (k // 2 + 1) * LANES].astype(F32)
        r = pltpu.roll(t, HEAD_DIM, 1)
        return (t, r) if k % 2 == 0 else (r, t)

    for k in range(GROUPS):
        k_lo, k_hi = head_tiles(kv_ref, 0, k)
        kp_lo, kp_hi = head_tiles(kvp_ref, 0, k)
        v_lo, v_hi = head_tiles(kv_ref, KV_WIDTH, k)
        vp_lo, vp_hi = head_tiles(kvp_ref, KV_WIDTH, k)
        kk = jnp.where(lo_half, k_lo, k_hi).astype(BF16)
        kkp = jnp.where(lo_half, kp_lo, kp_hi).astype(BF16)
        out = []
        for parity in range(2):
            if parity == 0:
                vv = jnp.concatenate([jnp.where(lo_half, v_lo, 1.0), jnp.where(lo_half, vp_lo, 1.0)], axis=0)
            else:
                vv = jnp.concatenate([jnp.where(lo_half, 1.0, v_hi), jnp.where(lo_half, 1.0, vp_hi)], axis=0)
            lhs = []
            for p in range(ATT_REP // 2):
                q2 = q_ref[:, k * GDIM + p * LANES:k * GDIM + (p + 1) * LANES]
                zero = jnp.zeros_like(q2)
                lhs.append(jnp.where(lo_half, q2, zero) if parity == 0 else jnp.where(lo_half, zero, q2))
            lhs = jnp.concatenate(lhs, axis=0)
            s_own = _dot_nt(lhs, kk)
            s_prev = _dot_nt(lhs, kkp)
            probs, esink = [], []
            for p in range(ATT_REP // 2):
                sink = sink_ref[k * ATT_REP + 2 * p + parity]
                s = jnp.where(own, s_own[p * w:(p + 1) * w], s_prev[p * w:(p + 1) * w] + prev_bias)
                mx = jnp.maximum(jnp.max(s, axis=-1, keepdims=True), sink)
                e = jnp.exp(s - mx)
                probs.append(jnp.concatenate([jnp.where(own, e, 0.0).astype(BF16),
                                              jnp.where(own, 0.0, e).astype(BF16)], axis=1))
                esink.append(jnp.exp(sink - mx))
            o = _dot(jnp.concatenate(probs, axis=0), vv.astype(BF16))
            out.append((o, esink))
        for p in range(ATT_REP // 2):
            o_even = out[0][0][p * w:(p + 1) * w]
            o_odd = out[1][0][p * w:(p + 1) * w]
            num = jnp.where(lo_half, o_even, o_odd)
            den = pltpu.roll(jnp.where(lo_half, o_odd, o_even), HEAD_DIM, 1)
            den = den + jnp.where(lo_half, out[0][1][p], out[1][1][p])
            o_ref[:, k * GDIM + p * LANES:k * GDIM + (p + 1) * LANES] = (num / den).astype(BF16)


def _swa(qkv, sinks):
    m = qkv.shape[0]
    kv_block = ATT_HEADS * HEAD_DIM // (2 * KV_WIDTH)
    return pl.pallas_call(
        _swa_body,
        grid=(m // CHUNK,),
        in_specs=[
            pl.BlockSpec(memory_space=pltpu.SMEM),
            pl.BlockSpec((CHUNK, D_MODEL), lambda n: (n, 0)),
            pl.BlockSpec((CHUNK, 2 * KV_WIDTH), lambda n: (n, kv_block)),
            pl.BlockSpec((CHUNK, 2 * KV_WIDTH), lambda n: (jnp.maximum(n - 1, 0), kv_block)),
        ],
        out_specs=pl.BlockSpec((CHUNK, D_MODEL), lambda n: (n, 0)),
        out_shape=jax.ShapeDtypeStruct((m, D_MODEL), BF16),
        compiler_params=_params(("parallel",), 32),
        name="swa",
    )(sinks, qkv, qkv, qkv)


def _xattn_body(x_ref, g_ref, wq_ref, kv_ref, wo_ref, gn_ref, o_ref, hn_ref, wq_b, wo_b):
    @pl.when(pl.program_id(0) == 0)
    def _():
        wq_b[...] = wq_ref[...].astype(BF16)
        wo_b[...] = wo_ref[...].astype(BF16)

    x = x_ref[...]
    h = _rms(x, g_ref[...]).astype(BF16)
    q = _dot(h, wq_b[...]).astype(BF16)
    outs = []
    for hd in range(X_HEADS):
        seg = slice(hd * X_HEAD_DIM, (hd + 1) * X_HEAD_DIM)
        k = kv_ref[:, seg]
        v = kv_ref[:, X_WIDTH + hd * X_HEAD_DIM:X_WIDTH + (hd + 1) * X_HEAD_DIM]
        s = _dot_nt(q[:, seg], k) * X_SCALE
        e = jnp.exp(s - jnp.max(s, axis=-1, keepdims=True))
        o = _dot(e.astype(BF16), v) * (1.0 / jnp.sum(e, axis=-1, keepdims=True))
        outs.append(o.astype(BF16))
    y = x + _dot(jnp.concatenate(outs, axis=1), wo_b[...])
    o_ref[...] = y
    hn_ref[...] = _rms(y, gn_ref[...]).astype(BF16)


def _xattn(x, gain, w_q, kv, w_o, layer, next_gain):
    m = x.shape[0]
    bm = 512
    row = pl.BlockSpec((bm, D_MODEL), lambda i: (i, 0))
    vec = pl.BlockSpec((1, D_MODEL), lambda i: (0, 0))
    return pl.pallas_call(
        _xattn_body,
        grid=(m // bm,),
        in_specs=[
            row, vec,
            pl.BlockSpec((None, D_MODEL, X_WIDTH), lambda i: (layer, 0, 0)),
            pl.BlockSpec((N_MEM, 2 * X_WIDTH), lambda i: (0, 0)),
            pl.BlockSpec((None, X_WIDTH, D_MODEL), lambda i: (layer, 0, 0)),
            vec,
        ],
        out_specs=[row, row],
        out_shape=[jax.ShapeDtypeStruct((m, D_MODEL), F32), jax.ShapeDtypeStruct((m, D_MODEL), BF16)],
        scratch_shapes=[pltpu.VMEM((D_MODEL, X_WIDTH), BF16), pltpu.VMEM((X_WIDTH, D_MODEL), BF16)],
        compiler_params=_params(("arbitrary",), 48),
        name="xattn",
    )(x, gain, w_q, kv, w_o, next_gain)


def _ffn(x, h, gain, w_gu, w_down, layer, name, more_casts, out_gain, gain_use):
    casts = [(w_down, layer)] + list(more_casts)
    if h is None:
        act, w_down_b, *copies = _ffn_up(x, gain.reshape(1, -1), w_gu, layer, casts)
    else:
        act, w_down_b, *copies = _ffn_up(h, None, w_gu, layer, casts)
    out = _mm_rows(act, w_down_b, x, 0.5, name, out_gain.reshape(1, -1), gain_use)
    y, h_next = out if gain_use == "next" else (out, None)
    return y, h_next, copies


def _pad_lanes(v):
    return jnp.pad(v.reshape(1, -1), ((0, 0), (0, LANES - v.shape[-1])))


def kernel(x, mem, positions, norm_ffn1, w_ffn1_gu, w_ffn1_down, norm_mix, w_in_even, gm_ln_g, gm_ln_b, gm_ws, gm_bs, conv_w, conv_b, dt_bias, a_log, d_skip, ssd_norm, w_out_even, w_qkv, b_qkv, sinks, w_o_odd, norm_xq, norm_mem, w_xq, w_xkv, w_xo, norm_ffn2, w_ffn2_gu, w_ffn2_down, final_norm):
    bsz, seq, d = x.shape
    assert (bsz, seq, d) == (1, SEQ, D_MODEL)
    xr = x.reshape(seq, d)
    memr = mem.reshape(N_MEM, d)
    inv_freq = ROPE_THETA ** (-jnp.arange(0, ROT_DIM, 2, dtype=F32) / ROT_DIM)
    cos_t, sin_t = _rope_table(positions.reshape(1, seq), inv_freq.reshape(ROT_HALF, 1))
    head_of_lane = jnp.arange(D_MODEL, dtype=jnp.int32) // HEAD_DIM
    e1 = (jnp.arange(LANES, dtype=jnp.int32)[:, None] == head_of_lane[None, :]).astype(BF16)
    e3 = jnp.concatenate([e1, e1, e1], axis=0)
    sel_row = jnp.arange((SSD_CONV - 1) * CHUNK, dtype=jnp.int32)[:, None]
    sel_col = jnp.arange(2 * CHUNK, dtype=jnp.int32)[None, :]
    shift = (sel_col == CHUNK + sel_row % CHUNK - (SSD_CONV - 1) + sel_row // CHUNK).astype(BF16)

    w_in_t = jnp.swapaxes(w_in_even, 1, 2)

    w_gu1 = w_ffn1_gu
    h = None
    for i in range(DEPTH):
        j = i // 2
        xr, h, (w_gu2,) = _ffn(xr, h, norm_ffn1[i], w_gu1, w_ffn1_down, i, "ffn1_down",
                               [(w_ffn2_gu, i)], norm_mix[i], "next")
        if i % 2 == 0:
            uv, rest, dt_raw, w_out_b = _even_in(h, w_in_t, w_out_even, j)
            mix = _even_mix(
                uv, rest, dt_raw, gm_ln_g[j].reshape(1, -1), gm_ln_b[j].reshape(1, -1), gm_ws[j], gm_bs[j].T,
                conv_w[j], conv_b[j].reshape(1, -1), _pad_lanes(dt_bias[j]), _pad_lanes(a_log[j]),
                jnp.repeat(d_skip[j], HEAD_DIM).reshape(1, -1), ssd_norm[j].reshape(1, -1), e3, shift)
            xr = _mm_rows(mix, w_out_b, xr, 1.0, "even_out")
        else:
            qkv, w_o_b = _qkv(h, w_qkv_b, w_o_odd, j, b_qkv[j].reshape(1, -1), cos_t, sin_t)
            att = _swa(qkv, sinks[j])
            xr = _mm_rows(att, w_o_b, xr, 1.0, "odd_out")
        kv = _norm_mm(memr, norm_mem[i].reshape(1, -1), w_xkv, i, N_MEM, BF16, "mem_kv")
        xr, h = _xattn(xr, norm_xq[i].reshape(1, -1), w_xq, kv, w_xo, i, norm_ffn2[i].reshape(1, -1))
        last = i + 1 == DEPTH
        ahead = []
        if not last:
            ahead.append((w_ffn1_gu, i + 1))
            if (i + 1) % 2 == 1:
                ahead.append((w_qkv, (i + 1) // 2))
        xr, h, copies = _ffn(xr, h, norm_ffn2[i], w_gu2, w_ffn2_down, i, "ffn2_down", ahead,
                             final_norm if last else norm_ffn1[i + 1], "final" if last else "next")
        if copies:
            w_gu1 = copies[0]
            w_qkv_b = copies[1] if len(copies) > 1 else None
    return xr.reshape(bsz, seq, d)
```

```python
import functools

import jax
import jax.numpy as jnp
from jax import lax
from jax.experimental import pallas as pl
from jax.experimental.pallas import tpu as pltpu

F32 = jnp.float32
BF16 = jnp.bfloat16

D_MODEL = 2048
SEQ = 8192
DEPTH = 2
EPS = 1e-5
N_MEM = 256
D_FF = 5632
CHUNK = 128
GROUPS = 4
GDIM = D_MODEL // GROUPS
HEAD_DIM = 64
SSD_HEADS = 32
SSD_STATE = 128
SSD_CONV = 4
CONV_DIM = D_MODEL + 2 * GROUPS * SSD_STATE
EVEN_MAIN = 2 * D_MODEL + D_MODEL + CONV_DIM
ATT_HEADS = 32
ATT_REP = ATT_HEADS // GROUPS
ATT_SCALE = HEAD_DIM ** -0.5
ROT_DIM = HEAD_DIM // 4
ROT_HALF = ROT_DIM // 2
ROPE_THETA = 500000.0
KV_WIDTH = GROUPS * HEAD_DIM
ODD_IN = (ATT_HEADS + 2 * GROUPS) * HEAD_DIM
X_HEADS = 4
X_HEAD_DIM = 128
X_WIDTH = X_HEADS * X_HEAD_DIM
X_SCALE = X_HEAD_DIM ** -0.5

LANES = 128
SUBLANES = 8
MXU_COLS = 256
FFN_SLAB = 1024
CONV_TAIL = 16
BM = 1024
BN = 512
MIB = 1024 * 1024


def _params(semantics, vmem_mib):
    return pltpu.CompilerParams(dimension_semantics=semantics, vmem_limit_bytes=vmem_mib * MIB)


def _rms(x, g):
    ms = jnp.mean(x * x, axis=-1, keepdims=True)
    return x * lax.rsqrt(ms + EPS) * g


def _silu(x):
    return x * jax.nn.sigmoid(x)


def _gelu(x):
    return 0.5 * x * (1.0 + lax.erf(x * (2.0 ** -0.5)))


def _dot(a, b):
    return jnp.dot(a, b, preferred_element_type=F32)


def _dot_nt(a, b):
    return lax.dot_general(a, b, (((1,), (1,)), ((), ())), preferred_element_type=F32)


def _dot_tn(a, b):
    return lax.dot_general(a, b, (((0,), (0,)), ((), ())), preferred_element_type=F32)


def _snake(i, j, nj):
    return jnp.where(i % 2 == 0, j, nj - 1 - j)


def _cast_specs(w, layer, rows, steps_per_row_block):
    _, r, c = w.shape
    n_slabs = pl.cdiv(r, rows)
    slab = lambda i, j: jnp.minimum(i * steps_per_row_block + j, n_slabs - 1)
    return (pl.BlockSpec((None, rows, c), lambda i, j: (layer, slab(i, j), 0)),
            pl.BlockSpec((rows, c), lambda i, j: (slab(i, j), 0)),
            jax.ShapeDtypeStruct((r, c), BF16))


def _slab_rows(w, steps):
    tile = 2 * SUBLANES
    return tile * pl.cdiv(w.shape[1], tile * steps)


def _ffn_up_body(*refs, n_cast, normed_input):
    n_in = 3 if normed_input else 4
    wg_ref, wu_ref = refs[n_in - 2:n_in]
    cast_in, o_ref = refs[n_in:n_in + n_cast], refs[n_in + n_cast]
    cast_out = refs[n_in + n_cast + 1:n_in + 2 * n_cast + 1]
    scratch = refs[n_in + 2 * n_cast + 1:]
    w_ref = scratch[0]
    if normed_input:
        h_ref = refs[0]
    else:
        x_ref, g_ref, h_ref = refs[0], refs[1], scratch[1]

        @pl.when(pl.program_id(1) == 0)
        def _():
            h_ref[...] = _rms(x_ref[...], g_ref[...]).astype(BF16)

    for src, dst in zip(cast_in, cast_out):
        dst[...] = src[...].astype(BF16)
    groups = BN // MXU_COLS
    for c in range(groups):
        w_ref[:, (2 * c) * MXU_COLS:(2 * c + 1) * MXU_COLS] = wg_ref[:, c * MXU_COLS:(c + 1) * MXU_COLS].astype(BF16)
        w_ref[:, (2 * c + 1) * MXU_COLS:(2 * c + 2) * MXU_COLS] = wu_ref[:, c * MXU_COLS:(c + 1) * MXU_COLS].astype(BF16)
    rows = o_ref.shape[0]
    for r0 in range(0, rows, FFN_SLAB):
        gu = _dot(h_ref[r0:r0 + FFN_SLAB, :], w_ref[...])
        for c in range(groups):
            g = gu[:, (2 * c) * MXU_COLS:(2 * c + 1) * MXU_COLS]
            u = gu[:, (2 * c + 1) * MXU_COLS:(2 * c + 2) * MXU_COLS]
            o_ref[r0:r0 + FFN_SLAB, c * MXU_COLS:(c + 1) * MXU_COLS] = (_silu(g) * u).astype(BF16)


def _ffn_up(xh, gain, w_gu, layer, cast_weights):
    m = xh.shape[0]
    normed_input = gain is None
    bm = 2 * BM if normed_input else BM
    nj = D_FF // BN
    steps = (m // bm) * nj
    if w_gu.ndim == 3:
        w_spec = lambda off: pl.BlockSpec((None, D_MODEL, BN), lambda i, j: (layer, 0, _snake(i, j, nj) + off))
    else:
        w_spec = lambda off: pl.BlockSpec((D_MODEL, BN), lambda i, j: (0, _snake(i, j, nj) + off))
    cast_specs = [_cast_specs(w, l, _slab_rows(w, steps), nj) for w, l in cast_weights]
    row_specs = [pl.BlockSpec((bm, D_MODEL), lambda i, j: (i, 0))]
    scratch = [pltpu.VMEM((D_MODEL, 2 * BN), BF16)]
    operands = [xh]
    if not normed_input:
        row_specs.append(pl.BlockSpec((1, D_MODEL), lambda i, j: (0, 0)))
        scratch.append(pltpu.VMEM((bm, D_MODEL), BF16))
        operands.append(gain)
    return pl.pallas_call(
        functools.partial(_ffn_up_body, n_cast=len(cast_weights), normed_input=normed_input),
        grid=(m // bm, nj),
        in_specs=row_specs + [w_spec(0), w_spec(nj)] + [s[0] for s in cast_specs],
        out_specs=[pl.BlockSpec((bm, BN), lambda i, j: (i, _snake(i, j, nj)))] + [s[1] for s in cast_specs],
        out_shape=[jax.ShapeDtypeStruct((m, D_FF), BF16)] + [s[2] for s in cast_specs],
        scratch_shapes=scratch,
        compiler_params=_params(("arbitrary", "arbitrary"), 56),
        name="ffn_up",
    )(*operands, w_gu, w_gu, *[w for w, _ in cast_weights])


def _mm_rows_body(a_ref, w_ref, r_ref, *refs, scale, gain_use):
    y = r_ref[...] + scale * _dot(a_ref[...], w_ref[...])
    o_ref = refs[1] if gain_use else refs[0]
    if gain_use == "final":
        y = _rms(y, refs[0][...])
    elif gain_use == "next":
        refs[2][...] = _rms(y, refs[0][...]).astype(BF16)
    o_ref[...] = y


ROWS_BM = 512


def _mm_rows(a, w, res, scale, name, gain=None, gain_use=None):
    m, k = a.shape
    n = w.shape[1]
    row_out = pl.BlockSpec((ROWS_BM, n), lambda i: (i, 0))
    out_specs, out_shape = [row_out], [jax.ShapeDtypeStruct((m, n), F32)]
    if gain_use == "next":
        out_specs.append(row_out)
        out_shape.append(jax.ShapeDtypeStruct((m, n), BF16))
    gains = [] if gain is None else [gain]
    out = pl.pallas_call(
        functools.partial(_mm_rows_body, scale=scale, gain_use=gain_use),
        grid=(m // ROWS_BM,),
        in_specs=[
            pl.BlockSpec((ROWS_BM, k), lambda i: (i, 0)),
            pl.BlockSpec((k, n), lambda i: (0, 0), pipeline_mode=pl.Buffered(1)),
            pl.BlockSpec((ROWS_BM, n), lambda i: (i, 0)),
        ] + [pl.BlockSpec((1, n), lambda i: (0, 0))] * len(gains),
        out_specs=out_specs,
        out_shape=out_shape,
        compiler_params=_params(("arbitrary",), 60),
        name=name,
    )(a, w, res, *gains)
    return out if gain_use == "next" else out[0]


def _norm_mm_body(x_ref, g_ref, w_ref, o_ref, h_ref):
    @pl.when(pl.program_id(1) == 0)
    def _():
        h_ref[...] = _rms(x_ref[...], g_ref[...]).astype(BF16)

    o_ref[...] = _dot(h_ref[...], w_ref[...].astype(BF16)).astype(o_ref.dtype)


def _mem_kv(mem, gains, w):
    m, k = mem.shape
    layers, _, n = w.shape
    return pl.pallas_call(
        _norm_mm_body,
        grid=(layers, n // BN),
        in_specs=[
            pl.BlockSpec((m, k), lambda l, j: (0, 0)),
            pl.BlockSpec((None, 1, k), lambda l, j: (l, 0, 0)),
            pl.BlockSpec((None, k, BN), lambda l, j: (l, 0, j)),
        ],
        out_specs=pl.BlockSpec((None, m, BN), lambda l, j: (l, 0, j)),
        out_shape=jax.ShapeDtypeStruct((layers, m, n), BF16),
        scratch_shapes=[pltpu.VMEM((m, k), BF16)],
        compiler_params=_params(("arbitrary", "arbitrary"), 40),
        name="mem_kv",
    )(mem, gains, w)


EVEN_BN = 2 * BN
QKV_SLAB = 256


def _row_slabs(ref, slab):
    return [slice(r0, r0 + slab) for r0 in range(0, ref.shape[0], slab)]


def _even_in_body(h_ref, w_ref, wdt_ref, wo_ref, uv_ref, rest_ref, dt_ref, wo_out_ref, *, nj, n_uv):
    @pl.when(pl.program_id(1) == 0)
    def _():
        row = lax.broadcasted_iota(jnp.int32, (LANES, 1), 0)
        dt_ref[...] = _dot_nt(h_ref[...], jnp.where(row < SSD_HEADS, wdt_ref[...], 0.0).astype(BF16))

    wo_out_ref[...] = wo_ref[...].astype(BF16)
    col = _snake(pl.program_id(0), pl.program_id(1), nj)

    @pl.when(col < n_uv)
    def _():
        uv_ref[...] = _gelu(_dot_nt(h_ref[...], w_ref[...].astype(BF16))).astype(BF16)

    @pl.when(col >= n_uv)
    def _():
        rest_ref[...] = _dot_nt(h_ref[...], w_ref[...].astype(BF16)).astype(BF16)


def _even_in(h, w_in_t, w_out, layer):
    m = h.shape[0]
    bn = EVEN_BN
    n_uv, n_xbc, n_z = 2 * D_MODEL // bn, CONV_DIM // bn, D_MODEL // bn
    nj = n_uv + n_xbc + n_z
    z_first, xbc_first = n_uv, n_uv + n_z

    def w_block(i, j):
        c = _snake(i, j, nj)
        return jnp.where(c < n_uv, c, jnp.where(c < n_uv + n_xbc, xbc_first + c - n_uv, z_first + c - n_uv - n_xbc))

    uv_col = lambda i, j: jnp.minimum(_snake(i, j, nj), n_uv - 1)
    rest_col = lambda i, j: jnp.maximum(_snake(i, j, nj) - n_uv, 0)
    wo_in_spec, wo_out_spec, wo_shape = _cast_specs(w_out, layer, _slab_rows(w_out, (m // BM) * nj), nj)
    return pl.pallas_call(
        functools.partial(_even_in_body, nj=nj, n_uv=n_uv),
        grid=(m // BM, nj),
        in_specs=[pl.BlockSpec((BM, D_MODEL), lambda i, j: (i, 0)),
                  pl.BlockSpec((None, bn, D_MODEL), lambda i, j: (layer, w_block(i, j), 0)),
                  pl.BlockSpec((None, LANES, D_MODEL), lambda i, j: (layer, EVEN_MAIN // LANES, 0)),
                  wo_in_spec],
        out_specs=[pl.BlockSpec((BM, bn), lambda i, j: (i, uv_col(i, j))),
                   pl.BlockSpec((BM, bn), lambda i, j: (i, rest_col(i, j))),
                   pl.BlockSpec((BM, LANES), lambda i, j: (i, 0)),
                   wo_out_spec],
        out_shape=[jax.ShapeDtypeStruct((m, 2 * D_MODEL), BF16),
                   jax.ShapeDtypeStruct((m, CONV_DIM + D_MODEL), BF16),
                   jax.ShapeDtypeStruct((m, LANES), F32),
                   wo_shape],
        compiler_params=_params(("arbitrary", "arbitrary"), 56),
        name="even_in",
    )(h, w_in_t, w_in_t, w_out)


def _rope_table_body(pos_ref, invf_ref, cos_ref, sin_ref):
    ang = pos_ref[...].astype(F32) * invf_ref[...]
    cos_ref[...] = jnp.cos(ang)
    sin_ref[...] = jnp.sin(ang)


def _rope_table(pos_row, invf_col):
    shape = jax.ShapeDtypeStruct((ROT_HALF, pos_row.shape[1]), F32)
    return pl.pallas_call(_rope_table_body, out_shape=[shape, shape], name="rope_table")(pos_row, invf_col)


def _qkv_body(h_ref, w_ref, b_ref, cost_ref, sint_ref, wo_ref, o_ref, wo_out_ref,
              cos_ref, sn_ref, sp_ref, *, nj):
    j = pl.program_id(1)

    @pl.when(j == 0)
    def _():
        reps = LANES // ROT_HALF
        cos = jnp.concatenate([cost_ref[...]] * reps, axis=0).T
        sin = jnp.concatenate([sint_ref[...]] * reps, axis=0).T
        lane = lax.broadcasted_iota(jnp.int32, (1, LANES), 1) % HEAD_DIM
        first = lane < ROT_HALF
        second = (lane >= ROT_HALF) & (lane < ROT_DIM)
        cos_ref[...] = jnp.where(first | second, cos, 1.0)
        sn_ref[...] = jnp.where(first, -sin, 0.0)
        sp_ref[...] = jnp.where(second, sin, 0.0)

    wo_out_ref[...] = wo_ref[...].astype(BF16)

    col_block = _snake(pl.program_id(0), j, nj)
    tiles = o_ref.shape[1] // LANES

    def kind(tile):
        return "q" if tile < ATT_HEADS * HEAD_DIM // LANES else "k" if tile < (ODD_IN - KV_WIDTH) // LANES else "v"

    for rows in _row_slabs(o_ref, QKV_SLAB):
        acc = _dot(h_ref[rows, :], w_ref[...]) + b_ref[...]

        def rope(a):
            return (a * cos_ref[rows, :] + pltpu.roll(a, LANES - ROT_HALF, 1) * sn_ref[rows, :]
                    + pltpu.roll(a, ROT_HALF, 1) * sp_ref[rows, :])

        for t in range(tiles):
            a = acc[:, t * LANES:(t + 1) * LANES]
            kinds = [kind(jb * tiles + t) for jb in range(nj)]
            roped = rope(a) if set(kinds) != {"v"} else None
            by_kind = {"q": lambda: roped * ATT_SCALE, "k": lambda: roped, "v": lambda: a}
            val = by_kind[kinds[-1]]()
            for jb in range(nj - 2, -1, -1):
                if kinds[jb] != kinds[jb + 1]:
                    val = jnp.where(col_block <= jb, by_kind[kinds[jb]](), val)
            o_ref[rows, t * LANES:(t + 1) * LANES] = val.astype(BF16)


def _qkv(h, w_b, w_o, layer, b, cos_t, sin_t):
    m = h.shape[0]
    nj = 2
    bn = ODD_IN // nj
    wo_in_spec, wo_out_spec, wo_shape = _cast_specs(w_o, layer, 128, nj)
    return pl.pallas_call(
        functools.partial(_qkv_body, nj=nj),
        grid=(m // BM, nj),
        in_specs=[
            pl.BlockSpec((BM, D_MODEL), lambda i, j: (i, 0)),
            pl.BlockSpec((D_MODEL, bn), lambda i, j: (0, _snake(i, j, nj))),
            pl.BlockSpec((1, bn), lambda i, j: (0, _snake(i, j, nj))),
            pl.BlockSpec((ROT_HALF, BM), lambda i, j: (0, i)),
            pl.BlockSpec((ROT_HALF, BM), lambda i, j: (0, i)),
            wo_in_spec,
        ],
        out_specs=[pl.BlockSpec((BM, bn), lambda i, j: (i, _snake(i, j, nj))), wo_out_spec],
        out_shape=[jax.ShapeDtypeStruct((m, ODD_IN), BF16), wo_shape],
        scratch_shapes=[
            pltpu.VMEM((BM, LANES), F32),
            pltpu.VMEM((BM, LANES), F32),
            pltpu.VMEM((BM, LANES), F32),
        ],
        compiler_params=_params(("arbitrary", "arbitrary"), 56),
        name="qkv_rope",
    )(h, w_b, b, cos_t, sin_t, w_o)


def _split3(x):
    hi = x.astype(BF16)
    r1 = x - hi.astype(F32)
    mid = r1.astype(BF16)
    lo = (r1 - mid.astype(F32)).astype(BF16)
    return hi, mid, lo


def _even_mix_body(u_ref, v_ref, z0_ref, z1_ref, xbc_ref, dt_ref, lng_ref, lnb_ref, ws_ref, bs_ref, cw_ref, cb_ref,
                   dtb_ref, alog_ref, dskip_ref, snorm_ref, e3_ref, shift_ref, o_ref, state_ref, xx_ref):
    c = pl.program_id(0)
    q = CHUNK

    @pl.when(c == 0)
    def _():
        state_ref[...] = jnp.zeros_like(state_ref)
        xx_ref[pl.ds(0, q), :] = jnp.zeros((q, CONV_DIM), BF16)

    row = lax.broadcasted_iota(jnp.int32, (q, q), 0)
    col = lax.broadcasted_iota(jnp.int32, (q, q), 1)
    causal = col <= row

    for g in range(GROUPS):
        seg = slice(g * GDIM, (g + 1) * GDIM)
        vg = v_ref[:, seg].astype(F32)
        mu = jnp.mean(vg, axis=-1, keepdims=True)
        d = vg - mu
        var = jnp.mean(d * d, axis=-1, keepdims=True)
        vn = d * lax.rsqrt(var + EPS) * lng_ref[:, seg] + lnb_ref[:, seg]
        w = jnp.where(causal, ws_ref[g], 0.0).astype(BF16)
        s = _dot(w, vn.astype(BF16)) + bs_ref[:, g:g + 1]
        o_ref[:, seg] = (u_ref[:, seg].astype(F32) * s).astype(BF16)

    x_cur = xbc_ref[...]
    xx_ref[pl.ds(q, q), :] = x_cur
    shifted = _dot(shift_ref[...], xx_ref[...])
    conv = cb_ref[...] + cw_ref[SSD_CONV - 1:SSD_CONV, :] * x_cur.astype(F32)
    for k in range(SSD_CONV - 1):
        conv = conv + cw_ref[k:k + 1, :] * shifted[k * q:(k + 1) * q]
    xx_ref[pl.ds(q - CONV_TAIL, CONV_TAIL), :] = x_cur[q - CONV_TAIL:, :]
    xbc = _silu(conv)
    xs = xbc[:, :D_MODEL]

    dt = jax.nn.softplus(dt_ref[...] + dtb_ref[...])
    a = dt * (-jnp.exp(alog_ref[...]))
    tri = jnp.where(causal, 1.0, 0.0).astype(BF16)
    a_hi, a_mid, a_lo = _split3(a)
    acs = _dot(tri, a_hi) + _dot(tri, a_mid) + _dot(tri, a_lo)
    acs_t = acs.T
    both = jnp.concatenate([dt, acs], axis=0)
    b_hi, b_mid, b_lo = _split3(both)
    both_e = _dot(jnp.concatenate([b_hi, b_mid, b_lo], axis=1), e3_ref[...])
    dt_e = both_e[:q]
    acs_e = both_e[q:]
    last_e = acs_e[q - 1:q, :]
    xdt = xs * dt_e
    xdec = (xdt * jnp.exp(last_e - acs_e)).astype(BF16)
    xdt_b = xdt.astype(BF16)
    grow_e = jnp.exp(acs_e)
    chunk_decay = jnp.exp(last_e)

    lane = lax.broadcasted_iota(jnp.int32, (q, LANES), 1)
    lo_half = lane < HEAD_DIM
    gate = _silu(jnp.concatenate([z0_ref[...], z1_ref[...]], axis=1).astype(F32))

    for g in range(GROUPS):
        seg = slice(g * GDIM, (g + 1) * GDIM)
        b_g = xbc[:, D_MODEL + g * SSD_STATE:D_MODEL + (g + 1) * SSD_STATE].astype(BF16)
        c_g = xbc[:, D_MODEL + GROUPS * SSD_STATE + g * SSD_STATE:
                  D_MODEL + GROUPS * SSD_STATE + (g + 1) * SSD_STATE].astype(BF16)
        cb_causal = jnp.where(causal, _dot_nt(c_g, b_g), 0.0)
        y_off = _dot(c_g, state_ref[g].astype(BF16)) * grow_e[:, seg]
        pieces = []
        for p in range(GDIM // LANES):
            mats = []
            for hh in range(2):
                h = g * (GDIM // HEAD_DIM) + 2 * p + hh
                seg_ij = jnp.minimum(acs[:, h:h + 1] - acs_t[h:h + 1, :], 0.0)
                mats.append((cb_causal * jnp.exp(seg_ij)).astype(BF16))
            x2 = xdt_b[:, g * GDIM + p * LANES:g * GDIM + (p + 1) * LANES]
            zero = jnp.zeros_like(x2)
            rhs = jnp.concatenate([jnp.where(lo_half, x2, zero), jnp.where(lo_half, zero, x2)], axis=0)
            pieces.append(_dot(jnp.concatenate(mats, axis=1), rhs))
        y_diag = jnp.concatenate(pieces, axis=1)
        new_states = _dot_tn(b_g, xdec[:, seg])
        state_ref[g] = state_ref[g] * chunk_decay[:, seg] + new_states
        y = y_diag + y_off + xs[:, seg] * dskip_ref[:, seg]
        y = y * gate[:, seg]
        y = y * lax.rsqrt(jnp.mean(y * y, axis=-1, keepdims=True) + EPS)
        o_ref[:, D_MODEL + g * GDIM:D_MODEL + (g + 1) * GDIM] = (y * snorm_ref[:, seg]).astype(BF16)


def _even_mix(uv, rest, dt_raw, ln_g, ln_b, ws, bs_t, conv_w, conv_b, dt_bias, a_log, d_skip_e, ssd_norm, e3,
              shift):
    m = uv.shape[0]
    full = lambda shape: pl.BlockSpec(shape, lambda c: (0,) * len(shape))
    z_block = CONV_DIM // EVEN_BN
    return pl.pallas_call(
        _even_mix_body,
        grid=(m // CHUNK,),
        in_specs=[
            pl.BlockSpec((CHUNK, D_MODEL), lambda c: (c, 0)),
            pl.BlockSpec((CHUNK, D_MODEL), lambda c: (c, 1)),
            pl.BlockSpec((CHUNK, EVEN_BN), lambda c: (c, z_block)),
            pl.BlockSpec((CHUNK, EVEN_BN), lambda c: (c, z_block + 1)),
            pl.BlockSpec((CHUNK, CONV_DIM), lambda c: (c, 0)),
            pl.BlockSpec((CHUNK, LANES), lambda c: (c, 0)),
            full((1, D_MODEL)), full((1, D_MODEL)),
            full((GROUPS, CHUNK, CHUNK)), full((CHUNK, GROUPS)),
            full((SSD_CONV, CONV_DIM)), full((1, CONV_DIM)),
            full((1, LANES)), full((1, LANES)),
            full((1, D_MODEL)), full((1, D_MODEL)),
            full((3 * LANES, D_MODEL)),
            full(((SSD_CONV - 1) * CHUNK, 2 * CHUNK)),
        ],
        out_specs=pl.BlockSpec((CHUNK, 2 * D_MODEL), lambda c: (c, 0)),
        out_shape=jax.ShapeDtypeStruct((m, 2 * D_MODEL), BF16),
        scratch_shapes=[
            pltpu.VMEM((GROUPS, SSD_STATE, GDIM), F32),
            pltpu.VMEM((2 * CHUNK, CONV_DIM), BF16),
        ],
        compiler_params=_params(("arbitrary",), 48),
        name="even_mix",
    )(uv, uv, rest, rest, rest, dt_raw, ln_g, ln_b, ws, bs_t, conv_w, conv_b, dt_bias, a_log, d_skip_e, ssd_norm,
      e3, shift)


def _swa_body(sink_ref, q_ref, kv_ref, kvp_ref, o_ref):
    n = pl.program_id(0)
    w = CHUNK
    row = lax.broadcasted_iota(jnp.int32, (w, w), 0)
    col = lax.broadcasted_iota(jnp.int32, (w, w), 1)
    own = col <= row
    lo_half = lax.broadcasted_iota(jnp.int32, (w, LANES), 1) < HEAD_DIM
    prev_bias = jnp.where(n > 0, 0.0, -jnp.inf)

    def head_tiles(ref, base, k):
        t = ref[:, base + (k // 2) * LANES:base + (k // 2 + 1) * LANES].astype(F32)
        r = pltpu.roll(t, HEAD_DIM, 1)
        return (t, r) if k % 2 == 0 else (r, t)

    for k in range(GROUPS):
        k_lo, k_hi = head_tiles(kv_ref, 0, k)
        kp_lo, kp_hi = head_tiles(kvp_ref, 0, k)
        v_lo, v_hi = head_tiles(kv_ref, KV_WIDTH, k)
        vp_lo, vp_hi = head_tiles(kvp_ref, KV_WIDTH, k)
        kk = jnp.where(lo_half, k_lo, k_hi).astype(BF16)
        kkp = jnp.where(lo_half, kp_lo, kp_hi).astype(BF16)
        out = []
        for parity in range(2):
            if parity == 0:
                vv = jnp.concatenate([jnp.where(lo_half, v_lo, 1.0), jnp.where(lo_half, vp_lo, 1.0)], axis=0)
            else:
                vv = jnp.concatenate([jnp.where(lo_half, 1.0, v_hi), jnp.where(lo_half, 1.0, vp_hi)], axis=0)
            lhs = []
            for p in range(ATT_REP // 2):
                q2 = q_ref[:, k * GDIM + p * LANES:k * GDIM + (p + 1) * LANES]
                zero = jnp.zeros_like(q2)
                lhs.append(jnp.where(lo_half, q2, zero) if parity == 0 else jnp.where(lo_half, zero, q2))
            lhs = jnp.concatenate(lhs, axis=0)
            s_own = _dot_nt(lhs, kk)
            s_prev = _dot_nt(lhs, kkp)
            probs, esink = [], []
            for p in range(ATT_REP // 2):
                sink = sink_ref[k * ATT_REP + 2 * p + parity]
                s = jnp.where(own, s_own[p * w:(p + 1) * w], s_prev[p * w:(p + 1) * w] + prev_bias)
                mx = jnp.maximum(jnp.max(s, axis=-1, keepdims=True), sink)
                e = jnp.exp(s - mx)
                probs.append(jnp.concatenate([jnp.where(own, e, 0.0).astype(BF16),
                                              jnp.where(own, 0.0, e).astype(BF16)], axis=1))
                esink.append(jnp.exp(sink - mx))
            o = _dot(jnp.concatenate(probs, axis=0), vv.astype(BF16))
            out.append((o, esink))
        for p in range(ATT_REP // 2):
            o_even = out[0][0][p * w:(p + 1) * w]
            o_odd = out[1][0][p * w:(p + 1) * w]
            num = jnp.where(lo_half, o_even, o_odd)
            den = pltpu.roll(jnp.where(lo_half, o_odd, o_even), HEAD_DIM, 1)
            den = den + jnp.where(lo_half, out[0][1][p], out[1][1][p])
            o_ref[:, k * GDIM + p * LANES:k * GDIM + (p + 1) * LANES] = (num / den).astype(BF16)


def _swa(qkv, sinks):
    m = qkv.shape[0]
    kv_block = ATT_HEADS * HEAD_DIM // (2 * KV_WIDTH)
    return pl.pallas_call(
        _swa_body,
        grid=(m // CHUNK,),
        in_specs=[
            pl.BlockSpec(memory_space=pltpu.SMEM),
            pl.BlockSpec((CHUNK, D_MODEL), lambda n: (n, 0)),
            pl.BlockSpec((CHUNK, 2 * KV_WIDTH), lambda n: (n, kv_block)),
            pl.BlockSpec((CHUNK, 2 * KV_WIDTH), lambda n: (jnp.maximum(n - 1, 0), kv_block)),
        ],
        out_specs=pl.BlockSpec((CHUNK, D_MODEL), lambda n: (n, 0)),
        out_shape=jax.ShapeDtypeStruct((m, D_MODEL), BF16),
        compiler_params=_params(("parallel",), 32),
        name="swa",
    )(sinks, qkv, qkv, qkv)


def _xattn_body(x_ref, g_ref, wq_ref, kv_ref, wo_ref, gn_ref, o_ref, hn_ref, wq_b, wo_b):
    @pl.when(pl.program_id(0) == 0)
    def _():
        wq_b[...] = wq_ref[...].astype(BF16)
        wo_b[...] = wo_ref[...].astype(BF16)

    x = x_ref[...]
    h = _rms(x, g_ref[...]).astype(BF16)
    q = _dot(h, wq_b[...]).astype(BF16)
    outs = []
    for hd in range(X_HEADS):
        seg = slice(hd * X_HEAD_DIM, (hd + 1) * X_HEAD_DIM)
        k = kv_ref[:, seg]
        v = kv_ref[:, X_WIDTH + hd * X_HEAD_DIM:X_WIDTH + (hd + 1) * X_HEAD_DIM]
        s = _dot_nt(q[:, seg], k) * X_SCALE
        e = jnp.exp(s - jnp.max(s, axis=-1, keepdims=True))
        o = _dot(e.astype(BF16), v) * (1.0 / jnp.sum(e, axis=-1, keepdims=True))
        outs.append(o.astype(BF16))
    y = x + _dot(jnp.concatenate(outs, axis=1), wo_b[...])
    o_ref[...] = y
    hn_ref[...] = _rms(y, gn_ref[...]).astype(BF16)


def _xattn(x, gain, w_q, kv, w_o, layer, next_gain):
    m = x.shape[0]
    bm = 512
    row = pl.BlockSpec((bm, D_MODEL), lambda i: (i, 0))
    vec = pl.BlockSpec((1, D_MODEL), lambda i: (0, 0))
    return pl.pallas_call(
        _xattn_body,
        grid=(m // bm,),
        in_specs=[
            row, vec,
            pl.BlockSpec((None, D_MODEL, X_WIDTH), lambda i: (layer, 0, 0)),
            pl.BlockSpec((None, N_MEM, 2 * X_WIDTH), lambda i: (layer, 0, 0)),
            pl.BlockSpec((None, X_WIDTH, D_MODEL), lambda i: (layer, 0, 0)),
            vec,
        ],
        out_specs=[row, row],
        out_shape=[jax.ShapeDtypeStruct((m, D_MODEL), F32), jax.ShapeDtypeStruct((m, D_MODEL), BF16)],
        scratch_shapes=[pltpu.VMEM((D_MODEL, X_WIDTH), BF16), pltpu.VMEM((X_WIDTH, D_MODEL), BF16)],
        compiler_params=_params(("arbitrary",), 48),
        name="xattn",
    )(x, gain, w_q, kv, w_o, next_gain)


def _ffn(x, h, gain, w_gu, w_down, layer, name, more_casts, out_gain, gain_use):
    casts = [(w_down, layer)] + list(more_casts)
    if h is None:
        act, w_down_b, *copies = _ffn_up(x, gain.reshape(1, -1), w_gu, layer, casts)
    else:
        act, w_down_b, *copies = _ffn_up(h, None, w_gu, layer, casts)
    out = _mm_rows(act, w_down_b, x, 0.5, name, out_gain.reshape(1, -1), gain_use)
    y, h_next = out if gain_use == "next" else (out, None)
    return y, h_next, copies


def _pad_lanes(v):
    return jnp.pad(v.reshape(1, -1), ((0, 0), (0, LANES - v.shape[-1])))


def kernel(x, mem, positions, norm_ffn1, w_ffn1_gu, w_ffn1_down, norm_mix, w_in_even, gm_ln_g, gm_ln_b, gm_ws, gm_bs, conv_w, conv_b, dt_bias, a_log, d_skip, ssd_norm, w_out_even, w_qkv, b_qkv, sinks, w_o_odd, norm_xq, norm_mem, w_xq, w_xkv, w_xo, norm_ffn2, w_ffn2_gu, w_ffn2_down, final_norm):
    bsz, seq, d = x.shape
    assert (bsz, seq, d) == (1, SEQ, D_MODEL)
    xr = x.reshape(seq, d)
    memr = mem.reshape(N_MEM, d)
    inv_freq = ROPE_THETA ** (-jnp.arange(0, ROT_DIM, 2, dtype=F32) / ROT_DIM)
    cos_t, sin_t = _rope_table(positions.reshape(1, seq), inv_freq.reshape(ROT_HALF, 1))
    head_of_lane = jnp.arange(D_MODEL, dtype=jnp.int32) // HEAD_DIM
    e1 = (jnp.arange(LANES, dtype=jnp.int32)[:, None] == head_of_lane[None, :]).astype(BF16)
    e3 = jnp.concatenate([e1, e1, e1], axis=0)
    sel_row = jnp.arange((SSD_CONV - 1) * CHUNK, dtype=jnp.int32)[:, None]
    sel_col = jnp.arange(2 * CHUNK, dtype=jnp.int32)[None, :]
    shift = (sel_col == CHUNK + sel_row % CHUNK - (SSD_CONV - 1) + sel_row // CHUNK).astype(BF16)

    w_in_t = jnp.swapaxes(w_in_even, 1, 2)

    kv_all = _mem_kv(memr, norm_mem.reshape(DEPTH, 1, d), w_xkv)
    w_gu1 = w_ffn1_gu
    h = None
    for i in range(DEPTH):
        j = i // 2
        xr, h, (w_gu2,) = _ffn(xr, h, norm_ffn1[i], w_gu1, w_ffn1_down, i, "ffn1_down",
                               [(w_ffn2_gu, i)], norm_mix[i], "next")
        if i % 2 == 0:
            uv, rest, dt_raw, w_out_b = _even_in(h, w_in_t, w_out_even, j)
            mix = _even_mix(
                uv, rest, dt_raw, gm_ln_g[j].reshape(1, -1), gm_ln_b[j].reshape(1, -1), gm_ws[j], gm_bs[j].T,
                conv_w[j], conv_b[j].reshape(1, -1), _pad_lanes(dt_bias[j]), _pad_lanes(a_log[j]),
                jnp.repeat(d_skip[j], HEAD_DIM).reshape(1, -1), ssd_norm[j].reshape(1, -1), e3, shift)
            xr = _mm_rows(mix, w_out_b, xr, 1.0, "even_out")
        else:
            qkv, w_o_b = _qkv(h, w_qkv_b, w_o_odd, j, b_qkv[j].reshape(1, -1), cos_t, sin_t)
            att = _swa(qkv, sinks[j])
            xr = _mm_rows(att, w_o_b, xr, 1.0, "odd_out")
        xr, h = _xattn(xr, norm_xq[i].reshape(1, -1), w_xq, kv_all, w_xo, i, norm_ffn2[i].reshape(1, -1))
        last = i + 1 == DEPTH
        ahead = []
        if not last:
            ahead.append((w_ffn1_gu, i + 1))
            if (i + 1) % 2 == 1:
                ahead.append((w_qkv, (i + 1) // 2))
        xr, h, copies = _ffn(xr, h, norm_ffn2[i], w_gu2, w_ffn2_down, i, "ffn2_down", ahead,
                             final_norm if last else norm_ffn1[i + 1], "final" if last else "next")
        if copies:
            w_gu1 = copies[0]
            w_qkv_b = copies[1] if len(copies) > 1 else None
    return xr.reshape(bsz, seq, d)
```
